```python
import math
import jax, jax.numpy as jnp
from jax import lax
import numpy as np

D_MODEL = 2048
BATCH = 16
SEQ = 256
DEPTH = 2
DEC_BATCH = 2
DEC_SEQ = 2048
PAST_LEN = 512

GRID_W = 64
ROPE_THETA = 10000.0
EPS = 1e-6
MLA_HEADS = 8
Q_RANK = 512
KV_RANK = 256
NOPE_DIM = 128
ROPE_DIM = 64
V_DIM = 128
MLA_WIDTH = MLA_HEADS * V_DIM
ATTN_BLOCK = 128
SSM_HEADS = 16
SSM_HEADDIM = 64
SSM_INNER = SSM_HEADS * SSM_HEADDIM
SSM_GROUPS = 2
SSM_STATE = 128
SSM_CONV = 5
SSM_CHUNK = 128
SSM_XBC = SSM_INNER + 2 * SSM_GROUPS * SSM_STATE
GM_WIDTH = 1024
GM_GROUPS = 4
GM_CHUNK = 128
N_BRANCH = 3
N_IN = Q_RANK + KV_RANK + ROPE_DIM + SSM_INNER + SSM_XBC + 2 * SSM_HEADS + 2 * GM_WIDTH + N_BRANCH * D_MODEL
N_EXPERTS = 16
EXPERT_FF = 1024
EC_CAPACITY = 2

kernel_name = 'hybrid_mla_ssd_gmlp_ec_diffusion_step'


def _rmsnorm(x, g):
    xf = x.astype(jnp.float32)
    y = xf * lax.rsqrt(jnp.mean(xf * xf, axis=-1, keepdims=True) + EPS)
    return y.astype(x.dtype) * g


def _split_in(z):
    widths = (Q_RANK, KV_RANK, ROPE_DIM, SSM_INNER, SSM_XBC, 2 * SSM_HEADS, 2 * GM_WIDTH, N_BRANCH * D_MODEL)
    idx, s = [], 0
    for w in widths[:-1]:
        s += w
        idx.append(s)
    return jnp.split(z, idx, axis=-1)


def _axial_rope(x, cos, sin):
    shp = x.shape
    xs = x.reshape(shp[:-1] + (2, 2, ROPE_DIM // 4))
    x1, x2 = xs[..., 0, :], xs[..., 1, :]
    out = jnp.stack([x1 * cos - x2 * sin, x1 * sin + x2 * cos], axis=-2)
    return out.reshape(shp)


def _attention(q, k, v):
    b, lq, h, d = q.shape
    nb = lq // ATTN_BLOCK
    qb = q.reshape(b, nb, ATTN_BLOCK, h, d).transpose(1, 0, 2, 3, 4)
    scale = 1.0 / math.sqrt(d)

    def block(qi):
        s = jnp.einsum('bqhd,bkhd->bhqk', qi, k).astype(jnp.float32) * scale
        w = jax.nn.softmax(s, axis=-1).astype(v.dtype)
        return jnp.einsum('bhqk,bkhd->bqhd', w, v)

    out = lax.map(block, qb)
    return out.transpose(1, 0, 2, 3, 4).reshape(b, lq, h, v.shape[-1])


def _mla_kv(ckv, kr, w_ukv):
    b, L, _ = ckv.shape
    kv = (ckv @ w_ukv).reshape(b, L, MLA_HEADS, NOPE_DIM + V_DIM)
    k_nope, v = kv[..., :NOPE_DIM], kv[..., NOPE_DIM:]
    k = jnp.concatenate([k_nope, jnp.broadcast_to(kr[:, :, None, :], (b, L, MLA_HEADS, ROPE_DIM))], axis=-1)
    return k, v


def _dwconv(x, w, bias):
    ch = x.shape[-1]
    y = lax.conv_general_dilated(x, w[:, None, :].astype(x.dtype), window_strides=(1,),
                                 padding=[(SSM_CONV // 2, SSM_CONV // 2)],
                                 dimension_numbers=('NWC', 'WIO', 'NWC'), feature_group_count=ch)
    return y + bias


def _ssd(x, dt, a, bm, cm, h0):
    f32 = jnp.float32
    b, L, H, P = x.shape
    nc = L // SSM_CHUNK
    rep = H // SSM_GROUPS
    xc = x.astype(f32).reshape(b, nc, SSM_CHUNK, H, P)
    bc = jnp.repeat(bm.astype(f32), rep, axis=2).reshape(b, nc, SSM_CHUNK, H, SSM_STATE)
    cc = jnp.repeat(cm.astype(f32), rep, axis=2).reshape(b, nc, SSM_CHUNK, H, SSM_STATE)
    dtc = dt.reshape(b, nc, SSM_CHUNK, H)
    acum = jnp.cumsum(dtc * a, axis=2)
    seg = acum[:, :, :, None, :] - acum[:, :, None, :, :]
    lower = jnp.tril(jnp.ones((SSM_CHUNK, SSM_CHUNK), dtype=bool))[None, None, :, :, None]
    lmat = jnp.exp(jnp.where(lower, seg, -jnp.inf))
    cb = jnp.einsum('bcihn,bcjhn->bcijh', cc, bc)
    y_diag = jnp.einsum('bcijh,bcjhp->bcihp', cb * lmat * dtc[:, :, None, :, :], xc)
    decay_end = jnp.exp(acum[:, :, -1:, :] - acum) * dtc
    states = jnp.einsum('bcjhn,bcjh,bcjhp->bchpn', bc, decay_end, xc)
    chunk_decay = jnp.exp(acum[:, :, -1, :])

    def step(h, inp):
        dec, st = inp
        return dec[:, :, None, None] * h + st, h

    h_fin, h_in = lax.scan(step, h0.astype(f32), (jnp.moveaxis(chunk_decay, 1, 0), jnp.moveaxis(states, 1, 0)))
    h_in = jnp.moveaxis(h_in, 0, 1)
    y_off = jnp.einsum('bcihn,bchpn,bcih->bcihp', cc, h_in, jnp.exp(acum))
    return (y_diag + y_off).reshape(b, L, H, P), h_fin


def _token_mixer(h, p, rope, ctx_ckv, ctx_kr, h0):
    f32 = jnp.float32
    b, L, _ = h.shape
    cq, ckv_raw, kr, z, xbc, dt_raw, gm, gate_logits = _split_in(h @ p['w_in'])

    ckv = _rmsnorm(ckv_raw, p['g_kvn'])
    q = (_rmsnorm(cq, p['g_qn']) @ p['w_uq']).reshape(b, L, MLA_HEADS, NOPE_DIM + ROPE_DIM)
    q_nope, q_rope = q[..., :NOPE_DIM], q[..., NOPE_DIM:]
    kr_pos = kr
    if rope is not None:
        cos, sin = rope
        q_rope = _axial_rope(q_rope, cos[:, None], sin[:, None])
        kr_pos = _axial_rope(kr, cos, sin)
    k, v = _mla_kv(ckv, kr_pos, p['w_ukv'])
    if ctx_ckv is not None:
        k_c, v_c = _mla_kv(ctx_ckv, ctx_kr, p['w_ukv'])
        k = jnp.concatenate([k_c, k], axis=1)
        v = jnp.concatenate([v_c, v], axis=1)
    attn = _attention(jnp.concatenate([q_nope, q_rope], axis=-1), k, v).reshape(b, L, MLA_WIDTH)

    xbc = jax.nn.silu(_dwconv(xbc, p['conv_w'], p['conv_b']))
    gn = SSM_GROUPS * SSM_STATE
    xs = xbc[..., :SSM_INNER].reshape(b, L, SSM_HEADS, SSM_HEADDIM)
    bm = xbc[..., SSM_INNER:SSM_INNER + gn].reshape(b, L, SSM_GROUPS, SSM_STATE)
    cm = xbc[..., SSM_INNER + gn:].reshape(b, L, SSM_GROUPS, SSM_STATE)
    dt = jax.nn.softplus(dt_raw.astype(f32).reshape(b, L, 2, SSM_HEADS) + p['dt_bias'].astype(f32))
    a = -jnp.exp(p['a_log'].astype(f32))
    if h0 is None:
        h0 = jnp.zeros((b, 2, SSM_HEADS, SSM_HEADDIM, SSM_STATE), f32)
    y_f, hf_f = _ssd(xs, dt[:, :, 0], a[0], bm, cm, h0[:, 0])
    y_b, hf_b = _ssd(jnp.flip(xs, 1), jnp.flip(dt[:, :, 1], 1), a[1], jnp.flip(bm, 1), jnp.flip(cm, 1), h0[:, 1])
    y = y_f + jnp.flip(y_b, 1) + p['d_skip'].astype(f32)[:, None] * xs.astype(f32)
    y = y.astype(h.dtype).reshape(b, L, SSM_INNER)
    ssm = _rmsnorm(y * jax.nn.silu(z), p['g_ssm'])

    gm = jax.nn.gelu(gm)
    u, vg = gm[..., :GM_WIDTH], gm[..., GM_WIDTH:]
    vg = _rmsnorm(vg, p['g_gv']).reshape(b, L // GM_CHUNK, GM_CHUNK, GM_GROUPS, GM_WIDTH // GM_GROUPS)
    sv = jnp.einsum('gij,bcjgd->bcigd', p['w_sp'], vg) + p['b_sp'].T[None, None, :, :, None]
    gmo = u * sv.reshape(b, L, GM_WIDTH)

    ga, gs, gc = jnp.split(jax.nn.sigmoid(gate_logits), N_BRANCH, axis=-1)
    merged = ga * (attn @ p['w_br_attn']) + gs * (ssm @ p['w_br_ssm']) + gc * (gmo @ p['w_br_gmlp'])
    return merged @ p['w_out'], ckv, kr, jnp.stack([hf_f, hf_b], axis=1).astype(h.dtype)


def _ec_moe(h, w_router, w_gate, w_up, w_down):
    b, n, d = h.shape
    cap = EC_CAPACITY * n // N_EXPERTS
    aff = jax.nn.softmax((h @ w_router).astype(jnp.float32), axis=-1)
    g, idx = lax.top_k(jnp.swapaxes(aff, 1, 2), cap)
    xs = jax.vmap(lambda hb, ib: hb[ib])(h, idx)
    hid = jax.nn.silu(jnp.einsum('becd,edf->becf', xs, w_gate)) * jnp.einsum('becd,edf->becf', xs, w_up)
    out = jnp.einsum('becf,efd->becd', hid, w_down) * g[..., None].astype(h.dtype)
    return jax.vmap(lambda ob, ib: jnp.zeros((n, d), h.dtype).at[ib.reshape(-1)].add(ob.reshape(-1, d)))(out, idx)


def _layer(x, mod, p, rope, ctx_ckv, ctx_kr, h0):
    sh1, sc1, gt1, sh2, sc2, gt2 = jnp.split(mod, 6, axis=-1)
    h = _rmsnorm(x, p['g_norm1']) * (1 + sc1) + sh1
    mix, ckv, kr, hfin = _token_mixer(h, p, rope, ctx_ckv, ctx_kr, h0)
    x = x + gt1 * mix
    h = _rmsnorm(x, p['g_norm2']) * (1 + sc2) + sh2
    x = x + gt2 * _ec_moe(h, p['w_router'], p['w_gate'], p['w_up'], p['w_down'])
    return x, ckv, kr, hfin


def setup_inputs(seed: int = 0) -> dict:
    key = jax.random.key(seed)
    ks = iter(jax.random.split(key, 40))
    f32 = jnp.float32
    D = D_MODEL

    def nrm(shape, scale):
        return jax.random.normal(next(ks), shape, f32) * scale

    def gain(shape):
        return 1.0 + nrm(shape, 0.02)

    dt0 = jnp.exp(jax.random.uniform(next(ks), (DEPTH, 2, SSM_HEADS), f32, math.log(1e-3), math.log(1e-1)))
    return {
        'x_prompt': nrm((BATCH, SEQ, D), 1.0),
        'x_sample': nrm((DEC_BATCH, DEC_SEQ, D), 1.0),
        'c': nrm((DEC_BATCH, D), 1.0),
        'cache_ckv': nrm((DEC_BATCH, DEPTH, PAST_LEN, KV_RANK), 1.0),
        'cache_krope': nrm((DEC_BATCH, DEPTH, PAST_LEN, ROPE_DIM), 1.0),
        'state_ssm': nrm((DEC_BATCH, DEPTH, 2, SSM_HEADS, SSM_HEADDIM, SSM_STATE), 0.5),
        'c_ctx': nrm((D,), 1.0),
        'w_mod': nrm((DEPTH, D, 6 * D), 0.5 * D ** -0.5),
        'b_mod': nrm((DEPTH, 6 * D), 0.01),
        'g_norm1': gain((DEPTH, D)),
        'g_norm2': gain((DEPTH, D)),
        'w_in': nrm((DEPTH, D, N_IN), D ** -0.5),
        'g_qn': gain((DEPTH, Q_RANK)),
        'w_uq': nrm((DEPTH, Q_RANK, MLA_HEADS * (NOPE_DIM + ROPE_DIM)), Q_RANK ** -0.5),
        'g_kvn': gain((DEPTH, KV_RANK)),
        'w_ukv': nrm((DEPTH, KV_RANK, MLA_HEADS * (NOPE_DIM + V_DIM)), KV_RANK ** -0.5),
        'conv_w': nrm((DEPTH, SSM_CONV, SSM_XBC), SSM_CONV ** -0.5),
        'conv_b': nrm((DEPTH, SSM_XBC), 0.01),
        'dt_bias': dt0 + jnp.log(-jnp.expm1(-dt0)),
        'a_log': jnp.log(jax.random.uniform(next(ks), (DEPTH, 2, SSM_HEADS), f32, 1.0, 16.0)),
        'd_skip': gain((DEPTH, SSM_HEADS)),
        'g_ssm': gain((DEPTH, SSM_INNER)),
        'g_gv': gain((DEPTH, GM_WIDTH)),
        'w_sp': nrm((DEPTH, GM_GROUPS, GM_CHUNK, GM_CHUNK), GM_CHUNK ** -0.5),
        'b_sp': 1.0 + nrm((DEPTH, GM_GROUPS, GM_CHUNK), 0.01),
        'w_br_attn': nrm((DEPTH, MLA_WIDTH, D), MLA_WIDTH ** -0.5),
        'w_br_ssm': nrm((DEPTH, SSM_INNER, D), SSM_INNER ** -0.5),
        'w_br_gmlp': nrm((DEPTH, GM_WIDTH, D), GM_WIDTH ** -0.5),
        'w_out': nrm((DEPTH, D, D), D ** -0.5),
        'w_router': nrm((DEPTH, D, N_EXPERTS), D ** -0.5),
        'w_gate': nrm((DEPTH, N_EXPERTS, D, EXPERT_FF), D ** -0.5),
        'w_up': nrm((DEPTH, N_EXPERTS, D, EXPERT_FF), D ** -0.5),
        'w_down': nrm((DEPTH, N_EXPERTS, EXPERT_FF, D), EXPERT_FF ** -0.5),
        'g_final': gain((D,)),
    }


def reference(x_prompt, x_sample, c, cache_ckv, cache_krope, state_ssm, c_ctx, w_mod, b_mod,
              g_norm1, g_norm2, w_in, g_qn, w_uq, g_kvn, w_ukv, conv_w, conv_b, dt_bias, a_log,
              d_skip, g_ssm, g_gv, w_sp, b_sp, w_br_attn, w_br_ssm, w_br_gmlp, w_out, w_router,
              w_gate, w_up, w_down, g_final):
    n_lat = x_sample.shape[1]
    rows = n_lat // GRID_W
    row = jnp.repeat(jnp.arange(rows), GRID_W).astype(jnp.float32)
    col = jnp.tile(jnp.arange(GRID_W), rows).astype(jnp.float32)
    nf = ROPE_DIM // 4
    freqs = jnp.power(ROPE_THETA, -jnp.arange(nf, dtype=jnp.float32) / nf)
    ang = jnp.stack([row[:, None] * freqs, col[:, None] * freqs], axis=1)
    rope = (jnp.cos(ang).astype(x_sample.dtype), jnp.sin(ang).astype(x_sample.dtype))

    xp, xs = x_prompt, x_sample
    ckvs, krs, sts = [], [], []
    for l in range(DEPTH):
        p = {
            'g_norm1': g_norm1[l], 'g_norm2': g_norm2[l], 'w_in': w_in[l], 'g_qn': g_qn[l],
            'w_uq': w_uq[l], 'g_kvn': g_kvn[l], 'w_ukv': w_ukv[l], 'conv_w': conv_w[l],
            'conv_b': conv_b[l], 'dt_bias': dt_bias[l], 'a_log': a_log[l], 'd_skip': d_skip[l],
            'g_ssm': g_ssm[l], 'g_gv': g_gv[l], 'w_sp': w_sp[l], 'b_sp': b_sp[l],
            'w_br_attn': w_br_attn[l], 'w_br_ssm': w_br_ssm[l], 'w_br_gmlp': w_br_gmlp[l],
            'w_out': w_out[l], 'w_router': w_router[l], 'w_gate': w_gate[l], 'w_up': w_up[l],
            'w_down': w_down[l],
        }
        mod_ctx = jax.nn.silu(c_ctx) @ w_mod[l] + b_mod[l]
        mod_lat = (jax.nn.silu(c) @ w_mod[l] + b_mod[l])[:, None, :]
        xp, ckv_l, kr_l, st_l = _layer(xp, mod_ctx, p, None, None, None, None)
        ckvs.append(ckv_l)
        krs.append(kr_l)
        sts.append(st_l)
        xs, _, _, _ = _layer(xs, mod_lat, p, rope, cache_ckv[:, l], cache_krope[:, l], state_ssm[:, l])

    y_prompt = _rmsnorm(xp, g_final)
    y_sample = _rmsnorm(xs, g_final)
    new_cache_ckv = jnp.stack(ckvs, axis=1)
    new_cache_krope = jnp.stack(krs, axis=1)
    new_state_ssm = jnp.stack(sts, axis=1)
    return (y_prompt, y_sample, new_cache_ckv, new_cache_krope, new_state_ssm)
```

```python
import functools
import math

import jax
import jax.numpy as jnp
from jax import lax
from jax.experimental import pallas as pl
from jax.experimental.pallas import tpu as pltpu

D_MODEL = 2048
DEPTH = 2
GRID_W = 64
ROPE_THETA = 10000.0
EPS = 1e-6
MLA_HEADS = 8
Q_RANK = 512
KV_RANK = 256
NOPE_DIM = 128
ROPE_DIM = 64
V_DIM = 128
MLA_WIDTH = MLA_HEADS * V_DIM
ATTN_BLOCK = 128
SSM_HEADS = 16
SSM_HEADDIM = 64
SSM_INNER = SSM_HEADS * SSM_HEADDIM
SSM_GROUPS = 2
SSM_STATE = 128
SSM_CONV = 5
SSM_CHUNK = 128
SSM_XBC = SSM_INNER + 2 * SSM_GROUPS * SSM_STATE
GM_WIDTH = 1024
GM_GROUPS = 4
GM_CHUNK = 128
N_BRANCH = 3
N_EXPERTS = 16
EXPERT_FF = 1024
EC_CAPACITY = 2

V7X_VMEM_LIMIT_BYTES = 56 * 1024 * 1024

BF16 = jnp.bfloat16
F32 = jnp.float32


def _mm_body(x_ref, w_ref, o_ref):
    o_ref[...] = jnp.dot(x_ref[...].astype(BF16), w_ref[...].astype(BF16),
                         preferred_element_type=F32).astype(o_ref.dtype)


def _matmul(x, w, out_dtype=F32, tm=512, tn=512, name="matmul"):
    m, k = x.shape
    k2, n = w.shape
    assert k == k2 and m % tm == 0
    tn = min(tn, n)
    return pl.pallas_call(
        _mm_body,
        grid=(m // tm, pl.cdiv(n, tn)),
        in_specs=[pl.BlockSpec((tm, k), lambda i, j: (i, 0)),
                  pl.BlockSpec((k, tn), lambda i, j: (0, j))],
        out_specs=pl.BlockSpec((tm, tn), lambda i, j: (i, j)),
        out_shape=jax.ShapeDtypeStruct((m, n), out_dtype),
        compiler_params=pltpu.CompilerParams(
            dimension_semantics=("parallel", "arbitrary"),
            vmem_limit_bytes=V7X_VMEM_LIMIT_BYTES),
        name=name,
    )(x, w)


def _ffn_body(x_ref, wg_ref, wu_ref, wd_ref, o_ref):
    f = pl.program_id(1)
    x = x_ref[0]
    gate = jnp.dot(x, wg_ref[0].astype(BF16), preferred_element_type=F32)
    up = jnp.dot(x, wu_ref[0].astype(BF16), preferred_element_type=F32)
    hid = (gate * jax.nn.sigmoid(gate) * up).astype(BF16)
    part = jnp.dot(hid, wd_ref[0].astype(BF16), preferred_element_type=F32)

    @pl.when(f == 0)
    def _():
        o_ref[0] = part

    @pl.when(f > 0)
    def _():
        o_ref[0] += part


def _expert_ffn(xs, w_gate, w_up, w_down, tf=256):
    e, r, d = xs.shape
    ff = w_gate.shape[-1]
    return pl.pallas_call(
        _ffn_body,
        grid=(e, ff // tf),
        in_specs=[pl.BlockSpec((1, r, d), lambda i, f: (i, 0, 0)),
                  pl.BlockSpec((1, d, tf), lambda i, f: (i, 0, f)),
                  pl.BlockSpec((1, d, tf), lambda i, f: (i, 0, f)),
                  pl.BlockSpec((1, tf, d), lambda i, f: (i, f, 0))],
        out_specs=pl.BlockSpec((1, r, d), lambda i, f: (i, 0, 0)),
        out_shape=jax.ShapeDtypeStruct((e, r, d), F32),
        compiler_params=pltpu.CompilerParams(
            dimension_semantics=("parallel", "arbitrary"),
            vmem_limit_bytes=V7X_VMEM_LIMIT_BYTES),
        name="expert_ffn",
    )(xs, w_gate, w_up, w_down)


def _rmsnorm(x, g):
    xf = x.astype(F32)
    y = xf * lax.rsqrt(jnp.mean(xf * xf, axis=-1, keepdims=True) + EPS)
    return y.astype(x.dtype) * g


def _split_in(z):
    widths = (Q_RANK, KV_RANK, ROPE_DIM, SSM_INNER, SSM_XBC, 2 * SSM_HEADS, 2 * GM_WIDTH, N_BRANCH * D_MODEL)
    idx, s = [], 0
    for w in widths[:-1]:
        s += w
        idx.append(s)
    return jnp.split(z, idx, axis=-1)


def _axial_rope(x, cos, sin):
    shp = x.shape
    xs = x.reshape(shp[:-1] + (2, 2, ROPE_DIM // 4))
    x1, x2 = xs[..., 0, :], xs[..., 1, :]
    out = jnp.stack([x1 * cos - x2 * sin, x1 * sin + x2 * cos], axis=-2)
    return out.reshape(shp)


def _attention(q, k, v):
    b, lq, h, d = q.shape
    nb = lq // ATTN_BLOCK
    qb = q.reshape(b, nb, ATTN_BLOCK, h, d).transpose(1, 0, 2, 3, 4)
    scale = 1.0 / math.sqrt(d)

    def block(qi):
        s = jnp.einsum('bqhd,bkhd->bhqk', qi, k).astype(F32) * scale
        w = jax.nn.softmax(s, axis=-1).astype(v.dtype)
        return jnp.einsum('bhqk,bkhd->bqhd', w, v)

    out = lax.map(block, qb)
    return out.transpose(1, 0, 2, 3, 4).reshape(b, lq, h, v.shape[-1])


def _mla_kv(ckv, kr, w_ukv):
    b, L, _ = ckv.shape
    kv = _matmul(ckv.reshape(b * L, KV_RANK), w_ukv, name="mla_ukv").reshape(
        b, L, MLA_HEADS, NOPE_DIM + V_DIM)
    k_nope, v = kv[..., :NOPE_DIM], kv[..., NOPE_DIM:]
    k = jnp.concatenate([k_nope, jnp.broadcast_to(kr[:, :, None, :], (b, L, MLA_HEADS, ROPE_DIM))], axis=-1)
    return k, v


def _dwconv(x, w, bias):
    ch = x.shape[-1]
    y = lax.conv_general_dilated(x, w[:, None, :].astype(x.dtype), window_strides=(1,),
                                 padding=[(SSM_CONV // 2, SSM_CONV // 2)],
                                 dimension_numbers=('NWC', 'WIO', 'NWC'), feature_group_count=ch)
    return y + bias


def _ssd(x, dt, a, bm, cm, h0):
    b, L, H, P = x.shape
    nc = L // SSM_CHUNK
    rep = H // SSM_GROUPS
    xc = x.astype(F32).reshape(b, nc, SSM_CHUNK, H, P)
    bc = jnp.repeat(bm.astype(F32), rep, axis=2).reshape(b, nc, SSM_CHUNK, H, SSM_STATE)
    cc = jnp.repeat(cm.astype(F32), rep, axis=2).reshape(b, nc, SSM_CHUNK, H, SSM_STATE)
    dtc = dt.reshape(b, nc, SSM_CHUNK, H)
    acum = jnp.cumsum(dtc * a, axis=2)
    seg = acum[:, :, :, None, :] - acum[:, :, None, :, :]
    lower = jnp.tril(jnp.ones((SSM_CHUNK, SSM_CHUNK), dtype=bool))[None, None, :, :, None]
    lmat = jnp.exp(jnp.where(lower, seg, -jnp.inf))
    cb = jnp.einsum('bcihn,bcjhn->bcijh', cc, bc)
    y_diag = jnp.einsum('bcijh,bcjhp->bcihp', cb * lmat * dtc[:, :, None, :, :], xc)
    decay_end = jnp.exp(acum[:, :, -1:, :] - acum) * dtc
    states = jnp.einsum('bcjhn,bcjh,bcjhp->bchpn', bc, decay_end, xc)
    chunk_decay = jnp.exp(acum[:, :, -1, :])

    def step(h, inp):
        dec, st = inp
        return dec[:, :, None, None] * h + st, h

    h_fin, h_in = lax.scan(step, h0.astype(F32), (jnp.moveaxis(chunk_decay, 1, 0), jnp.moveaxis(states, 1, 0)))
    h_in = jnp.moveaxis(h_in, 0, 1)
    y_off = jnp.einsum('bcihn,bchpn,bcih->bcihp', cc, h_in, jnp.exp(acum))
    return (y_diag + y_off).reshape(b, L, H, P), h_fin


def _token_mixer(h, p, rope, ctx_ckv, ctx_kr, h0):
    b, L, _ = h.shape
    zall = _matmul(h.reshape(b * L, D_MODEL), p['w_in'], tm=1024, tn=512, name="in_proj").reshape(b, L, -1)
    cq, ckv_raw, kr, z, xbc, dt_raw, gm, gate_logits = _split_in(zall)

    ckv = _rmsnorm(ckv_raw, p['g_kvn'])
    q = _matmul(_rmsnorm(cq, p['g_qn']).reshape(b * L, Q_RANK), p['w_uq'], name="mla_uq").reshape(
        b, L, MLA_HEADS, NOPE_DIM + ROPE_DIM)
    q_nope, q_rope = q[..., :NOPE_DIM], q[..., NOPE_DIM:]
    kr_pos = kr
    if rope is not None:
        cos, sin = rope
        q_rope = _axial_rope(q_rope, cos[:, None], sin[:, None])
        kr_pos = _axial_rope(kr, cos, sin)
    k, v = _mla_kv(ckv, kr_pos, p['w_ukv'])
    if ctx_ckv is not None:
        k_c, v_c = _mla_kv(ctx_ckv, ctx_kr, p['w_ukv'])
        k = jnp.concatenate([k_c, k], axis=1)
        v = jnp.concatenate([v_c, v], axis=1)
    attn = _attention(jnp.concatenate([q_nope, q_rope], axis=-1), k, v).reshape(b, L, MLA_WIDTH)

    xbc = jax.nn.silu(_dwconv(xbc, p['conv_w'], p['conv_b']))
    gn = SSM_GROUPS * SSM_STATE
    xs = xbc[..., :SSM_INNER].reshape(b, L, SSM_HEADS, SSM_HEADDIM)
    bm = xbc[..., SSM_INNER:SSM_INNER + gn].reshape(b, L, SSM_GROUPS, SSM_STATE)
    cm = xbc[..., SSM_INNER + gn:].reshape(b, L, SSM_GROUPS, SSM_STATE)
    dt = jax.nn.softplus(dt_raw.astype(F32).reshape(b, L, 2, SSM_HEADS) + p['dt_bias'].astype(F32))
    a = -jnp.exp(p['a_log'].astype(F32))
    if h0 is None:
        h0 = jnp.zeros((b, 2, SSM_HEADS, SSM_HEADDIM, SSM_STATE), F32)
    y_f, hf_f = _ssd(xs, dt[:, :, 0], a[0], bm, cm, h0[:, 0])
    y_b, hf_b = _ssd(jnp.flip(xs, 1), jnp.flip(dt[:, :, 1], 1), a[1], jnp.flip(bm, 1), jnp.flip(cm, 1), h0[:, 1])
    y = y_f + jnp.flip(y_b, 1) + p['d_skip'].astype(F32)[:, None] * xs.astype(F32)
    y = y.astype(h.dtype).reshape(b, L, SSM_INNER)
    ssm = _rmsnorm(y * jax.nn.silu(z), p['g_ssm'])

    gm = jax.nn.gelu(gm)
    u, vg = gm[..., :GM_WIDTH], gm[..., GM_WIDTH:]
    vg = _rmsnorm(vg, p['g_gv']).reshape(b, L // GM_CHUNK, GM_CHUNK, GM_GROUPS, GM_WIDTH // GM_GROUPS)
    sv = jnp.einsum('gij,bcjgd->bcigd', p['w_sp'], vg) + p['b_sp'].T[None, None, :, :, None]
    gmo = u * sv.reshape(b, L, GM_WIDTH)

    ga, gs, gc = jnp.split(jax.nn.sigmoid(gate_logits), N_BRANCH, axis=-1)
    n = b * L
    merged = (ga.reshape(n, -1) * _matmul(attn.reshape(n, -1), p['w_br_attn'], name="br_attn")
              + gs.reshape(n, -1) * _matmul(ssm.reshape(n, -1), p['w_br_ssm'], name="br_ssm")
              + gc.reshape(n, -1) * _matmul(gmo.reshape(n, -1), p['w_br_gmlp'], name="br_gmlp"))
    mix = _matmul(merged, p['w_out'], name="out_proj").reshape(b, L, D_MODEL)
    return mix, ckv, kr, jnp.stack([hf_f, hf_b], axis=1).astype(h.dtype)


def _ec_moe(h, w_router, w_gate, w_up, w_down):
    b, n, d = h.shape
    cap = EC_CAPACITY * n // N_EXPERTS
    aff = jax.nn.softmax((h @ w_router).astype(F32), axis=-1)
    g, idx = lax.top_k(jnp.swapaxes(aff, 1, 2), cap)
    xs = jax.vmap(lambda hb, ib: hb[ib])(h, idx)
    xe = jnp.swapaxes(xs, 0, 1).reshape(N_EXPERTS, b * cap, d).astype(BF16)
    oe = _expert_ffn(xe, w_gate, w_up, w_down)
    out = jnp.swapaxes(oe.reshape(N_EXPERTS, b, cap, d), 0, 1) * g[..., None].astype(h.dtype)
    return jax.vmap(lambda ob, ib: jnp.zeros((n, d), h.dtype).at[ib.reshape(-1)].add(ob.reshape(-1, d)))(out, idx)


def _layer(x, mod, p, rope, ctx_ckv, ctx_kr, h0):
    sh1, sc1, gt1, sh2, sc2, gt2 = jnp.split(mod, 6, axis=-1)
    h = _rmsnorm(x, p['g_norm1']) * (1 + sc1) + sh1
    mix, ckv, kr, hfin = _token_mixer(h, p, rope, ctx_ckv, ctx_kr, h0)
    x = x + gt1 * mix
    h = _rmsnorm(x, p['g_norm2']) * (1 + sc2) + sh2
    x = x + gt2 * _ec_moe(h, p['w_router'], p['w_gate'], p['w_up'], p['w_down'])
    return x, ckv, kr, hfin


def kernel(x_prompt, x_sample, c, cache_ckv, cache_krope, state_ssm, c_ctx, w_mod, b_mod,
           g_norm1, g_norm2, w_in, g_qn, w_uq, g_kvn, w_ukv, conv_w, conv_b, dt_bias, a_log,
           d_skip, g_ssm, g_gv, w_sp, b_sp, w_br_attn, w_br_ssm, w_br_gmlp, w_out, w_router,
           w_gate, w_up, w_down, g_final):
    n_lat = x_sample.shape[1]
    rows = n_lat // GRID_W
    row = jnp.repeat(jnp.arange(rows), GRID_W).astype(F32)
    col = jnp.tile(jnp.arange(GRID_W), rows).astype(F32)
    nf = ROPE_DIM // 4
    freqs = jnp.power(ROPE_THETA, -jnp.arange(nf, dtype=F32) / nf)
    ang = jnp.stack([row[:, None] * freqs, col[:, None] * freqs], axis=1)
    rope = (jnp.cos(ang).astype(x_sample.dtype), jnp.sin(ang).astype(x_sample.dtype))

    xp, xs = x_prompt, x_sample
    ckvs, krs, sts = [], [], []
    for l in range(DEPTH):
        p = {
            'g_norm1': g_norm1[l], 'g_norm2': g_norm2[l], 'w_in': w_in[l], 'g_qn': g_qn[l],
            'w_uq': w_uq[l], 'g_kvn': g_kvn[l], 'w_ukv': w_ukv[l], 'conv_w': conv_w[l],
            'conv_b': conv_b[l], 'dt_bias': dt_bias[l], 'a_log': a_log[l], 'd_skip': d_skip[l],
            'g_ssm': g_ssm[l], 'g_gv': g_gv[l], 'w_sp': w_sp[l], 'b_sp': b_sp[l],
            'w_br_attn': w_br_attn[l], 'w_br_ssm': w_br_ssm[l], 'w_br_gmlp': w_br_gmlp[l],
            'w_out': w_out[l], 'w_router': w_router[l], 'w_gate': w_gate[l], 'w_up': w_up[l],
            'w_down': w_down[l],
        }
        mod_ctx = jax.nn.silu(c_ctx) @ w_mod[l] + b_mod[l]
        mod_lat = (jax.nn.silu(c) @ w_mod[l] + b_mod[l])[:, None, :]
        xp, ckv_l, kr_l, st_l = _layer(xp, mod_ctx, p, None, None, None, None)
        ckvs.append(ckv_l)
        krs.append(kr_l)
        sts.append(st_l)
        xs, _, _, _ = _layer(xs, mod_lat, p, rope, cache_ckv[:, l], cache_krope[:, l], state_ssm[:, l])

    y_prompt = _rmsnorm(xp, g_final)
    y_sample = _rmsnorm(xs, g_final)
    return (y_prompt, y_sample, jnp.stack(ckvs, axis=1), jnp.stack(krs, axis=1), jnp.stack(sts, axis=1))
```

```python
import functools
import math

import jax
import jax.numpy as jnp
from jax import lax
from jax.experimental import pallas as pl
from jax.experimental.pallas import tpu as pltpu

D_MODEL = 2048
BATCH = 16
SEQ = 256
DEPTH = 2
DEC_BATCH = 2
DEC_SEQ = 2048
PAST_LEN = 512
GRID_W = 64
ROPE_THETA = 10000.0
EPS = 1e-6
MLA_HEADS = 8
Q_RANK = 512
KV_RANK = 256
NOPE_DIM = 128
ROPE_DIM = 64
V_DIM = 128
MLA_WIDTH = MLA_HEADS * V_DIM
SSM_HEADS = 16
SSM_HEADDIM = 64
SSM_INNER = SSM_HEADS * SSM_HEADDIM
SSM_GROUPS = 2
SSM_STATE = 128
SSM_CONV = 5
SSM_CHUNK = 128
SSM_XBC = SSM_INNER + 2 * SSM_GROUPS * SSM_STATE
GM_WIDTH = 1024
GM_GROUPS = 4
GM_CHUNK = 128
N_BRANCH = 3
N_EXPERTS = 16
EXPERT_FF = 1024
EC_CAPACITY = 2

N_CTX = BATCH * SEQ
N_LAT = DEC_BATCH * DEC_SEQ
N_TOK = N_CTX + N_LAT
CAP_CTX = EC_CAPACITY * SEQ // N_EXPERTS
CAP_LAT = EC_CAPACITY * DEC_SEQ // N_EXPERTS
ROWS_CTX = BATCH * CAP_CTX
ROWS_LAT = DEC_BATCH * CAP_LAT

LANES = 128
QK_HEAD = 2 * LANES

ZB_Z = Q_RANK
ZB_XBC = ZB_Z + SSM_INNER
ZB_GM = ZB_XBC + SSM_XBC
ZB_GATE = ZB_GM + 2 * GM_WIDTH
ZB_WIDTH = ZB_GATE + N_BRANCH * D_MODEL
ZK_WIDTH = LANES

V7X_VMEM_LIMIT_BYTES = 56 * 1024 * 1024

BF16 = jnp.bfloat16
F32 = jnp.float32


def _cparams(*sem):
    return pltpu.CompilerParams(dimension_semantics=sem, vmem_limit_bytes=V7X_VMEM_LIMIT_BYTES)


def _rms(x, g):
    return x * lax.rsqrt(jnp.mean(x * x, axis=-1, keepdims=True) + EPS) * g


def _swap16(x):
    lane = lax.broadcasted_iota(jnp.int32, x.shape, 1)
    return jnp.where((lane % 32) < 16, pltpu.roll(x, LANES - 16, 1), pltpu.roll(x, 16, 1))


def _mm_body(x_ref, w_ref, o_ref):
    o_ref[...] = jnp.dot(x_ref[...].astype(BF16), w_ref[...].astype(BF16),
                         preferred_element_type=F32).astype(o_ref.dtype)


def _matmul(x, w, out_dtype=F32, tm=512, tn=512, name="matmul"):
    m, k = x.shape
    _, n = w.shape
    tn = min(tn, n)
    return pl.pallas_call(
        _mm_body,
        grid=(m // tm, n // tn),
        in_specs=[pl.BlockSpec((tm, k), lambda i, j: (i, 0)),
                  pl.BlockSpec((k, tn), lambda i, j: (0, j))],
        out_specs=pl.BlockSpec((tm, tn), lambda i, j: (i, j)),
        out_shape=jax.ShapeDtypeStruct((m, n), out_dtype),
        compiler_params=_cparams("parallel", "arbitrary"),
        name=name,
    )(x, w)


IN_TM = 1024
IN_TN = 512
NORM_ROWS = 256


def _norm_mod_to(h_scr, x_ref, g_ref, sc_ref, sh_ref):
    g = g_ref[...]
    mul = 1.0 + sc_ref[0]
    add = sh_ref[0]
    for r in range(0, x_ref.shape[0], NORM_ROWS):
        x = x_ref[r:r + NORM_ROWS, :]
        h_scr[r:r + NORM_ROWS, :] = (_rms(x, g) * mul + add).astype(BF16)


def _inproj_body(x_ref, g_ref, sc_ref, sh_ref, w_ref, o_ref, h_scr):
    @pl.when(pl.program_id(1) == 0)
    def _():
        _norm_mod_to(h_scr, x_ref, g_ref, sc_ref, sh_ref)

    o_ref[...] = jnp.dot(h_scr[...], w_ref[...], preferred_element_type=F32).astype(o_ref.dtype)


def _inproj_small_body(x_ref, g_ref, sc_ref, sh_ref, w_ref, ckv_ref, zk_ref, h_scr):
    _norm_mod_to(h_scr, x_ref, g_ref, sc_ref, sh_ref)
    acc = jnp.dot(h_scr[...], w_ref[...], preferred_element_type=F32)
    ckv_ref[...] = acc[:, :KV_RANK]
    zk_ref[...] = acc[:, KV_RANK:]


def _inproj(x, g, sc_t, sh_t, wb, ws):
    nt = N_TOK // IN_TM
    common = [pl.BlockSpec((IN_TM, D_MODEL), lambda i, j: (i, 0)),
              pl.BlockSpec((1, D_MODEL), lambda i, j: (0, 0)),
              pl.BlockSpec((1, 1, D_MODEL), lambda i, j: (i, 0, 0)),
              pl.BlockSpec((1, 1, D_MODEL), lambda i, j: (i, 0, 0))]
    zb = pl.pallas_call(
        _inproj_body,
        grid=(nt, ZB_WIDTH // IN_TN),
        in_specs=common + [pl.BlockSpec((D_MODEL, IN_TN), lambda i, j: (0, j))],
        out_specs=pl.BlockSpec((IN_TM, IN_TN), lambda i, j: (i, j)),
        out_shape=jax.ShapeDtypeStruct((N_TOK, ZB_WIDTH), BF16),
        scratch_shapes=[pltpu.VMEM((IN_TM, D_MODEL), BF16)],
        compiler_params=_cparams("parallel", "arbitrary"),
        name="in_proj",
    )(x, g, sc_t, sh_t, wb)
    nsmall = KV_RANK + ZK_WIDTH
    ckv_raw, zk = pl.pallas_call(
        _inproj_small_body,
        grid=(nt, 1),
        in_specs=common + [pl.BlockSpec((D_MODEL, nsmall), lambda i, j: (0, 0))],
        out_specs=[pl.BlockSpec((IN_TM, KV_RANK), lambda i, j: (i, 0)),
                   pl.BlockSpec((IN_TM, ZK_WIDTH), lambda i, j: (i, 0))],
        out_shape=[jax.ShapeDtypeStruct((N_TOK, KV_RANK), F32),
                   jax.ShapeDtypeStruct((N_TOK, ZK_WIDTH), F32)],
        scratch_shapes=[pltpu.VMEM((IN_TM, D_MODEL), BF16)],
        compiler_params=_cparams("parallel", "arbitrary"),
        name="in_proj_small",
    )(x, g, sc_t, sh_t, ws)
    return zb, ckv_raw, zk


QKV_TM = 512
ATTN_SCALE = 1.0 / math.sqrt(NOPE_DIM + ROPE_DIM)


def _qproj_body(cq_ref, g_ref, w_ref, c_ref, s_ref, o_ref):
    qn = _rms(cq_ref[...].astype(F32), g_ref[...]).astype(BF16)
    q = jnp.dot(qn, w_ref[...], preferred_element_type=F32)
    c = c_ref[...]
    s = s_ref[...]
    for h in range(MLA_HEADS):
        lo = h * QK_HEAD
        r = q[:, lo + LANES:lo + QK_HEAD]
        o_ref[:, lo:lo + LANES] = (q[:, lo:lo + LANES] * ATTN_SCALE).astype(BF16)
        o_ref[:, lo + LANES:lo + QK_HEAD] = ((r * c + _swap16(r) * s) * ATTN_SCALE).astype(BF16)


def _qproj(zb, g_qn, w_uq_p, rope_c, rope_s):
    return pl.pallas_call(
        _qproj_body,
        grid=(N_TOK // QKV_TM,),
        in_specs=[pl.BlockSpec((QKV_TM, Q_RANK), lambda i: (i, 0)),
                  pl.BlockSpec((1, Q_RANK), lambda i: (0, 0)),
                  pl.BlockSpec((Q_RANK, MLA_HEADS * QK_HEAD), lambda i: (0, 0)),
                  pl.BlockSpec((QKV_TM, LANES), lambda i: (i, 0)),
                  pl.BlockSpec((QKV_TM, LANES), lambda i: (i, 0))],
        out_specs=pl.BlockSpec((QKV_TM, MLA_HEADS * QK_HEAD), lambda i: (i, 0)),
        out_shape=jax.ShapeDtypeStruct((N_TOK, MLA_HEADS * QK_HEAD), BF16),
        compiler_params=_cparams("parallel"),
        name="mla_q",
    )(zb, g_qn, w_uq_p, rope_c, rope_s)


def _kvproj_body(ckv_ref, zk_ref, g_ref, w_ref, c_ref, s_ref, ckv_o, kv_o, kr_o):
    ckv = _rms(ckv_ref[...], g_ref[...])
    ckv_o[...] = ckv
    kv_o[...] = jnp.dot(ckv.astype(BF16), w_ref[...], preferred_element_type=F32).astype(BF16)
    zk = zk_ref[...]
    is_rope = lax.broadcasted_iota(jnp.int32, zk.shape, 1) < ROPE_DIM
    kr = jnp.where(is_rope, zk, 0.0)
    rot = kr * c_ref[...] + _swap16(kr) * s_ref[...]
    kr_o[...] = jnp.where(is_rope, rot, 0.0).astype(BF16)


def _kvproj(ckv_raw, zk, g_kvn, w_ukv_p, rope_c, rope_s):
    return pl.pallas_call(
        _kvproj_body,
        grid=(N_TOK // QKV_TM,),
        in_specs=[pl.BlockSpec((QKV_TM, KV_RANK), lambda i: (i, 0)),
                  pl.BlockSpec((QKV_TM, ZK_WIDTH), lambda i: (i, 0)),
                  pl.BlockSpec((1, KV_RANK), lambda i: (0, 0)),
                  pl.BlockSpec((KV_RANK, 2 * MLA_WIDTH), lambda i: (0, 0)),
                  pl.BlockSpec((QKV_TM, LANES), lambda i: (i, 0)),
                  pl.BlockSpec((QKV_TM, LANES), lambda i: (i, 0))],
        out_specs=[pl.BlockSpec((QKV_TM, KV_RANK), lambda i: (i, 0)),
                   pl.BlockSpec((QKV_TM, 2 * MLA_WIDTH), lambda i: (i, 0)),
                   pl.BlockSpec((QKV_TM, LANES), lambda i: (i, 0))],
        out_shape=[jax.ShapeDtypeStruct((N_TOK, KV_RANK), F32),
                   jax.ShapeDtypeStruct((N_TOK, 2 * MLA_WIDTH), BF16),
                   jax.ShapeDtypeStruct((N_TOK, LANES), BF16)],
        compiler_params=_cparams("parallel"),
        name="mla_kv",
    )(ckv_raw, zk, g_kvn, w_ukv_p, rope_c, rope_s)


def _attn_body(n_parts, q_ref, *refs):
    o_ref = refs[3 * n_parts]
    for h in range(MLA_HEADS):
        qh = q_ref[:, h * QK_HEAD:(h + 1) * QK_HEAD]
        scores = []
        for p in range(n_parts):
            kn_ref, kr_ref = refs[3 * p], refs[3 * p + 1]
            kh = jnp.concatenate([kn_ref[:, h * LANES:(h + 1) * LANES], kr_ref[...]], axis=1)
            scores.append(lax.dot_general(qh, kh, (((1,), (1,)), ((), ())),
                                          preferred_element_type=F32))
        m = scores[0].max(axis=1, keepdims=True)
        for s in scores[1:]:
            m = jnp.maximum(m, s.max(axis=1, keepdims=True))
        den = 0.0
        acc = 0.0
        for p in range(n_parts):
            e = jnp.exp(scores[p] - m)
            den = den + e.sum(axis=1, keepdims=True)
            v_ref = refs[3 * p + 2]
            acc = acc + jnp.dot(e.astype(BF16), v_ref[:, h * LANES:(h + 1) * LANES],
                                preferred_element_type=F32)
        o_ref[:, h * LANES:(h + 1) * LANES] = (acc / den).astype(BF16)


def _attention(q, kv, kr, n_seq, seq_len, tq, row0, cache=None):
    nq = seq_len // tq
    q0 = row0 // tq
    s0 = row0 // seq_len
    in_specs = [pl.BlockSpec((tq, MLA_HEADS * QK_HEAD), lambda b, i: (q0 + b * nq + i, 0))]
    args = [q]
    if cache is not None:
        kv_c, kr_c, len_c = cache
        in_specs += [pl.BlockSpec((len_c, MLA_WIDTH), lambda b, i: (b, 0)),
                     pl.BlockSpec((len_c, LANES), lambda b, i: (b, 0)),
                     pl.BlockSpec((len_c, MLA_WIDTH), lambda b, i: (b, 1))]
        args += [kv_c, kr_c, kv_c]
    in_specs += [pl.BlockSpec((seq_len, MLA_WIDTH), lambda b, i: (s0 + b, 0)),
                 pl.BlockSpec((seq_len, LANES), lambda b, i: (s0 + b, 0)),
                 pl.BlockSpec((seq_len, MLA_WIDTH), lambda b, i: (s0 + b, 1))]
    args += [kv, kr, kv]
    n_parts = 1 if cache is None else 2
    return pl.pallas_call(
        functools.partial(_attn_body, n_parts),
        grid=(n_seq, nq),
        in_specs=in_specs,
        out_specs=pl.BlockSpec((tq, MLA_WIDTH), lambda b, i: (b * nq + i, 0)),
        out_shape=jax.ShapeDtypeStruct((n_seq * seq_len, MLA_WIDTH), BF16),
        compiler_params=_cparams("parallel", "arbitrary"),
        name="mla_attn_cache" if cache is not None else "mla_attn",
    )(*args)


MG_TM = 1024
MG_TN = 512
OUT_TM = 256


def _merge_body(a_ref, s_ref, c_ref, wa_ref, ws_ref, wc_ref, ga_ref, gs_ref, gc_ref, o_ref):
    def branch(x_ref, w_ref, gate_ref):
        y = jnp.dot(x_ref[...], w_ref[...], preferred_element_type=F32)
        return jax.nn.sigmoid(gate_ref[...].astype(F32)) * y

    o_ref[...] = (branch(a_ref, wa_ref, ga_ref) + branch(s_ref, ws_ref, gs_ref)
                  + branch(c_ref, wc_ref, gc_ref)).astype(BF16)


def _merge(attn, ssm, gmo, wa, ws, wc, zb):
    g0 = ZB_GATE // MG_TN
    gstep = D_MODEL // MG_TN
    xspec = pl.BlockSpec((MG_TM, MLA_WIDTH), lambda i, j: (i, 0))
    wspec = pl.BlockSpec((MLA_WIDTH, MG_TN), lambda i, j: (0, j))
    return pl.pallas_call(
        _merge_body,
        grid=(N_TOK // MG_TM, gstep),
        in_specs=[xspec, xspec, xspec, wspec, wspec, wspec,
                  pl.BlockSpec((MG_TM, MG_TN), lambda i, j: (i, g0 + j)),
                  pl.BlockSpec((MG_TM, MG_TN), lambda i, j: (i, g0 + gstep + j)),
                  pl.BlockSpec((MG_TM, MG_TN), lambda i, j: (i, g0 + 2 * gstep + j))],
        out_specs=pl.BlockSpec((MG_TM, MG_TN), lambda i, j: (i, j)),
        out_shape=jax.ShapeDtypeStruct((N_TOK, D_MODEL), BF16),
        compiler_params=_cparams("parallel", "arbitrary"),
        name="branch_merge",
    )(attn, ssm, gmo, wa, ws, wc, zb, zb, zb)


def _split_bf16(x):
    hi = x.astype(BF16)
    return hi, (x - hi.astype(F32)).astype(BF16)


def _outproj_body(m_ref, w_ref, x_ref, gt_ref, g_ref, sc_ref, sh_ref, wr_ref, x1_ref, h2_ref, lg_ref):
    mix = jnp.dot(m_ref[...], w_ref[...], preferred_element_type=F32)
    x1 = x_ref[...] + gt_ref[0] * mix
    x1_ref[...] = x1
    h2 = _rms(x1, g_ref[...]) * (1.0 + sc_ref[0]) + sh_ref[0]
    h2_ref[...] = h2.astype(BF16)
    h_hi, h_lo = _split_bf16(h2)
    w_hi, w_lo = _split_bf16(wr_ref[...])
    nt = (((1,), (1,)), ((), ()))
    lg_ref[...] = (lax.dot_general(w_hi, h_hi, nt, preferred_element_type=F32)
                   + lax.dot_general(w_hi, h_lo, nt, preferred_element_type=F32)
                   + lax.dot_general(w_lo, h_hi, nt, preferred_element_type=F32))


def _outproj(merged, w_out, x, gt_t, g2, sc_t, sh_t, w_router_t):
    mspec = pl.BlockSpec((1, 1, D_MODEL), lambda i: (i, 0, 0))
    return pl.pallas_call(
        _outproj_body,
        grid=(N_TOK // OUT_TM,),
        in_specs=[pl.BlockSpec((OUT_TM, D_MODEL), lambda i: (i, 0)),
                  pl.BlockSpec((D_MODEL, D_MODEL), lambda i: (0, 0)),
                  pl.BlockSpec((OUT_TM, D_MODEL), lambda i: (i, 0)),
                  mspec,
                  pl.BlockSpec((1, D_MODEL), lambda i: (0, 0)),
                  mspec, mspec,
                  pl.BlockSpec((N_EXPERTS, D_MODEL), lambda i: (0, 0))],
        out_specs=[pl.BlockSpec((OUT_TM, D_MODEL), lambda i: (i, 0)),
                   pl.BlockSpec((OUT_TM, D_MODEL), lambda i: (i, 0)),
                   pl.BlockSpec((N_EXPERTS, OUT_TM), lambda i: (0, i))],
        out_shape=[jax.ShapeDtypeStruct((N_TOK, D_MODEL), F32),
                   jax.ShapeDtypeStruct((N_TOK, D_MODEL), BF16),
                   jax.ShapeDtypeStruct((N_EXPERTS, N_TOK), F32)],
        compiler_params=_cparams("parallel"),
        name="out_proj",
    )(merged, w_out, x, gt_t, g2, sc_t, sh_t, w_router_t)


PREFIX_BLK = 256


def _prefix_count(mask):
    n = mask.shape[1]
    upper = (lax.broadcasted_iota(jnp.int32, (PREFIX_BLK, PREFIX_BLK), 0)
             < lax.broadcasted_iota(jnp.int32, (PREFIX_BLK, PREFIX_BLK), 1)).astype(BF16)
    run = jnp.zeros((mask.shape[0], 1), F32)
    outs = []
    for k in range(0, n, PREFIX_BLK):
        blk = mask[:, k:k + PREFIX_BLK]
        outs.append(jnp.dot(blk.astype(BF16), upper, preferred_element_type=F32) + run)
        run = run + jnp.sum(blk, axis=1, keepdims=True)
    return outs[0] if len(outs) == 1 else jnp.concatenate(outs, axis=1)


def _route_body(cap, lg_ref, h_ref, slot_ref, xe_ref, ge_ref, slot_scr, aff_scr):
    n = lg_ref.shape[1]
    x = pl.program_id(1)

    @pl.when(x == 0)
    def _():
        lg = lg_ref[...]
        e = jnp.exp(lg - lg.max(axis=0, keepdims=True))
        aff = e / e.sum(axis=0, keepdims=True)
        bits = pltpu.bitcast(aff, jnp.int32)

        def refine(b, lo):
            cand = lo | lax.shift_left(jnp.int32(1), 30 - b)
            cnt = jnp.sum((bits >= cand).astype(F32), axis=1, keepdims=True)
            return jnp.where(cnt >= cap, cand, lo)

        thr = lax.fori_loop(0, 31, refine, jnp.zeros((N_EXPERTS, 1), jnp.int32))
        above = (bits > thr).astype(F32)
        tied = (bits == thr).astype(F32)
        need = cap - jnp.sum(above, axis=1, keepdims=True)
        sel = above + tied * (_prefix_count(tied) < need).astype(F32)
        slot = jnp.where(sel > 0.0, _prefix_count(sel), -1.0)
        slot_scr[...] = slot
        aff_scr[...] = aff
        slot_ref[...] = slot.astype(jnp.int32)

    row = lax.broadcasted_iota(jnp.int32, (cap, n), 0).astype(F32)
    onehot = slot_scr[pl.ds(x, 1), :] == row
    xe_ref[0] = jnp.dot(onehot.astype(BF16), h_ref[...], preferred_element_type=F32).astype(BF16)
    gate = jnp.sum(jnp.where(onehot, aff_scr[pl.ds(x, 1), :], 0.0), axis=1, keepdims=True)
    ge_ref[0] = jnp.broadcast_to(gate, (cap, LANES))


def _route(logits_t, h2, n_seq, seq_len, cap, row0):
    s0 = row0 // seq_len
    return pl.pallas_call(
        functools.partial(_route_body, cap),
        grid=(n_seq, N_EXPERTS),
        in_specs=[pl.BlockSpec((N_EXPERTS, seq_len), lambda b, x: (0, s0 + b)),
                  pl.BlockSpec((seq_len, D_MODEL), lambda b, x: (s0 + b, 0))],
        out_specs=[pl.BlockSpec((N_EXPERTS, seq_len), lambda b, x: (0, b)),
                   pl.BlockSpec((1, cap, D_MODEL), lambda b, x: (x, b, 0)),
                   pl.BlockSpec((1, cap, LANES), lambda b, x: (x, b, 0))],
        out_shape=[jax.ShapeDtypeStruct((N_EXPERTS, n_seq * seq_len), jnp.int32),
                   jax.ShapeDtypeStruct((N_EXPERTS, n_seq * cap, D_MODEL), BF16),
                   jax.ShapeDtypeStruct((N_EXPERTS, n_seq * cap, LANES), F32)],
        scratch_shapes=[pltpu.VMEM((N_EXPERTS, seq_len), F32),
                        pltpu.VMEM((N_EXPERTS, seq_len), F32)],
        compiler_params=_cparams("parallel", "arbitrary"),
        name="moe_route_gather",
    )(logits_t, h2)


FFN_TF = 256


def _ffn_body(xc_ref, xl_ref, gc_ref, gl_ref, wg_ref, wu_ref, wd_ref, yc_ref, yl_ref, acc):
    f = pl.program_id(1)
    x = jnp.concatenate([xc_ref[0], xl_ref[0]], axis=0)
    gate = jnp.dot(x, wg_ref[0].astype(BF16), preferred_element_type=F32)
    up = jnp.dot(x, wu_ref[0].astype(BF16), preferred_element_type=F32)
    hid = (gate * jax.nn.sigmoid(gate) * up).astype(BF16)
    part = jnp.dot(hid, wd_ref[0].astype(BF16), preferred_element_type=F32)

    @pl.when(f == 0)
    def _():
        acc[...] = part

    @pl.when(f > 0)
    def _():
        acc[...] += part

    @pl.when(f == pl.num_programs(1) - 1)
    def _():
        yc_ref[0] = (acc[:ROWS_CTX, :] * gc_ref[0][:, :1]).astype(BF16)
        yl_ref[0] = (acc[ROWS_CTX:, :] * gl_ref[0][:, :1]).astype(BF16)


def _expert_ffn(xe_c, xe_l, ge_c, ge_l, w_gate, w_up, w_down):
    def xspec(rows, width):
        return pl.BlockSpec((1, rows, width), lambda i, f: (i, 0, 0))

    return pl.pallas_call(
        _ffn_body,
        grid=(N_EXPERTS, EXPERT_FF // FFN_TF),
        in_specs=[xspec(ROWS_CTX, D_MODEL), xspec(ROWS_LAT, D_MODEL),
                  xspec(ROWS_CTX, LANES), xspec(ROWS_LAT, LANES),
                  pl.BlockSpec((1, D_MODEL, FFN_TF), lambda i, f: (i, 0, f)),
                  pl.BlockSpec((1, D_MODEL, FFN_TF), lambda i, f: (i, 0, f)),
                  pl.BlockSpec((1, FFN_TF, D_MODEL), lambda i, f: (i, f, 0))],
        out_specs=[xspec(ROWS_CTX, D_MODEL), xspec(ROWS_LAT, D_MODEL)],
        out_shape=[jax.ShapeDtypeStruct((N_EXPERTS, ROWS_CTX, D_MODEL), BF16),
                   jax.ShapeDtypeStruct((N_EXPERTS, ROWS_LAT, D_MODEL), BF16)],
        scratch_shapes=[pltpu.VMEM((ROWS_CTX + ROWS_LAT, D_MODEL), F32)],
        compiler_params=_cparams("parallel", "arbitrary"),
        name="expert_ffn",
    )(xe_c, xe_l, ge_c, ge_l, w_gate, w_up, w_down)


CMB_TN = 512


def _combine_body(cap, slot_ref, y_ref, x_ref, gt_ref, o_ref, acc):
    n = slot_ref.shape[0]
    slot = slot_ref[...]
    col = lax.broadcasted_iota(jnp.int32, (n, cap), 1)
    for x in range(N_EXPERTS):
        onehot = (slot[:, x:x + 1] == col).astype(BF16)
        part = jnp.dot(onehot, y_ref[x], preferred_element_type=F32)
        if x == 0:
            acc[...] = part
        else:
            acc[...] += part
    o_ref[...] = x_ref[...] + gt_ref[0] * acc[...]


def _combine(slot_t, y, x1, gt_t, n_seq, seq_len, cap, row0):
    s0 = row0 // seq_len
    return pl.pallas_call(
        functools.partial(_combine_body, cap),
        grid=(n_seq, D_MODEL // CMB_TN),
        in_specs=[pl.BlockSpec((seq_len, N_EXPERTS), lambda b, j: (b, 0)),
                  pl.BlockSpec((N_EXPERTS, cap, CMB_TN), lambda b, j: (0, b, j)),
                  pl.BlockSpec((seq_len, CMB_TN), lambda b, j: (s0 + b, j)),
                  pl.BlockSpec((1, 1, CMB_TN), lambda b, j: (b, 0, j))],
        out_specs=pl.BlockSpec((seq_len, CMB_TN), lambda b, j: (b, j)),
        out_shape=jax.ShapeDtypeStruct((n_seq * seq_len, D_MODEL), F32),
        scratch_shapes=[pltpu.VMEM((seq_len, CMB_TN), F32)],
        compiler_params=_cparams("parallel", "arbitrary"),
        name="moe_combine",
    )(slot_t, y, x1, gt_t)


def _rmsnorm(x, g):
    xf = x.astype(F32)
    y = xf * lax.rsqrt(jnp.mean(xf * xf, axis=-1, keepdims=True) + EPS)
    return y.astype(x.dtype) * g


def _dwconv(x, w, bias):
    ch = x.shape[-1]
    y = lax.conv_general_dilated(x, w[:, None, :].astype(x.dtype), window_strides=(1,),
                                 padding=[(SSM_CONV // 2, SSM_CONV // 2)],
                                 dimension_numbers=('NWC', 'WIO', 'NWC'), feature_group_count=ch)
    return y + bias


def _ssd(x, dt, a, bm, cm, h0):
    b, L, H, P = x.shape
    nc = L // SSM_CHUNK
    rep = H // SSM_GROUPS
    xc = x.astype(F32).reshape(b, nc, SSM_CHUNK, H, P)
    bc = jnp.repeat(bm.astype(F32), rep, axis=2).reshape(b, nc, SSM_CHUNK, H, SSM_STATE)
    cc = jnp.repeat(cm.astype(F32), rep, axis=2).reshape(b, nc, SSM_CHUNK, H, SSM_STATE)
    dtc = dt.reshape(b, nc, SSM_CHUNK, H)
    acum = jnp.cumsum(dtc * a, axis=2)
    seg = acum[:, :, :, None, :] - acum[:, :, None, :, :]
    lower = jnp.tril(jnp.ones((SSM_CHUNK, SSM_CHUNK), dtype=bool))[None, None, :, :, None]
    lmat = jnp.exp(jnp.where(lower, seg, -jnp.inf))
    cb = jnp.einsum('bcihn,bcjhn->bcijh', cc, bc)
    y_diag = jnp.einsum('bcijh,bcjhp->bcihp', cb * lmat * dtc[:, :, None, :, :], xc)
    decay_end = jnp.exp(acum[:, :, -1:, :] - acum) * dtc
    states = jnp.einsum('bcjhn,bcjh,bcjhp->bchpn', bc, decay_end, xc)
    chunk_decay = jnp.exp(acum[:, :, -1, :])

    def step(h, inp):
        dec, st = inp
        return dec[:, :, None, None] * h + st, h

    h_fin, h_in = lax.scan(step, h0.astype(F32), (jnp.moveaxis(chunk_decay, 1, 0), jnp.moveaxis(states, 1, 0)))
    h_in = jnp.moveaxis(h_in, 0, 1)
    y_off = jnp.einsum('bcihn,bchpn,bcih->bcihp', cc, h_in, jnp.exp(acum))
    return (y_diag + y_off).reshape(b, L, H, P), h_fin


def _ssm_gmlp_glue(zb_seq, dt_raw, p, h0):
    b, L, _ = zb_seq.shape
    z = zb_seq[..., ZB_Z:ZB_XBC].astype(F32)
    xbc = zb_seq[..., ZB_XBC:ZB_GM].astype(F32)
    gm = zb_seq[..., ZB_GM:ZB_GATE].astype(F32)

    xbc = jax.nn.silu(_dwconv(xbc, p['conv_w'], p['conv_b']))
    gn = SSM_GROUPS * SSM_STATE
    xs = xbc[..., :SSM_INNER].reshape(b, L, SSM_HEADS, SSM_HEADDIM)
    bm = xbc[..., SSM_INNER:SSM_INNER + gn].reshape(b, L, SSM_GROUPS, SSM_STATE)
    cm = xbc[..., SSM_INNER + gn:].reshape(b, L, SSM_GROUPS, SSM_STATE)
    dt = jax.nn.softplus(dt_raw.reshape(b, L, 2, SSM_HEADS) + p['dt_bias'].astype(F32))
    a = -jnp.exp(p['a_log'].astype(F32))
    if h0 is None:
        h0 = jnp.zeros((b, 2, SSM_HEADS, SSM_HEADDIM, SSM_STATE), F32)
    y_f, hf_f = _ssd(xs, dt[:, :, 0], a[0], bm, cm, h0[:, 0])
    y_b, hf_b = _ssd(jnp.flip(xs, 1), jnp.flip(dt[:, :, 1], 1), a[1], jnp.flip(bm, 1), jnp.flip(cm, 1), h0[:, 1])
    y = y_f + jnp.flip(y_b, 1) + p['d_skip'].astype(F32)[:, None] * xs
    y = y.reshape(b, L, SSM_INNER)
    ssm = _rmsnorm(y * jax.nn.silu(z), p['g_ssm'])

    gm = jax.nn.gelu(gm)
    u, vg = gm[..., :GM_WIDTH], gm[..., GM_WIDTH:]
    vg = _rmsnorm(vg, p['g_gv']).reshape(b, L // GM_CHUNK, GM_CHUNK, GM_GROUPS, GM_WIDTH // GM_GROUPS)
    sv = jnp.einsum('gij,bcjgd->bcigd', p['w_sp'], vg) + p['b_sp'].T[None, None, :, :, None]
    gmo = u * sv.reshape(b, L, GM_WIDTH)
    return ssm, gmo, jnp.stack([hf_f, hf_b], axis=1)


def _prep_w_in(w):
    o = [0, Q_RANK, Q_RANK + KV_RANK, Q_RANK + KV_RANK + ROPE_DIM]
    o.append(o[-1] + SSM_INNER)
    o.append(o[-1] + SSM_XBC)
    o.append(o[-1] + 2 * SSM_HEADS)
    o.append(o[-1] + 2 * GM_WIDTH)
    o.append(o[-1] + N_BRANCH * D_MODEL)
    seg = [w[:, o[k]:o[k + 1]] for k in range(8)]
    cq, ckv, kr, z, xbc, dt, gm, gates = seg
    wb = jnp.concatenate([cq, z, xbc, gm, gates], axis=1).astype(BF16)
    pad = jnp.zeros((D_MODEL, ZK_WIDTH - ROPE_DIM - 2 * SSM_HEADS), w.dtype)
    ws = jnp.concatenate([ckv, kr, dt, pad], axis=1).astype(BF16)
    return wb, ws


def _prep_w_uq(w):
    w = w.reshape(Q_RANK, MLA_HEADS, NOPE_DIM + ROPE_DIM)
    pad = jnp.zeros((Q_RANK, MLA_HEADS, QK_HEAD - NOPE_DIM - ROPE_DIM), w.dtype)
    return jnp.concatenate([w, pad], axis=2).reshape(Q_RANK, MLA_HEADS * QK_HEAD).astype(BF16)


def _prep_w_ukv(w):
    w = w.reshape(KV_RANK, MLA_HEADS, NOPE_DIM + V_DIM)
    return jnp.concatenate([w[:, :, :NOPE_DIM].reshape(KV_RANK, MLA_WIDTH),
                            w[:, :, NOPE_DIM:].reshape(KV_RANK, MLA_WIDTH)], axis=1).astype(BF16)


def _rope_tables(n_lat):
    rows = n_lat // GRID_W
    row = jnp.repeat(jnp.arange(rows), GRID_W).astype(F32)
    col = jnp.tile(jnp.arange(GRID_W), rows).astype(F32)
    nf = ROPE_DIM // 4
    freqs = jnp.power(ROPE_THETA, -jnp.arange(nf, dtype=F32) / nf)
    ang = jnp.stack([row[:, None] * freqs, col[:, None] * freqs], axis=1)
    cos, sin = jnp.cos(ang), jnp.sin(ang)
    c64 = jnp.concatenate([cos[:, 0], cos[:, 0], cos[:, 1], cos[:, 1]], axis=1)
    s64 = jnp.concatenate([-sin[:, 0], sin[:, 0], -sin[:, 1], sin[:, 1]], axis=1)
    c_lat = jnp.concatenate([c64, jnp.ones((n_lat, LANES - ROPE_DIM), F32)], axis=1)
    s_lat = jnp.concatenate([s64, jnp.zeros((n_lat, LANES - ROPE_DIM), F32)], axis=1)
    rope_c = jnp.concatenate([jnp.ones((N_CTX, LANES), F32), jnp.tile(c_lat, (DEC_BATCH, 1))], axis=0)
    rope_s = jnp.concatenate([jnp.zeros((N_CTX, LANES), F32), jnp.tile(s_lat, (DEC_BATCH, 1))], axis=0)
    return rope_c, rope_s


def _rows_mod(vec3, tm):
    idx = [0] * (N_CTX // tm) + [1] * (DEC_SEQ // tm) + [2] * (DEC_SEQ // tm)
    return vec3[jnp.array(idx)][:, None, :]


def kernel(x_prompt, x_sample, c, cache_ckv, cache_krope, state_ssm, c_ctx, w_mod, b_mod,
           g_norm1, g_norm2, w_in, g_qn, w_uq, g_kvn, w_ukv, conv_w, conv_b, dt_bias, a_log,
           d_skip, g_ssm, g_gv, w_sp, b_sp, w_br_attn, w_br_ssm, w_br_gmlp, w_out, w_router,
           w_gate, w_up, w_down, g_final):
    rope_c, rope_s = _rope_tables(DEC_SEQ)
    x = jnp.concatenate([x_prompt.reshape(N_CTX, D_MODEL), x_sample.reshape(N_LAT, D_MODEL)], axis=0)
    cond = jax.nn.silu(jnp.concatenate([c_ctx[None, :], c], axis=0))

    ckvs, krs, sts = [], [], []
    for l in range(DEPTH):
        p = {'conv_w': conv_w[l], 'conv_b': conv_b[l], 'dt_bias': dt_bias[l], 'a_log': a_log[l],
             'd_skip': d_skip[l], 'g_ssm': g_ssm[l], 'g_gv': g_gv[l], 'w_sp': w_sp[l], 'b_sp': b_sp[l]}
        mod = (cond @ w_mod[l] + b_mod[l]).reshape(3, 6, D_MODEL)
        sh1, sc1, gt1, sh2, sc2, gt2 = [mod[:, k] for k in range(6)]

        wb, ws = _prep_w_in(w_in[l])
        zb, ckv_raw, zk = _inproj(x, g_norm1[l][None, :], _rows_mod(sc1, IN_TM), _rows_mod(sh1, IN_TM), wb, ws)

        q = _qproj(zb, g_qn[l][None, :], _prep_w_uq(w_uq[l]), rope_c, rope_s)
        w_ukv_p = _prep_w_ukv(w_ukv[l])
        ckv, kv, krot = _kvproj(ckv_raw, zk, g_kvn[l][None, :], w_ukv_p, rope_c, rope_s)
        kv_cache = _matmul(cache_ckv[:, l].reshape(DEC_BATCH * PAST_LEN, KV_RANK), w_ukv_p,
                           out_dtype=BF16, tn=2 * MLA_WIDTH, name="mla_kv_cache")
        kr_cache = jnp.pad(cache_krope[:, l].reshape(DEC_BATCH * PAST_LEN, ROPE_DIM),
                           ((0, 0), (0, LANES - ROPE_DIM))).astype(BF16)
        attn_c = _attention(q, kv, krot, BATCH, SEQ, SEQ, 0)
        attn_l = _attention(q, kv, krot, DEC_BATCH, DEC_SEQ, 256, N_CTX, cache=(kv_cache, kr_cache, PAST_LEN))
        attn = jnp.concatenate([attn_c, attn_l], axis=0)

        dt_raw = zk[:, ROPE_DIM:ROPE_DIM + 2 * SSM_HEADS]
        ssm_c, gmo_c, st_c = _ssm_gmlp_glue(zb[:N_CTX].reshape(BATCH, SEQ, ZB_WIDTH),
                                            dt_raw[:N_CTX].reshape(BATCH, SEQ, -1), p, None)
        ssm_l, gmo_l, _ = _ssm_gmlp_glue(zb[N_CTX:].reshape(DEC_BATCH, DEC_SEQ, ZB_WIDTH),
                                         dt_raw[N_CTX:].reshape(DEC_BATCH, DEC_SEQ, -1), p, state_ssm[:, l])
        ssm = jnp.concatenate([ssm_c.reshape(N_CTX, -1), ssm_l.reshape(N_LAT, -1)], axis=0).astype(BF16)
        gmo = jnp.concatenate([gmo_c.reshape(N_CTX, -1), gmo_l.reshape(N_LAT, -1)], axis=0).astype(BF16)

        merged = _merge(attn, ssm, gmo, w_br_attn[l].astype(BF16), w_br_ssm[l].astype(BF16),
                        w_br_gmlp[l].astype(BF16), zb)
        x1, h2, logits_t = _outproj(merged, w_out[l].astype(BF16), x, _rows_mod(gt1, OUT_TM),
                                    g_norm2[l][None, :], _rows_mod(sc2, OUT_TM), _rows_mod(sh2, OUT_TM),
                                    w_router[l].T)

        slot_c, xe_c, ge_c = _route(logits_t, h2, BATCH, SEQ, CAP_CTX, 0)
        slot_l, xe_l, ge_l = _route(logits_t, h2, DEC_BATCH, DEC_SEQ, CAP_LAT, N_CTX)
        y_c, y_l = _expert_ffn(xe_c, xe_l, ge_c, ge_l, w_gate[l], w_up[l], w_down[l])
        gt2_c = jnp.broadcast_to(gt2[0][None, None, :], (BATCH, 1, D_MODEL))
        gt2_l = gt2[1:][:, None, :]
        xo_c = _combine(slot_c.T, y_c, x1, gt2_c, BATCH, SEQ, CAP_CTX, 0)
        xo_l = _combine(slot_l.T, y_l, x1, gt2_l, DEC_BATCH, DEC_SEQ, CAP_LAT, N_CTX)
        x = jnp.concatenate([xo_c, xo_l], axis=0)

        ckvs.append(ckv[:N_CTX].reshape(BATCH, SEQ, KV_RANK))
        krs.append(zk[:N_CTX, :ROPE_DIM].reshape(BATCH, SEQ, ROPE_DIM))
        sts.append(st_c)

    y = _rmsnorm(x, g_final)
    y_prompt = y[:N_CTX].reshape(BATCH, SEQ, D_MODEL)
    y_sample = y[N_CTX:].reshape(DEC_BATCH, DEC_SEQ, D_MODEL)
    return (y_prompt, y_sample, jnp.stack(ckvs, axis=1), jnp.stack(krs, axis=1), jnp.stack(sts, axis=1))
```

```python
import functools
import math

import jax
import jax.numpy as jnp
from jax import lax
from jax.experimental import pallas as pl
from jax.experimental.pallas import tpu as pltpu

D_MODEL = 2048
BATCH = 16
SEQ = 256
DEPTH = 2
DEC_BATCH = 2
DEC_SEQ = 2048
PAST_LEN = 512
GRID_W = 64
ROPE_THETA = 10000.0
EPS = 1e-6
MLA_HEADS = 8
Q_RANK = 512
KV_RANK = 256
NOPE_DIM = 128
ROPE_DIM = 64
V_DIM = 128
MLA_WIDTH = MLA_HEADS * V_DIM
SSM_HEADS = 16
SSM_HEADDIM = 64
SSM_INNER = SSM_HEADS * SSM_HEADDIM
SSM_GROUPS = 2
SSM_STATE = 128
SSM_CONV = 5
SSM_CHUNK = 128
SSM_XBC = SSM_INNER + 2 * SSM_GROUPS * SSM_STATE
GM_WIDTH = 1024
GM_GROUPS = 4
GM_CHUNK = 128
N_BRANCH = 3
N_EXPERTS = 16
EXPERT_FF = 1024
EC_CAPACITY = 2

N_CTX = BATCH * SEQ
N_LAT = DEC_BATCH * DEC_SEQ
N_TOK = N_CTX + N_LAT
CAP_CTX = EC_CAPACITY * SEQ // N_EXPERTS
CAP_LAT = EC_CAPACITY * DEC_SEQ // N_EXPERTS
ROWS_CTX = BATCH * CAP_CTX
ROWS_LAT = DEC_BATCH * CAP_LAT

LANES = 128
QK_HEAD = 2 * LANES

ZB_Z = 0
ZB_CQ = SSM_INNER
ZB_XBC = ZB_CQ + Q_RANK
ZB_GM = ZB_XBC + SSM_XBC
ZB_GATE = ZB_GM + 2 * GM_WIDTH
ZB_WIDTH = ZB_GATE + N_BRANCH * D_MODEL
ZK_WIDTH = LANES

V7X_VMEM_LIMIT_BYTES = 56 * 1024 * 1024

BF16 = jnp.bfloat16
F32 = jnp.float32


def _cparams(*sem):
    return pltpu.CompilerParams(dimension_semantics=sem, vmem_limit_bytes=V7X_VMEM_LIMIT_BYTES)


def _rms(x, g):
    return x * lax.rsqrt(jnp.mean(x * x, axis=-1, keepdims=True) + EPS) * g


def _swap16(x):
    lane = lax.broadcasted_iota(jnp.int32, x.shape, 1)
    return jnp.where((lane % 32) < 16, pltpu.roll(x, LANES - 16, 1), pltpu.roll(x, 16, 1))


def _mm_body(x_ref, w_ref, o_ref):
    o_ref[...] = jnp.dot(x_ref[...].astype(BF16), w_ref[...].astype(BF16),
                         preferred_element_type=F32).astype(o_ref.dtype)


def _matmul(x, w, out_dtype=F32, tm=512, tn=512, name="matmul"):
    m, k = x.shape
    _, n = w.shape
    tn = min(tn, n)
    return pl.pallas_call(
        _mm_body,
        grid=(m // tm, n // tn),
        in_specs=[pl.BlockSpec((tm, k), lambda i, j: (i, 0)),
                  pl.BlockSpec((k, tn), lambda i, j: (0, j))],
        out_specs=pl.BlockSpec((tm, tn), lambda i, j: (i, j)),
        out_shape=jax.ShapeDtypeStruct((m, n), out_dtype),
        compiler_params=_cparams("parallel", "arbitrary"),
        name=name,
    )(x, w)


IN_TM = 1024
IN_TN = 512
NORM_ROWS = 256


def _ctx_lat_specs(tm, width, n_grid_axes):
    n_ctx = N_CTX // tm
    if n_grid_axes == 1:
        return [pl.BlockSpec((tm, width), lambda i: (jnp.minimum(i, n_ctx - 1), 0)),
                pl.BlockSpec((tm, width), lambda i: (jnp.maximum(i - n_ctx, 0), 0))]
    return [pl.BlockSpec((tm, width), lambda i, j: (jnp.minimum(i, n_ctx - 1), 0)),
            pl.BlockSpec((tm, width), lambda i, j: (jnp.maximum(i - n_ctx, 0), 0))]


def _ctx_or_lat(tm, c_ref, l_ref, rows=slice(None)):
    return jnp.where(pl.program_id(0) < N_CTX // tm, c_ref[rows, :], l_ref[rows, :])


def _norm_mod_to(h_scr, xc_ref, xl_ref, g_ref, sc_ref, sh_ref):
    g = g_ref[...]
    mul = 1.0 + sc_ref[0]
    add = sh_ref[0]
    for r in range(0, IN_TM, NORM_ROWS):
        x = _ctx_or_lat(IN_TM, xc_ref, xl_ref, slice(r, r + NORM_ROWS))
        h_scr[r:r + NORM_ROWS, :] = (_rms(x, g) * mul + add).astype(BF16)


def _inproj_body(xc_ref, xl_ref, g_ref, sc_ref, sh_ref, w_ref, o_ref, h_scr):
    @pl.when(pl.program_id(1) == 0)
    def _():
        _norm_mod_to(h_scr, xc_ref, xl_ref, g_ref, sc_ref, sh_ref)

    o_ref[...] = jnp.dot(h_scr[...], w_ref[...], preferred_element_type=F32).astype(o_ref.dtype)


def _inproj_small_body(xc_ref, xl_ref, g_ref, sc_ref, sh_ref, w_ref, ckv_ref, zk_ref, h_scr):
    _norm_mod_to(h_scr, xc_ref, xl_ref, g_ref, sc_ref, sh_ref)
    acc = jnp.dot(h_scr[...], w_ref[...], preferred_element_type=F32)
    ckv_ref[...] = acc[:, :KV_RANK]
    zk_ref[...] = acc[:, KV_RANK:]


def _inproj(x_c, x_l, g, sc_t, sh_t, wb, ws):
    nt = N_TOK // IN_TM
    common = _ctx_lat_specs(IN_TM, D_MODEL, 2) + [
              pl.BlockSpec((1, D_MODEL), lambda i, j: (0, 0)),
              pl.BlockSpec((1, 1, D_MODEL), lambda i, j: (i, 0, 0)),
              pl.BlockSpec((1, 1, D_MODEL), lambda i, j: (i, 0, 0))]
    zb = pl.pallas_call(
        _inproj_body,
        grid=(nt, ZB_WIDTH // IN_TN),
        in_specs=common + [pl.BlockSpec((D_MODEL, IN_TN), lambda i, j: (0, j))],
        out_specs=pl.BlockSpec((IN_TM, IN_TN), lambda i, j: (i, j)),
        out_shape=jax.ShapeDtypeStruct((N_TOK, ZB_WIDTH), BF16),
        scratch_shapes=[pltpu.VMEM((IN_TM, D_MODEL), BF16)],
        compiler_params=_cparams("parallel", "arbitrary"),
        name="in_proj",
    )(x_c, x_l, g, sc_t, sh_t, wb)
    nsmall = KV_RANK + ZK_WIDTH
    ckv_raw, zk = pl.pallas_call(
        _inproj_small_body,
        grid=(nt, 1),
        in_specs=common + [pl.BlockSpec((D_MODEL, nsmall), lambda i, j: (0, 0))],
        out_specs=[pl.BlockSpec((IN_TM, KV_RANK), lambda i, j: (i, 0)),
                   pl.BlockSpec((IN_TM, ZK_WIDTH), lambda i, j: (i, 0))],
        out_shape=[jax.ShapeDtypeStruct((N_TOK, KV_RANK), F32),
                   jax.ShapeDtypeStruct((N_TOK, ZK_WIDTH), F32)],
        scratch_shapes=[pltpu.VMEM((IN_TM, D_MODEL), BF16)],
        compiler_params=_cparams("parallel", "arbitrary"),
        name="in_proj_small",
    )(x_c, x_l, g, sc_t, sh_t, ws)
    return zb, ckv_raw, zk


QKV_TM = 512
ATTN_SCALE = 1.0 / math.sqrt(NOPE_DIM + ROPE_DIM)


def _qproj_body(cq_ref, g_ref, w_ref, c_ref, s_ref, o_ref):
    qn = _rms(cq_ref[...].astype(F32), g_ref[...]).astype(BF16)
    q = jnp.dot(qn, w_ref[...], preferred_element_type=F32)
    c = c_ref[...]
    s = s_ref[...]
    for h in range(MLA_HEADS):
        lo = h * QK_HEAD
        r = q[:, lo + LANES:lo + QK_HEAD]
        o_ref[:, lo:lo + LANES] = (q[:, lo:lo + LANES] * ATTN_SCALE).astype(BF16)
        o_ref[:, lo + LANES:lo + QK_HEAD] = ((r * c + _swap16(r) * s) * ATTN_SCALE).astype(BF16)


def _qproj(zb, g_qn, w_uq_p, rope_c, rope_s):
    return pl.pallas_call(
        _qproj_body,
        grid=(N_TOK // QKV_TM,),
        in_specs=[pl.BlockSpec((QKV_TM, Q_RANK), lambda i: (i, ZB_CQ // Q_RANK)),
                  pl.BlockSpec((1, Q_RANK), lambda i: (0, 0)),
                  pl.BlockSpec((Q_RANK, MLA_HEADS * QK_HEAD), lambda i: (0, 0)),
                  pl.BlockSpec((QKV_TM, LANES), lambda i: (i, 0)),
                  pl.BlockSpec((QKV_TM, LANES), lambda i: (i, 0))],
        out_specs=pl.BlockSpec((QKV_TM, MLA_HEADS * QK_HEAD), lambda i: (i, 0)),
        out_shape=jax.ShapeDtypeStruct((N_TOK, MLA_HEADS * QK_HEAD), BF16),
        compiler_params=_cparams("parallel"),
        name="mla_q",
    )(zb, g_qn, w_uq_p, rope_c, rope_s)


def _kvproj_body(ckv_ref, zk_ref, g_ref, w_ref, c_ref, s_ref, ckv_o, kv_o, kr_o):
    ckv = _rms(ckv_ref[...], g_ref[...])
    ckv_o[...] = ckv
    kv_o[...] = jnp.dot(ckv.astype(BF16), w_ref[...], preferred_element_type=F32).astype(BF16)
    zk = zk_ref[...]
    is_rope = lax.broadcasted_iota(jnp.int32, zk.shape, 1) < ROPE_DIM
    kr = jnp.where(is_rope, zk, 0.0)
    rot = kr * c_ref[...] + _swap16(kr) * s_ref[...]
    kr_o[...] = jnp.where(is_rope, rot, 0.0).astype(BF16)


def _kvproj(ckv_raw, zk, g_kvn, w_ukv_p, rope_c, rope_s):
    return pl.pallas_call(
        _kvproj_body,
        grid=(N_TOK // QKV_TM,),
        in_specs=[pl.BlockSpec((QKV_TM, KV_RANK), lambda i: (i, 0)),
                  pl.BlockSpec((QKV_TM, ZK_WIDTH), lambda i: (i, 0)),
                  pl.BlockSpec((1, KV_RANK), lambda i: (0, 0)),
                  pl.BlockSpec((KV_RANK, 2 * MLA_WIDTH), lambda i: (0, 0)),
                  pl.BlockSpec((QKV_TM, LANES), lambda i: (i, 0)),
                  pl.BlockSpec((QKV_TM, LANES), lambda i: (i, 0))],
        out_specs=[pl.BlockSpec((QKV_TM, KV_RANK), lambda i: (i, 0)),
                   pl.BlockSpec((QKV_TM, 2 * MLA_WIDTH), lambda i: (i, 0)),
                   pl.BlockSpec((QKV_TM, LANES), lambda i: (i, 0))],
        out_shape=[jax.ShapeDtypeStruct((N_TOK, KV_RANK), F32),
                   jax.ShapeDtypeStruct((N_TOK, 2 * MLA_WIDTH), BF16),
                   jax.ShapeDtypeStruct((N_TOK, LANES), BF16)],
        compiler_params=_cparams("parallel"),
        name="mla_kv",
    )(ckv_raw, zk, g_kvn, w_ukv_p, rope_c, rope_s)


def _attn_body(n_parts, q_ref, *refs):
    o_ref = refs[3 * n_parts]
    for h in range(MLA_HEADS):
        qh = q_ref[:, h * QK_HEAD:(h + 1) * QK_HEAD]
        scores = []
        for p in range(n_parts):
            kn_ref, kr_ref = refs[3 * p], refs[3 * p + 1]
            kh = jnp.concatenate([kn_ref[:, h * LANES:(h + 1) * LANES], kr_ref[...]], axis=1)
            scores.append(lax.dot_general(qh, kh, (((1,), (1,)), ((), ())),
                                          preferred_element_type=F32))
        m = scores[0].max(axis=1, keepdims=True)
        for s in scores[1:]:
            m = jnp.maximum(m, s.max(axis=1, keepdims=True))
        den = 0.0
        acc = 0.0
        for p in range(n_parts):
            e = jnp.exp(scores[p] - m)
            den = den + e.sum(axis=1, keepdims=True)
            v_ref = refs[3 * p + 2]
            acc = acc + jnp.dot(e.astype(BF16), v_ref[:, h * LANES:(h + 1) * LANES],
                                preferred_element_type=F32)
        o_ref[:, h * LANES:(h + 1) * LANES] = (acc / den).astype(BF16)


def _attention(q, kv, kr, n_seq, seq_len, tq, row0, cache=None):
    nq = seq_len // tq
    q0 = row0 // tq
    s0 = row0 // seq_len
    in_specs = [pl.BlockSpec((tq, MLA_HEADS * QK_HEAD), lambda b, i: (q0 + b * nq + i, 0))]
    args = [q]
    if cache is not None:
        kv_c, kr_c, len_c = cache
        in_specs += [pl.BlockSpec((len_c, MLA_WIDTH), lambda b, i: (b, 0)),
                     pl.BlockSpec((len_c, LANES), lambda b, i: (b, 0)),
                     pl.BlockSpec((len_c, MLA_WIDTH), lambda b, i: (b, 1))]
        args += [kv_c, kr_c, kv_c]
    in_specs += [pl.BlockSpec((seq_len, MLA_WIDTH), lambda b, i: (s0 + b, 0)),
                 pl.BlockSpec((seq_len, LANES), lambda b, i: (s0 + b, 0)),
                 pl.BlockSpec((seq_len, MLA_WIDTH), lambda b, i: (s0 + b, 1))]
    args += [kv, kr, kv]
    n_parts = 1 if cache is None else 2
    return pl.pallas_call(
        functools.partial(_attn_body, n_parts),
        grid=(n_seq, nq),
        in_specs=in_specs,
        out_specs=pl.BlockSpec((tq, MLA_WIDTH), lambda b, i: (b * nq + i, 0)),
        out_shape=jax.ShapeDtypeStruct((n_seq * seq_len, MLA_WIDTH), BF16),
        compiler_params=_cparams("parallel", "arbitrary"),
        name="mla_attn_cache" if cache is not None else "mla_attn",
    )(*args)


MG_TM = 1024
MG_TN = 512
OUT_TM = 256


def _merge_body(ac_ref, al_ref, sc_ref, sl_ref, c_ref, wa_ref, ws_ref, wc_ref, ga_ref, gs_ref, gc_ref,
                o_ref):
    def branch(x, w_ref, gate_ref):
        y = jnp.dot(x, w_ref[...].astype(BF16), preferred_element_type=F32)
        return jax.nn.sigmoid(gate_ref[...].astype(F32)) * y

    o_ref[...] = (branch(_ctx_or_lat(MG_TM, ac_ref, al_ref), wa_ref, ga_ref)
                  + branch(_ctx_or_lat(MG_TM, sc_ref, sl_ref), ws_ref, gs_ref)
                  + branch(c_ref[...], wc_ref, gc_ref)).astype(BF16)


def _merge(attn_c, attn_l, ssm_c, ssm_l, gmo, wa, ws, wc, zb):
    g0 = ZB_GATE // MG_TN
    gstep = D_MODEL // MG_TN
    pair = _ctx_lat_specs(MG_TM, MLA_WIDTH, 2)
    wspec = pl.BlockSpec((MLA_WIDTH, MG_TN), lambda i, j: (0, j))
    return pl.pallas_call(
        _merge_body,
        grid=(N_TOK // MG_TM, gstep),
        in_specs=pair + pair + [pl.BlockSpec((MG_TM, GM_WIDTH), lambda i, j: (i, 0)),
                                wspec, wspec, wspec,
                                pl.BlockSpec((MG_TM, MG_TN), lambda i, j: (i, g0 + j)),
                                pl.BlockSpec((MG_TM, MG_TN), lambda i, j: (i, g0 + gstep + j)),
                                pl.BlockSpec((MG_TM, MG_TN), lambda i, j: (i, g0 + 2 * gstep + j))],
        out_specs=pl.BlockSpec((MG_TM, MG_TN), lambda i, j: (i, j)),
        out_shape=jax.ShapeDtypeStruct((N_TOK, D_MODEL), BF16),
        compiler_params=_cparams("parallel", "arbitrary"),
        name="branch_merge",
    )(attn_c, attn_l, ssm_c, ssm_l, gmo, wa, ws, wc, zb, zb, zb)


def _split_bf16(x):
    hi = x.astype(BF16)
    return hi, (x - hi.astype(F32)).astype(BF16)


def _outproj_body(m_ref, w_ref, xc_ref, xl_ref, gt_ref, g_ref, sc_ref, sh_ref, wr_ref,
                  x1_ref, h2_ref, lg_ref):
    mix = jnp.dot(m_ref[...], w_ref[...], preferred_element_type=F32)
    x1 = _ctx_or_lat(OUT_TM, xc_ref, xl_ref) + gt_ref[0] * mix
    x1_ref[...] = x1
    h2 = _rms(x1, g_ref[...]) * (1.0 + sc_ref[0]) + sh_ref[0]
    h2_ref[...] = h2.astype(BF16)
    h_hi, h_lo = _split_bf16(h2)
    w_hi, w_lo = _split_bf16(wr_ref[...])
    lg_ref[...] = (jnp.dot(h_hi, w_hi, preferred_element_type=F32)
                   + jnp.dot(h_lo, w_hi, preferred_element_type=F32)
                   + jnp.dot(h_hi, w_lo, preferred_element_type=F32))


def _outproj(merged, w_out, x_c, x_l, gt_t, g2, sc_t, sh_t, w_router_p):
    mspec = pl.BlockSpec((1, 1, D_MODEL), lambda i: (i, 0, 0))
    return pl.pallas_call(
        _outproj_body,
        grid=(N_TOK // OUT_TM,),
        in_specs=[pl.BlockSpec((OUT_TM, D_MODEL), lambda i: (i, 0)),
                  pl.BlockSpec((D_MODEL, D_MODEL), lambda i: (0, 0))]
                 + _ctx_lat_specs(OUT_TM, D_MODEL, 1) + [
                  mspec,
                  pl.BlockSpec((1, D_MODEL), lambda i: (0, 0)),
                  mspec, mspec,
                  pl.BlockSpec((D_MODEL, LANES), lambda i: (0, 0))],
        out_specs=[pl.BlockSpec((OUT_TM, D_MODEL), lambda i: (i, 0)),
                   pl.BlockSpec((OUT_TM, D_MODEL), lambda i: (i, 0)),
                   pl.BlockSpec((OUT_TM, LANES), lambda i: (i, 0))],
        out_shape=[jax.ShapeDtypeStruct((N_TOK, D_MODEL), F32),
                   jax.ShapeDtypeStruct((N_TOK, D_MODEL), BF16),
                   jax.ShapeDtypeStruct((N_TOK, LANES), F32)],
        compiler_params=_cparams("parallel"),
        name="out_proj",
    )(merged, w_out, x_c, x_l, gt_t, g2, sc_t, sh_t, w_router_p)


RANK_BLK = 256


def _route_body(cap, epb, lg_ref, h_ref, slot_ref, xe_ref, ge_ref, slot_scr, aff_scr):
    n = lg_ref.shape[0]
    step = pl.program_id(1)

    @pl.when(step == 0)
    def _():
        lane = lax.broadcasted_iota(jnp.int32, (n, LANES), 1)
        lg = jnp.where(lane < N_EXPERTS, lg_ref[...], -jnp.inf)
        e = jnp.exp(lg - lg.max(axis=1, keepdims=True))
        aff = e / e.sum(axis=1, keepdims=True)
        aff_t = aff.T
        aff_scr[...] = aff_t
        sub = lax.broadcasted_iota(jnp.int32, (RANK_BLK, RANK_BLK), 0)
        col = lax.broadcasted_iota(jnp.int32, (RANK_BLK, RANK_BLK), 1)
        earlier = col < sub
        lower_strict = earlier.astype(BF16)
        lane_b = lax.broadcasted_iota(jnp.int32, (RANK_BLK, LANES), 1)
        run = jnp.zeros((1, LANES), F32)
        blocks = []
        for r0 in range(0, n, RANK_BLK):
            rank = jnp.zeros((RANK_BLK, LANES), F32)
            for x in range(N_EXPERTS):
                mine = aff[r0:r0 + RANK_BLK, x:x + 1]
                others = aff_t[x:x + 1, :]
                near = others[:, r0:r0 + RANK_BLK]
                ahead = (near > mine) | ((near == mine) & earlier)
                cnt = jnp.sum(jnp.where(ahead, 1.0, 0.0), axis=1, keepdims=True)
                if r0 > 0:
                    cnt = cnt + jnp.sum(jnp.where(others[:, :r0] >= mine, 1.0, 0.0), axis=1, keepdims=True)
                if r0 + RANK_BLK < n:
                    cnt = cnt + jnp.sum(jnp.where(others[:, r0 + RANK_BLK:] > mine, 1.0, 0.0),
                                        axis=1, keepdims=True)
                rank = rank + jnp.where(lane_b == x, cnt, 0.0)
            sel = jnp.where((rank < cap) & (lane_b < N_EXPERTS), 1.0, 0.0)
            before = jnp.dot(lower_strict, sel.astype(BF16), preferred_element_type=F32) + run
            run = run + jnp.sum(sel, axis=0, keepdims=True)
            blocks.append(jnp.where(sel > 0.0, before, -1.0))
        slot = blocks[0] if len(blocks) == 1 else jnp.concatenate(blocks, axis=0)
        slot_ref[...] = slot.astype(jnp.int32)
        slot_scr[...] = slot.T

    row = lax.broadcasted_iota(jnp.int32, (cap, n), 0).astype(F32)
    for k in range(epb):
        x = step * epb + k
        onehot = slot_scr[pl.ds(x, 1), :] == row
        xe_ref[k] = jnp.dot(onehot.astype(BF16), h_ref[...], preferred_element_type=F32).astype(BF16)
        gate = jnp.sum(jnp.where(onehot, aff_scr[pl.ds(x, 1), :], 0.0), axis=1, keepdims=True)
        ge_ref[k] = jnp.broadcast_to(gate, (cap, LANES))


def _route(logits, h2, n_seq, seq_len, cap, row0, epb):
    s0 = row0 // seq_len
    return pl.pallas_call(
        functools.partial(_route_body, cap, epb),
        grid=(n_seq, N_EXPERTS // epb),
        in_specs=[pl.BlockSpec((seq_len, LANES), lambda b, x: (s0 + b, 0)),
                  pl.BlockSpec((seq_len, D_MODEL), lambda b, x: (s0 + b, 0))],
        out_specs=[pl.BlockSpec((seq_len, LANES), lambda b, x: (b, 0)),
                   pl.BlockSpec((epb, cap, D_MODEL), lambda b, x: (x, b, 0)),
                   pl.BlockSpec((epb, cap, LANES), lambda b, x: (x, b, 0))],
        out_shape=[jax.ShapeDtypeStruct((n_seq * seq_len, LANES), jnp.int32),
                   jax.ShapeDtypeStruct((N_EXPERTS, n_seq * cap, D_MODEL), BF16),
                   jax.ShapeDtypeStruct((N_EXPERTS, n_seq * cap, LANES), F32)],
        scratch_shapes=[pltpu.VMEM((LANES, seq_len), F32),
                        pltpu.VMEM((LANES, seq_len), F32)],
        compiler_params=_cparams("parallel", "arbitrary"),
        name="moe_route_gather",
    )(logits, h2)


FFN_TF = 256


def _ffn_body(xc_ref, xl_ref, gc_ref, gl_ref, wg_ref, wu_ref, wd_ref, yc_ref, yl_ref, acc):
    f = pl.program_id(1)
    x = jnp.concatenate([xc_ref[0], xl_ref[0]], axis=0)
    gate = jnp.dot(x, wg_ref[0].astype(BF16), preferred_element_type=F32)
    up = jnp.dot(x, wu_ref[0].astype(BF16), preferred_element_type=F32)
    hid = (gate * jax.nn.sigmoid(gate) * up).astype(BF16)
    part = jnp.dot(hid, wd_ref[0].astype(BF16), preferred_element_type=F32)

    @pl.when(f == 0)
    def _():
        acc[...] = part

    @pl.when(f > 0)
    def _():
        acc[...] += part

    @pl.when(f == pl.num_programs(1) - 1)
    def _():
        yc_ref[0] = (acc[:ROWS_CTX, :] * gc_ref[0][:, :1]).astype(BF16)
        yl_ref[0] = (acc[ROWS_CTX:, :] * gl_ref[0][:, :1]).astype(BF16)


def _expert_ffn(xe_c, xe_l, ge_c, ge_l, w_gate, w_up, w_down):
    def xspec(rows, width):
        return pl.BlockSpec((1, rows, width), lambda i, f: (i, 0, 0))

    return pl.pallas_call(
        _ffn_body,
        grid=(N_EXPERTS, EXPERT_FF // FFN_TF),
        in_specs=[xspec(ROWS_CTX, D_MODEL), xspec(ROWS_LAT, D_MODEL),
                  xspec(ROWS_CTX, LANES), xspec(ROWS_LAT, LANES),
                  pl.BlockSpec((1, D_MODEL, FFN_TF), lambda i, f: (i, 0, f)),
                  pl.BlockSpec((1, D_MODEL, FFN_TF), lambda i, f: (i, 0, f)),
                  pl.BlockSpec((1, FFN_TF, D_MODEL), lambda i, f: (i, f, 0))],
        out_specs=[xspec(ROWS_CTX, D_MODEL), xspec(ROWS_LAT, D_MODEL)],
        out_shape=[jax.ShapeDtypeStruct((N_EXPERTS, ROWS_CTX, D_MODEL), BF16),
                   jax.ShapeDtypeStruct((N_EXPERTS, ROWS_LAT, D_MODEL), BF16)],
        scratch_shapes=[pltpu.VMEM((ROWS_CTX + ROWS_LAT, D_MODEL), F32)],
        compiler_params=_cparams("parallel", "arbitrary"),
        name="expert_ffn",
    )(xe_c, xe_l, ge_c, ge_l, w_gate, w_up, w_down)


CMB_TN = 512


def _combine_body(cap, slot_ref, y_ref, x_ref, gt_ref, o_ref, acc):
    n = slot_ref.shape[0]
    slot = slot_ref[...]
    col = lax.broadcasted_iota(jnp.int32, (n, cap), 1)
    for x in range(N_EXPERTS):
        onehot = (slot[:, x:x + 1] == col).astype(BF16)
        part = jnp.dot(onehot, y_ref[x], preferred_element_type=F32)
        if x == 0:
            acc[...] = part
        else:
            acc[...] += part
    o_ref[...] = x_ref[...] + gt_ref[0] * acc[...]


def _combine(slot_t, y, x1, gt_t, n_seq, seq_len, cap, row0):
    s0 = row0 // seq_len
    return pl.pallas_call(
        functools.partial(_combine_body, cap),
        grid=(n_seq, D_MODEL // CMB_TN),
        in_specs=[pl.BlockSpec((seq_len, LANES), lambda b, j: (b, 0)),
                  pl.BlockSpec((N_EXPERTS, cap, CMB_TN), lambda b, j: (0, b, j)),
                  pl.BlockSpec((seq_len, CMB_TN), lambda b, j: (s0 + b, j)),
                  pl.BlockSpec((1, 1, CMB_TN), lambda b, j: (b, 0, j))],
        out_specs=pl.BlockSpec((seq_len, CMB_TN), lambda b, j: (b, j)),
        out_shape=jax.ShapeDtypeStruct((n_seq * seq_len, D_MODEL), F32),
        scratch_shapes=[pltpu.VMEM((seq_len, CMB_TN), F32)],
        compiler_params=_cparams("parallel", "arbitrary"),
        name="moe_combine",
    )(slot_t, y, x1, gt_t)


CONV_TN = 256
CONV_HALO = 8
DT_LANE = ROPE_DIM


def _conv_body(x_ref, w_ref, b_ref, o_ref):
    seq_len = x_ref.shape[0]
    halo = jnp.zeros((CONV_HALO, CONV_TN), F32)
    ext = jnp.concatenate([halo, x_ref[...].astype(F32), halo], axis=0)
    w = w_ref[...]
    y = b_ref[...]
    for k in range(SSM_CONV):
        lo = CONV_HALO - SSM_CONV // 2 + k
        y = y + w[k:k + 1, :] * ext[lo:lo + seq_len, :]
    o_ref[...] = (y * jax.nn.sigmoid(y)).astype(BF16)


def _conv_silu(zb, conv_w8, conv_b, n_seq, seq_len, row0):
    s0 = row0 // seq_len
    c0 = ZB_XBC // CONV_TN
    return pl.pallas_call(
        _conv_body,
        grid=(n_seq, SSM_XBC // CONV_TN),
        in_specs=[pl.BlockSpec((seq_len, CONV_TN), lambda b, j: (s0 + b, c0 + j)),
                  pl.BlockSpec((8, CONV_TN), lambda b, j: (0, j)),
                  pl.BlockSpec((1, CONV_TN), lambda b, j: (0, j))],
        out_specs=pl.BlockSpec((seq_len, CONV_TN), lambda b, j: (b, j)),
        out_shape=jax.ShapeDtypeStruct((n_seq * seq_len, SSM_XBC), BF16),
        compiler_params=_cparams("parallel", "arbitrary"),
        name="ssm_conv",
    )(zb, conv_w8, conv_b)


def _split3(x):
    hi = x.astype(BF16)
    r = x - hi.astype(F32)
    mid = r.astype(BF16)
    return hi, mid, (r - mid.astype(F32)).astype(BF16)


def _ssd_body(nc, xa_ref, zk_ref, z_ref, h0_ref, bias_ref, a_ref, d_ref, g_ref, ef_ref, eb_ref,
              o_ref, st_ref, hb_in, hf_cur, hb_cur):
    q = SSM_CHUNK
    half = SSM_INNER // SSM_GROUPS
    ii = lax.broadcasted_iota(jnp.int32, (q, q), 0)
    jj = lax.broadcasted_iota(jnp.int32, (q, q), 1)
    lower = ii >= jj
    upper = ii <= jj
    lower_b = lower.astype(BF16)
    upper_b = upper.astype(BF16)
    lane = lax.broadcasted_iota(jnp.int32, (q, LANES), 1)
    is_dt = (lane >= DT_LANE) & (lane < DT_LANE + 2 * SSM_HEADS)
    is_fwd = lane < DT_LANE + SSM_HEADS

    def tri_cumsum(tri, v):
        hi, mid, lo = _split3(v)
        return (jnp.dot(tri, hi, preferred_element_type=F32) + jnp.dot(tri, mid, preferred_element_type=F32)
                + jnp.dot(tri, lo, preferred_element_type=F32))

    def expand(v, e_ref):
        hi, lo = _split_bf16(v)
        return (jnp.dot(hi, e_ref[...], preferred_element_type=F32)
                + jnp.dot(lo, e_ref[...], preferred_element_type=F32))

    def chunk_factors(r0):
        dt = jnp.where(is_dt, jax.nn.softplus(zk_ref[pl.ds(r0, q), :] + bias_ref[...]), 0.0)
        dta = dt * a_ref[...]
        cum = jnp.where(is_fwd, tri_cumsum(lower_b, dta), tri_cumsum(upper_b, dta))
        tot = jnp.sum(dta, axis=0, keepdims=True)
        return dt, cum, tot

    def state_update(r0, dend_x, cdec_x, h_prev):
        x = xa_ref[pl.ds(r0, q), 0:SSM_INNER].astype(F32)
        xs = (x * dend_x).astype(BF16)
        parts = []
        for g in range(SSM_GROUPS):
            lo = SSM_INNER + g * SSM_STATE
            b_t = xa_ref[pl.ds(r0, q), lo:lo + SSM_STATE].astype(F32).T.astype(BF16)
            parts.append(jnp.dot(b_t, xs[:, g * half:(g + 1) * half], preferred_element_type=F32))
        return cdec_x * h_prev + jnp.concatenate(parts, axis=1)

    hf_cur[...] = h0_ref[0, 0]
    hb_cur[...] = h0_ref[0, 1]

    def bwd_step(t, carry):
        c = nc - 1 - t
        r0 = pl.multiple_of(c * q, q)
        dt, cum, tot = chunk_factors(r0)
        hb_in[c] = hb_cur[...]
        dend_x = expand(jnp.exp(tot - cum) * dt, eb_ref)
        cdec_x = expand(jnp.broadcast_to(jnp.exp(tot), (8, LANES)), eb_ref)[0:1]
        hb_cur[...] = state_update(r0, dend_x, cdec_x, hb_cur[...])
        return carry

    lax.fori_loop(0, nc, bwd_step, 0)

    def fwd_step(c, carry):
        r0 = pl.multiple_of(c * q, q)
        dt, cum, tot = chunk_factors(r0)
        cum_t = cum.T
        dt_t = dt.T
        eoff = jnp.exp(cum)
        x_bf = xa_ref[pl.ds(r0, q), 0:SSM_INNER]
        cb = []
        c_bf = []
        for g in range(SSM_GROUPS):
            lo_b = SSM_INNER + g * SSM_STATE
            lo_c = SSM_INNER + (SSM_GROUPS + g) * SSM_STATE
            c_g = xa_ref[pl.ds(r0, q), lo_c:lo_c + SSM_STATE]
            b_g = xa_ref[pl.ds(r0, q), lo_b:lo_b + SSM_STATE]
            c_bf.append(c_g)
            cb.append(lax.dot_general(c_g, b_g, (((1,), (1,)), ((), ())), preferred_element_type=F32))

        def head_matrix(h):
            f = DT_LANE + h
            b = DT_LANE + SSM_HEADS + h
            lf = jnp.where(lower, jnp.exp(cum[:, f:f + 1] - cum_t[f:f + 1, :]), 0.0) * dt_t[f:f + 1, :]
            lb = jnp.where(upper, jnp.exp(cum[:, b:b + 1] - cum_t[b:b + 1, :]), 0.0) * dt_t[b:b + 1, :]
            return (cb[h // (SSM_HEADS // SSM_GROUPS)] * (lf + lb)).astype(BF16)

        lane_lo = lane < SSM_HEADDIM
        pairs = []
        for hp in range(SSM_HEADS // 2):
            x_pair = x_bf[:, hp * LANES:(hp + 1) * LANES]
            y0 = jnp.dot(head_matrix(2 * hp), x_pair, preferred_element_type=F32)
            y1 = jnp.dot(head_matrix(2 * hp + 1), x_pair, preferred_element_type=F32)
            pairs.append(jnp.where(lane_lo, y0, y1))
        y = jnp.concatenate(pairs, axis=1)

        def off_diag(h_t, factor_x):
            h_bf = h_t.astype(BF16)
            parts = [jnp.dot(c_bf[g], h_bf[:, g * half:(g + 1) * half], preferred_element_type=F32)
                     for g in range(SSM_GROUPS)]
            return jnp.concatenate(parts, axis=1) * factor_x

        y = y + off_diag(hf_cur[...], expand(eoff, ef_ref)) + off_diag(hb_in[c], expand(eoff, eb_ref))
        y = y + d_ref[...] * x_bf.astype(F32)
        zg = z_ref[pl.ds(r0, q), :].astype(F32)
        o_ref[pl.ds(r0, q), :] = _rms(y * (zg * jax.nn.sigmoid(zg)), g_ref[...]).astype(BF16)

        dend_x = expand(jnp.exp(tot - cum) * dt, ef_ref)
        cdec_x = expand(jnp.broadcast_to(jnp.exp(tot), (8, LANES)), ef_ref)[0:1]
        hf_cur[...] = state_update(r0, dend_x, cdec_x, hf_cur[...])
        return carry

    lax.fori_loop(0, nc, fwd_step, 0)
    st_ref[0, 0] = hf_cur[...]
    st_ref[0, 1] = hb_cur[...]


def _ssd(xa, zk, zb, h0_t, dt_bias_p, a_p, d_x, g_ssm, e_f, e_b, n_seq, seq_len, row0):
    s0 = row0 // seq_len
    nc = seq_len // SSM_CHUNK
    vec = lambda w: pl.BlockSpec((1, w), lambda b: (0, 0))
    return pl.pallas_call(
        functools.partial(_ssd_body, nc),
        grid=(n_seq,),
        in_specs=[pl.BlockSpec((seq_len, SSM_XBC), lambda b: (b, 0)),
                  pl.BlockSpec((seq_len, ZK_WIDTH), lambda b: (s0 + b, 0)),
                  pl.BlockSpec((seq_len, SSM_INNER), lambda b: (s0 + b, ZB_Z // SSM_INNER)),
                  pl.BlockSpec((1, 2, SSM_STATE, SSM_INNER), lambda b: (b, 0, 0, 0)),
                  vec(LANES), vec(LANES), vec(SSM_INNER), vec(SSM_INNER),
                  pl.BlockSpec((LANES, SSM_INNER), lambda b: (0, 0)),
                  pl.BlockSpec((LANES, SSM_INNER), lambda b: (0, 0))],
        out_specs=[pl.BlockSpec((seq_len, SSM_INNER), lambda b: (b, 0)),
                   pl.BlockSpec((1, 2, SSM_STATE, SSM_INNER), lambda b: (b, 0, 0, 0))],
        out_shape=[jax.ShapeDtypeStruct((n_seq * seq_len, SSM_INNER), BF16),
                   jax.ShapeDtypeStruct((n_seq, 2, SSM_STATE, SSM_INNER), F32)],
        scratch_shapes=[pltpu.VMEM((nc, SSM_STATE, SSM_INNER), F32),
                        pltpu.VMEM((SSM_STATE, SSM_INNER), F32),
                        pltpu.VMEM((SSM_STATE, SSM_INNER), F32)],
        compiler_params=_cparams("parallel"),
        name="ssd",
    )(xa, zk, zb, h0_t, dt_bias_p, a_p, d_x, g_ssm, e_f, e_b)


GM_TM = 512
GM_GROUP_W = GM_WIDTH // GM_GROUPS


def _gmlp_body(u_ref, v_ref, g_ref, w_ref, b_ref, o_ref):
    for r in range(0, GM_TM, GM_CHUNK):
        u = jax.nn.gelu(u_ref[r:r + GM_CHUNK, :].astype(F32))
        vg = _rms(jax.nn.gelu(v_ref[r:r + GM_CHUNK, :].astype(F32)), g_ref[...]).astype(BF16)
        sv = jnp.concatenate(
            [jnp.dot(w_ref[k], vg[:, k * GM_GROUP_W:(k + 1) * GM_GROUP_W], preferred_element_type=F32)
             for k in range(GM_GROUPS)], axis=1)
        o_ref[r:r + GM_CHUNK, :] = (u * (sv + b_ref[...])).astype(BF16)


def _gmlp(zb, g_gv, w_sp, b_x):
    return pl.pallas_call(
        _gmlp_body,
        grid=(N_TOK // GM_TM,),
        in_specs=[pl.BlockSpec((GM_TM, GM_WIDTH), lambda i: (i, ZB_GM // GM_WIDTH)),
                  pl.BlockSpec((GM_TM, GM_WIDTH), lambda i: (i, ZB_GM // GM_WIDTH + 1)),
                  pl.BlockSpec((1, GM_WIDTH), lambda i: (0, 0)),
                  pl.BlockSpec((GM_GROUPS, GM_CHUNK, GM_CHUNK), lambda i: (0, 0, 0)),
                  pl.BlockSpec((GM_CHUNK, GM_WIDTH), lambda i: (0, 0))],
        out_specs=pl.BlockSpec((GM_TM, GM_WIDTH), lambda i: (i, 0)),
        out_shape=jax.ShapeDtypeStruct((N_TOK, GM_WIDTH), BF16),
        compiler_params=_cparams("parallel"),
        name="gmlp",
    )(zb, zb, g_gv, w_sp, b_x)


FN_TM = 512


def _final_norm_body(x_ref, g_ref, o_ref):
    o_ref[...] = _rms(x_ref[...], g_ref[...])


def _final_norm(x, g):
    rows = x.shape[0]
    return pl.pallas_call(
        _final_norm_body,
        grid=(rows // FN_TM,),
        in_specs=[pl.BlockSpec((FN_TM, D_MODEL), lambda i: (i, 0)),
                  pl.BlockSpec((1, D_MODEL), lambda i: (0, 0))],
        out_specs=pl.BlockSpec((FN_TM, D_MODEL), lambda i: (i, 0)),
        out_shape=jax.ShapeDtypeStruct((rows, D_MODEL), F32),
        compiler_params=_cparams("parallel"),
        name="final_norm",
    )(x, g)


def _dt_lanes(v):
    return jnp.pad(v.reshape(1, 2 * SSM_HEADS).astype(F32),
                   ((0, 0), (DT_LANE, LANES - DT_LANE - 2 * SSM_HEADS)))


def _head_expanders():
    lane = lax.broadcasted_iota(jnp.int32, (LANES, SSM_INNER), 0)
    head = lax.broadcasted_iota(jnp.int32, (LANES, SSM_INNER), 1) // SSM_HEADDIM
    e_f = (lane == head + DT_LANE).astype(BF16)
    e_b = (lane == head + DT_LANE + SSM_HEADS).astype(BF16)
    return e_f, e_b


W_IN_SEGS = {}
_off = 0
for _name, _w in (("cq", Q_RANK), ("ckv", KV_RANK), ("kr", ROPE_DIM), ("z", SSM_INNER), ("xbc", SSM_XBC),
                  ("dt", 2 * SSM_HEADS), ("gm", 2 * GM_WIDTH), ("gates", N_BRANCH * D_MODEL)):
    W_IN_SEGS[_name] = (_off, _w)
    _off += _w
N_IN = _off
PREP_TK = 256


def _prep_w_in_body(w_ref, wb_ref, ws_ref):
    def seg(name):
        start, width = W_IN_SEGS[name]
        return w_ref[:, start:start + width].astype(BF16)

    o = 0
    for name in ("z", "cq", "xbc", "gm", "gates"):
        width = W_IN_SEGS[name][1]
        wb_ref[:, o:o + width] = seg(name)
        o += width
    o = 0
    for name in ("ckv", "kr", "dt"):
        width = W_IN_SEGS[name][1]
        ws_ref[:, o:o + width] = seg(name)
        o += width
    ws_ref[:, o:] = jnp.zeros((PREP_TK, KV_RANK + ZK_WIDTH - o), BF16)


def _prep_w_in(w):
    nsmall = KV_RANK + ZK_WIDTH
    return pl.pallas_call(
        _prep_w_in_body,
        grid=(D_MODEL // PREP_TK,),
        in_specs=[pl.BlockSpec((PREP_TK, N_IN), lambda i: (i, 0))],
        out_specs=[pl.BlockSpec((PREP_TK, ZB_WIDTH), lambda i: (i, 0)),
                   pl.BlockSpec((PREP_TK, nsmall), lambda i: (i, 0))],
        out_shape=[jax.ShapeDtypeStruct((D_MODEL, ZB_WIDTH), BF16),
                   jax.ShapeDtypeStruct((D_MODEL, nsmall), BF16)],
        compiler_params=_cparams("parallel"),
        name="w_in_prep",
    )(w)


MOD_TN = 1024
MOD_ROWS = 8


def _mod_body(c_ref, w_ref, b_ref, o_ref):
    c = c_ref[...]
    act = (c * jax.nn.sigmoid(c)).astype(BF16)
    o_ref[0] = jnp.dot(act, w_ref[0].astype(BF16), preferred_element_type=F32) + b_ref[0]


def _modulation(cond, w_mod, b_mod):
    n = 6 * D_MODEL
    return pl.pallas_call(
        _mod_body,
        grid=(DEPTH, n // MOD_TN),
        in_specs=[pl.BlockSpec((MOD_ROWS, D_MODEL), lambda l, j: (0, 0)),
                  pl.BlockSpec((1, D_MODEL, MOD_TN), lambda l, j: (l, 0, j)),
                  pl.BlockSpec((1, 1, MOD_TN), lambda l, j: (l, 0, j))],
        out_specs=pl.BlockSpec((1, MOD_ROWS, MOD_TN), lambda l, j: (l, 0, j)),
        out_shape=jax.ShapeDtypeStruct((DEPTH, MOD_ROWS, n), F32),
        compiler_params=_cparams("parallel", "arbitrary"),
        name="modulation",
    )(cond, w_mod, b_mod[:, None, :])


def _prep_w_uq(w):
    w = w.reshape(Q_RANK, MLA_HEADS, NOPE_DIM + ROPE_DIM)
    pad = jnp.zeros((Q_RANK, MLA_HEADS, QK_HEAD - NOPE_DIM - ROPE_DIM), w.dtype)
    return jnp.concatenate([w, pad], axis=2).reshape(Q_RANK, MLA_HEADS * QK_HEAD).astype(BF16)


def _prep_w_ukv(w):
    w = w.reshape(KV_RANK, MLA_HEADS, NOPE_DIM + V_DIM)
    return jnp.concatenate([w[:, :, :NOPE_DIM].reshape(KV_RANK, MLA_WIDTH),
                            w[:, :, NOPE_DIM:].reshape(KV_RANK, MLA_WIDTH)], axis=1).astype(BF16)


def _rope_tables(n_lat):
    rows = n_lat // GRID_W
    row = jnp.repeat(jnp.arange(rows), GRID_W).astype(F32)
    col = jnp.tile(jnp.arange(GRID_W), rows).astype(F32)
    nf = ROPE_DIM // 4
    freqs = jnp.power(ROPE_THETA, -jnp.arange(nf, dtype=F32) / nf)
    ang = jnp.stack([row[:, None] * freqs, col[:, None] * freqs], axis=1)
    cos, sin = jnp.cos(ang), jnp.sin(ang)
    c64 = jnp.concatenate([cos[:, 0], cos[:, 0], cos[:, 1], cos[:, 1]], axis=1)
    s64 = jnp.concatenate([-sin[:, 0], sin[:, 0], -sin[:, 1], sin[:, 1]], axis=1)
    c_lat = jnp.concatenate([c64, jnp.ones((n_lat, LANES - ROPE_DIM), F32)], axis=1)
    s_lat = jnp.concatenate([s64, jnp.zeros((n_lat, LANES - ROPE_DIM), F32)], axis=1)
    rope_c = jnp.concatenate([jnp.ones((N_CTX, LANES), F32), jnp.tile(c_lat, (DEC_BATCH, 1))], axis=0)
    rope_s = jnp.concatenate([jnp.zeros((N_CTX, LANES), F32), jnp.tile(s_lat, (DEC_BATCH, 1))], axis=0)
    return rope_c, rope_s


def _rows_mod(vec3, tm):
    idx = [0] * (N_CTX // tm) + [1] * (DEC_SEQ // tm) + [2] * (DEC_SEQ // tm)
    return vec3[jnp.array(idx)][:, None, :]


def kernel(x_prompt, x_sample, c, cache_ckv, cache_krope, state_ssm, c_ctx, w_mod, b_mod,
           g_norm1, g_norm2, w_in, g_qn, w_uq, g_kvn, w_ukv, conv_w, conv_b, dt_bias, a_log,
           d_skip, g_ssm, g_gv, w_sp, b_sp, w_br_attn, w_br_ssm, w_br_gmlp, w_out, w_router,
           w_gate, w_up, w_down, g_final):
    rope_c, rope_s = _rope_tables(DEC_SEQ)
    x_c = x_prompt.reshape(N_CTX, D_MODEL)
    x_l = x_sample.reshape(N_LAT, D_MODEL)
    cond = jnp.concatenate([c_ctx[None, :], c, jnp.zeros((MOD_ROWS - 1 - DEC_BATCH, D_MODEL), F32)], axis=0)
    mod_all = _modulation(cond, w_mod, b_mod)
    e_f, e_b = _head_expanders()
    h0_c = jnp.zeros((BATCH, 2, SSM_STATE, SSM_INNER), F32)

    ckvs, krs, sts = [], [], []
    for l in range(DEPTH):
        mod = mod_all[l, :1 + DEC_BATCH].reshape(1 + DEC_BATCH, 6, D_MODEL)
        sh1, sc1, gt1, sh2, sc2, gt2 = [mod[:, k] for k in range(6)]

        wb, ws = _prep_w_in(w_in[l])
        zb, ckv_raw, zk = _inproj(x_c, x_l, g_norm1[l][None, :], _rows_mod(sc1, IN_TM), _rows_mod(sh1, IN_TM),
                                  wb, ws)

        q = _qproj(zb, g_qn[l][None, :], _prep_w_uq(w_uq[l]), rope_c, rope_s)
        w_ukv_p = _prep_w_ukv(w_ukv[l])
        ckv, kv, krot = _kvproj(ckv_raw, zk, g_kvn[l][None, :], w_ukv_p, rope_c, rope_s)
        kv_cache = _matmul(cache_ckv[:, l].reshape(DEC_BATCH * PAST_LEN, KV_RANK), w_ukv_p,
                           out_dtype=BF16, tn=2 * MLA_WIDTH, name="mla_kv_cache")
        kr_cache = jnp.pad(cache_krope[:, l].reshape(DEC_BATCH * PAST_LEN, ROPE_DIM),
                           ((0, 0), (0, LANES - ROPE_DIM))).astype(BF16)
        attn_c = _attention(q, kv, krot, BATCH, SEQ, SEQ, 0)
        attn_l = _attention(q, kv, krot, DEC_BATCH, DEC_SEQ, 256, N_CTX, cache=(kv_cache, kr_cache, PAST_LEN))

        conv_w8 = jnp.pad(conv_w[l], ((0, 8 - SSM_CONV), (0, 0)))
        ssd_par = (_dt_lanes(dt_bias[l]), _dt_lanes(-jnp.exp(a_log[l])),
                   jnp.repeat(d_skip[l], SSM_HEADDIM)[None, :], g_ssm[l][None, :], e_f, e_b)
        xa_c = _conv_silu(zb, conv_w8, conv_b[l][None, :], BATCH, SEQ, 0)
        xa_l = _conv_silu(zb, conv_w8, conv_b[l][None, :], DEC_BATCH, DEC_SEQ, N_CTX)
        h0_l = jnp.transpose(state_ssm[:, l], (0, 1, 4, 2, 3)).reshape(DEC_BATCH, 2, SSM_STATE, SSM_INNER)
        ssm_c, st_c = _ssd(xa_c, zk, zb, h0_c, *ssd_par, BATCH, SEQ, 0)
        ssm_l, _ = _ssd(xa_l, zk, zb, h0_l, *ssd_par, DEC_BATCH, DEC_SEQ, N_CTX)
        gmo = _gmlp(zb, g_gv[l][None, :], w_sp[l].astype(BF16), jnp.repeat(b_sp[l].T, GM_GROUP_W, axis=1))
        st_c = jnp.transpose(st_c.reshape(BATCH, 2, SSM_STATE, SSM_HEADS, SSM_HEADDIM), (0, 1, 3, 4, 2))

        merged = _merge(attn_c, attn_l, ssm_c, ssm_l, gmo, w_br_attn[l], w_br_ssm[l], w_br_gmlp[l], zb)
        w_router_p = jnp.pad(w_router[l], ((0, 0), (0, LANES - N_EXPERTS)))
        x1, h2, logits = _outproj(merged, w_out[l].astype(BF16), x_c, x_l, _rows_mod(gt1, OUT_TM),
                                  g_norm2[l][None, :], _rows_mod(sc2, OUT_TM), _rows_mod(sh2, OUT_TM),
                                  w_router_p)

        slot_c, xe_c, ge_c = _route(logits, h2, BATCH, SEQ, CAP_CTX, 0, N_EXPERTS)
        slot_l, xe_l, ge_l = _route(logits, h2, DEC_BATCH, DEC_SEQ, CAP_LAT, N_CTX, 2)
        y_c, y_l = _expert_ffn(xe_c, xe_l, ge_c, ge_l, w_gate[l], w_up[l], w_down[l])
        gt2_c = jnp.broadcast_to(gt2[0][None, None, :], (BATCH, 1, D_MODEL))
        gt2_l = gt2[1:][:, None, :]
        x_c = _combine(slot_c, y_c, x1, gt2_c, BATCH, SEQ, CAP_CTX, 0)
        x_l = _combine(slot_l, y_l, x1, gt2_l, DEC_BATCH, DEC_SEQ, CAP_LAT, N_CTX)

        ckvs.append(ckv[:N_CTX].reshape(BATCH, SEQ, KV_RANK))
        krs.append(zk[:N_CTX, :ROPE_DIM].reshape(BATCH, SEQ, ROPE_DIM))
        sts.append(st_c)

    y_prompt = _final_norm(x_c, g_final[None, :]).reshape(BATCH, SEQ, D_MODEL)
    y_sample = _final_norm(x_l, g_final[None, :]).reshape(DEC_BATCH, DEC_SEQ, D_MODEL)
    return (y_prompt, y_sample, jnp.stack(ckvs, axis=1), jnp.stack(krs, axis=1), jnp.stack(sts, axis=1))
```

```python
import functools
import math

import jax
import jax.numpy as jnp
from jax import lax
from jax.experimental import pallas as pl
from jax.experimental.pallas import tpu as pltpu

D_MODEL = 2048
BATCH = 16
SEQ = 256
DEPTH = 2
DEC_BATCH = 2
DEC_SEQ = 2048
PAST_LEN = 512
GRID_W = 64
ROPE_THETA = 10000.0
EPS = 1e-6
MLA_HEADS = 8
Q_RANK = 512
KV_RANK = 256
NOPE_DIM = 128
ROPE_DIM = 64
V_DIM = 128
MLA_WIDTH = MLA_HEADS * V_DIM
SSM_HEADS = 16
SSM_HEADDIM = 64
SSM_INNER = SSM_HEADS * SSM_HEADDIM
SSM_GROUPS = 2
SSM_STATE = 128
SSM_CONV = 5
SSM_CHUNK = 128
SSM_XBC = SSM_INNER + 2 * SSM_GROUPS * SSM_STATE
GM_WIDTH = 1024
GM_GROUPS = 4
GM_CHUNK = 128
N_BRANCH = 3
N_EXPERTS = 16
EXPERT_FF = 1024
EC_CAPACITY = 2

N_CTX = BATCH * SEQ
N_LAT = DEC_BATCH * DEC_SEQ
N_TOK = N_CTX + N_LAT
CAP_CTX = EC_CAPACITY * SEQ // N_EXPERTS
CAP_LAT = EC_CAPACITY * DEC_SEQ // N_EXPERTS
ROWS_CTX = BATCH * CAP_CTX
ROWS_LAT = DEC_BATCH * CAP_LAT

LANES = 128
QK_HEAD = 2 * LANES

ZB_Z = 0
ZB_CQ = SSM_INNER
ZB_XBC = ZB_CQ + Q_RANK
ZB_GM = ZB_XBC + SSM_XBC
ZB_GATE = ZB_GM + 2 * GM_WIDTH
ZB_WIDTH = ZB_GATE + N_BRANCH * D_MODEL
ZK_WIDTH = LANES

V7X_VMEM_LIMIT_BYTES = 56 * 1024 * 1024

BF16 = jnp.bfloat16
F32 = jnp.float32


def _cparams(*sem):
    return pltpu.CompilerParams(dimension_semantics=sem, vmem_limit_bytes=V7X_VMEM_LIMIT_BYTES)


def _rms(x, g):
    return x * lax.rsqrt(jnp.mean(x * x, axis=-1, keepdims=True) + EPS) * g


def _swap16(x):
    lane = lax.broadcasted_iota(jnp.int32, x.shape, 1)
    return jnp.where((lane % 32) < 16, pltpu.roll(x, LANES - 16, 1), pltpu.roll(x, 16, 1))


def _mm_body(x_ref, w_ref, o_ref):
    o_ref[...] = jnp.dot(x_ref[...].astype(BF16), w_ref[...].astype(BF16),
                         preferred_element_type=F32).astype(o_ref.dtype)


def _matmul(x, w, out_dtype=F32, tm=512, tn=512, name="matmul"):
    m, k = x.shape
    _, n = w.shape
    tn = min(tn, n)
    return pl.pallas_call(
        _mm_body,
        grid=(m // tm, n // tn),
        in_specs=[pl.BlockSpec((tm, k), lambda i, j: (i, 0)),
                  pl.BlockSpec((k, tn), lambda i, j: (0, j))],
        out_specs=pl.BlockSpec((tm, tn), lambda i, j: (i, j)),
        out_shape=jax.ShapeDtypeStruct((m, n), out_dtype),
        compiler_params=_cparams("parallel", "arbitrary"),
        name=name,
    )(x, w)


IN_TM = 1024
IN_TN = 512
NORM_ROWS = 256


def _ctx_lat_specs(tm, width, n_grid_axes):
    n_ctx = N_CTX // tm
    if n_grid_axes == 1:
        return [pl.BlockSpec((tm, width), lambda i: (jnp.minimum(i, n_ctx - 1), 0)),
                pl.BlockSpec((tm, width), lambda i: (jnp.maximum(i - n_ctx, 0), 0))]
    return [pl.BlockSpec((tm, width), lambda i, j: (jnp.minimum(i, n_ctx - 1), 0)),
            pl.BlockSpec((tm, width), lambda i, j: (jnp.maximum(i - n_ctx, 0), 0))]


def _ctx_or_lat(tm, c_ref, l_ref, rows=slice(None)):
    return jnp.where(pl.program_id(0) < N_CTX // tm, c_ref[rows, :], l_ref[rows, :])


def _norm_mod_to(h_scr, xc_ref, xl_ref, g_ref, sc_ref, sh_ref):
    g = g_ref[...]
    mul = 1.0 + sc_ref[0]
    add = sh_ref[0]
    for r in range(0, IN_TM, NORM_ROWS):
        x = _ctx_or_lat(IN_TM, xc_ref, xl_ref, slice(r, r + NORM_ROWS))
        h_scr[r:r + NORM_ROWS, :] = (_rms(x, g) * mul + add).astype(BF16)


def _inproj_body(xc_ref, xl_ref, g_ref, sc_ref, sh_ref, w_ref, o_ref, h_scr):
    @pl.when(pl.program_id(1) == 0)
    def _():
        _norm_mod_to(h_scr, xc_ref, xl_ref, g_ref, sc_ref, sh_ref)

    o_ref[...] = jnp.dot(h_scr[...], w_ref[...], preferred_element_type=F32).astype(o_ref.dtype)


def _inproj_small_body(xc_ref, xl_ref, g_ref, sc_ref, sh_ref, w_ref, ckv_ref, zk_ref, h_scr):
    _norm_mod_to(h_scr, xc_ref, xl_ref, g_ref, sc_ref, sh_ref)
    acc = jnp.dot(h_scr[...], w_ref[...], preferred_element_type=F32)
    ckv_ref[...] = acc[:, :KV_RANK]
    zk_ref[...] = acc[:, KV_RANK:]


def _inproj(x_c, x_l, g, sc_t, sh_t, wb, ws):
    nt = N_TOK // IN_TM
    common = _ctx_lat_specs(IN_TM, D_MODEL, 2) + [
              pl.BlockSpec((1, D_MODEL), lambda i, j: (0, 0)),
              pl.BlockSpec((1, 1, D_MODEL), lambda i, j: (i, 0, 0)),
              pl.BlockSpec((1, 1, D_MODEL), lambda i, j: (i, 0, 0))]
    zb = pl.pallas_call(
        _inproj_body,
        grid=(nt, ZB_WIDTH // IN_TN),
        in_specs=common + [pl.BlockSpec((D_MODEL, IN_TN), lambda i, j: (0, j))],
        out_specs=pl.BlockSpec((IN_TM, IN_TN), lambda i, j: (i, j)),
        out_shape=jax.ShapeDtypeStruct((N_TOK, ZB_WIDTH), BF16),
        scratch_shapes=[pltpu.VMEM((IN_TM, D_MODEL), BF16)],
        compiler_params=_cparams("parallel", "arbitrary"),
        name="in_proj",
    )(x_c, x_l, g, sc_t, sh_t, wb)
    nsmall = KV_RANK + ZK_WIDTH
    ckv_raw, zk = pl.pallas_call(
        _inproj_small_body,
        grid=(nt, 1),
        in_specs=common + [pl.BlockSpec((D_MODEL, nsmall), lambda i, j: (0, 0))],
        out_specs=[pl.BlockSpec((IN_TM, KV_RANK), lambda i, j: (i, 0)),
                   pl.BlockSpec((IN_TM, ZK_WIDTH), lambda i, j: (i, 0))],
        out_shape=[jax.ShapeDtypeStruct((N_TOK, KV_RANK), F32),
                   jax.ShapeDtypeStruct((N_TOK, ZK_WIDTH), F32)],
        scratch_shapes=[pltpu.VMEM((IN_TM, D_MODEL), BF16)],
        compiler_params=_cparams("parallel", "arbitrary"),
        name="in_proj_small",
    )(x_c, x_l, g, sc_t, sh_t, ws)
    return zb, ckv_raw, zk


QKV_TM = 512
ATTN_SCALE = 1.0 / math.sqrt(NOPE_DIM + ROPE_DIM)


def _qproj_body(cq_ref, g_ref, w_ref, c_ref, s_ref, o_ref):
    qn = _rms(cq_ref[...].astype(F32), g_ref[...]).astype(BF16)
    q = jnp.dot(qn, w_ref[...], preferred_element_type=F32)
    c = c_ref[...]
    s = s_ref[...]
    for h in range(MLA_HEADS):
        lo = h * QK_HEAD
        r = q[:, lo + LANES:lo + QK_HEAD]
        o_ref[:, lo:lo + LANES] = (q[:, lo:lo + LANES] * ATTN_SCALE).astype(BF16)
        o_ref[:, lo + LANES:lo + QK_HEAD] = ((r * c + _swap16(r) * s) * ATTN_SCALE).astype(BF16)


def _qproj(zb, g_qn, w_uq_p, rope_c, rope_s):
    return pl.pallas_call(
        _qproj_body,
        grid=(N_TOK // QKV_TM,),
        in_specs=[pl.BlockSpec((QKV_TM, Q_RANK), lambda i: (i, ZB_CQ // Q_RANK)),
                  pl.BlockSpec((1, Q_RANK), lambda i: (0, 0)),
                  pl.BlockSpec((Q_RANK, MLA_HEADS * QK_HEAD), lambda i: (0, 0)),
                  pl.BlockSpec((QKV_TM, LANES), lambda i: (i, 0)),
                  pl.BlockSpec((QKV_TM, LANES), lambda i: (i, 0))],
        out_specs=pl.BlockSpec((QKV_TM, MLA_HEADS * QK_HEAD), lambda i: (i, 0)),
        out_shape=jax.ShapeDtypeStruct((N_TOK, MLA_HEADS * QK_HEAD), BF16),
        compiler_params=_cparams("parallel"),
        name="mla_q",
    )(zb, g_qn, w_uq_p, rope_c, rope_s)


def _kvproj_body(ckv_ref, zk_ref, g_ref, w_ref, c_ref, s_ref, ckv_o, kv_o, kr_o):
    ckv = _rms(ckv_ref[...], g_ref[...])
    ckv_o[...] = ckv
    kv_o[...] = jnp.dot(ckv.astype(BF16), w_ref[...], preferred_element_type=F32).astype(BF16)
    zk = zk_ref[...]
    is_rope = lax.broadcasted_iota(jnp.int32, zk.shape, 1) < ROPE_DIM
    kr = jnp.where(is_rope, zk, 0.0)
    rot = kr * c_ref[...] + _swap16(kr) * s_ref[...]
    kr_o[...] = jnp.where(is_rope, rot, 0.0).astype(BF16)


def _kvproj(ckv_raw, zk, g_kvn, w_ukv_p, rope_c, rope_s):
    return pl.pallas_call(
        _kvproj_body,
        grid=(N_TOK // QKV_TM,),
        in_specs=[pl.BlockSpec((QKV_TM, KV_RANK), lambda i: (i, 0)),
                  pl.BlockSpec((QKV_TM, ZK_WIDTH), lambda i: (i, 0)),
                  pl.BlockSpec((1, KV_RANK), lambda i: (0, 0)),
                  pl.BlockSpec((KV_RANK, 2 * MLA_WIDTH), lambda i: (0, 0)),
                  pl.BlockSpec((QKV_TM, LANES), lambda i: (i, 0)),
                  pl.BlockSpec((QKV_TM, LANES), lambda i: (i, 0))],
        out_specs=[pl.BlockSpec((QKV_TM, KV_RANK), lambda i: (i, 0)),
                   pl.BlockSpec((QKV_TM, 2 * MLA_WIDTH), lambda i: (i, 0)),
                   pl.BlockSpec((QKV_TM, LANES), lambda i: (i, 0))],
        out_shape=[jax.ShapeDtypeStruct((N_TOK, KV_RANK), F32),
                   jax.ShapeDtypeStruct((N_TOK, 2 * MLA_WIDTH), BF16),
                   jax.ShapeDtypeStruct((N_TOK, LANES), BF16)],
        compiler_params=_cparams("parallel"),
        name="mla_kv",
    )(ckv_raw, zk, g_kvn, w_ukv_p, rope_c, rope_s)


def _attn_body(n_parts, q_ref, *refs):
    o_ref = refs[3 * n_parts]
    for h in range(MLA_HEADS):
        qh = q_ref[:, h * QK_HEAD:(h + 1) * QK_HEAD]
        scores = []
        for p in range(n_parts):
            kn_ref, kr_ref = refs[3 * p], refs[3 * p + 1]
            kh = jnp.concatenate([kn_ref[:, h * LANES:(h + 1) * LANES], kr_ref[...]], axis=1)
            scores.append(lax.dot_general(qh, kh, (((1,), (1,)), ((), ())),
                                          preferred_element_type=F32))
        m = scores[0].max(axis=1, keepdims=True)
        for s in scores[1:]:
            m = jnp.maximum(m, s.max(axis=1, keepdims=True))
        den = 0.0
        acc = 0.0
        for p in range(n_parts):
            e = jnp.exp(scores[p] - m)
            den = den + e.sum(axis=1, keepdims=True)
            v_ref = refs[3 * p + 2]
            acc = acc + jnp.dot(e.astype(BF16), v_ref[:, h * LANES:(h + 1) * LANES],
                                preferred_element_type=F32)
        o_ref[:, h * LANES:(h + 1) * LANES] = (acc / den).astype(BF16)


def _attention(q, kv, kr, n_seq, seq_len, tq, row0, cache=None):
    nq = seq_len // tq
    q0 = row0 // tq
    s0 = row0 // seq_len
    in_specs = [pl.BlockSpec((tq, MLA_HEADS * QK_HEAD), lambda b, i: (q0 + b * nq + i, 0))]
    args = [q]
    if cache is not None:
        kv_c, kr_c, len_c = cache
        in_specs += [pl.BlockSpec((len_c, MLA_WIDTH), lambda b, i: (b, 0)),
                     pl.BlockSpec((len_c, LANES), lambda b, i: (b, 0)),
                     pl.BlockSpec((len_c, MLA_WIDTH), lambda b, i: (b, 1))]
        args += [kv_c, kr_c, kv_c]
    in_specs += [pl.BlockSpec((seq_len, MLA_WIDTH), lambda b, i: (s0 + b, 0)),
                 pl.BlockSpec((seq_len, LANES), lambda b, i: (s0 + b, 0)),
                 pl.BlockSpec((seq_len, MLA_WIDTH), lambda b, i: (s0 + b, 1))]
    args += [kv, kr, kv]
    n_parts = 1 if cache is None else 2
    return pl.pallas_call(
        functools.partial(_attn_body, n_parts),
        grid=(n_seq, nq),
        in_specs=in_specs,
        out_specs=pl.BlockSpec((tq, MLA_WIDTH), lambda b, i: (b * nq + i, 0)),
        out_shape=jax.ShapeDtypeStruct((n_seq * seq_len, MLA_WIDTH), BF16),
        compiler_params=_cparams("parallel", "arbitrary"),
        name="mla_attn_cache" if cache is not None else "mla_attn",
    )(*args)


MG_TM = 1024
MG_TN = 512
OUT_TM = 256


def _merge_body(ac_ref, al_ref, sc_ref, sl_ref, c_ref, wa_ref, ws_ref, wc_ref, ga_ref, gs_ref, gc_ref,
                o_ref):
    def branch(x, w_ref, gate_ref):
        y = jnp.dot(x, w_ref[0].astype(BF16), preferred_element_type=F32)
        return jax.nn.sigmoid(gate_ref[...].astype(F32)) * y

    o_ref[...] = (branch(_ctx_or_lat(MG_TM, ac_ref, al_ref), wa_ref, ga_ref)
                  + branch(_ctx_or_lat(MG_TM, sc_ref, sl_ref), ws_ref, gs_ref)
                  + branch(c_ref[...], wc_ref, gc_ref)).astype(BF16)


def _merge(attn_c, attn_l, ssm_c, ssm_l, gmo, wa, ws, wc, zb, layer):
    g0 = ZB_GATE // MG_TN
    gstep = D_MODEL // MG_TN
    pair = _ctx_lat_specs(MG_TM, MLA_WIDTH, 2)
    wspec = pl.BlockSpec((1, MLA_WIDTH, MG_TN), lambda i, j: (layer, 0, j))
    return pl.pallas_call(
        _merge_body,
        grid=(N_TOK // MG_TM, gstep),
        in_specs=pair + pair + [pl.BlockSpec((MG_TM, GM_WIDTH), lambda i, j: (i, 0)),
                                wspec, wspec, wspec,
                                pl.BlockSpec((MG_TM, MG_TN), lambda i, j: (i, g0 + j)),
                                pl.BlockSpec((MG_TM, MG_TN), lambda i, j: (i, g0 + gstep + j)),
                                pl.BlockSpec((MG_TM, MG_TN), lambda i, j: (i, g0 + 2 * gstep + j))],
        out_specs=pl.BlockSpec((MG_TM, MG_TN), lambda i, j: (i, j)),
        out_shape=jax.ShapeDtypeStruct((N_TOK, D_MODEL), BF16),
        compiler_params=_cparams("parallel", "arbitrary"),
        name="branch_merge",
    )(attn_c, attn_l, ssm_c, ssm_l, gmo, wa, ws, wc, zb, zb, zb)


def _split_bf16(x):
    hi = x.astype(BF16)
    return hi, (x - hi.astype(F32)).astype(BF16)


def _outproj_body(m_ref, w_ref, xc_ref, xl_ref, gt_ref, g_ref, sc_ref, sh_ref, wr_ref,
                  x1_ref, h2_ref, lg_ref):
    mix = jnp.dot(m_ref[...], w_ref[...], preferred_element_type=F32)
    x1 = _ctx_or_lat(OUT_TM, xc_ref, xl_ref) + gt_ref[0] * mix
    x1_ref[...] = x1
    h2 = _rms(x1, g_ref[...]) * (1.0 + sc_ref[0]) + sh_ref[0]
    h2_ref[...] = h2.astype(BF16)
    h_hi, h_lo = _split_bf16(h2)
    w_hi, w_lo = _split_bf16(wr_ref[...])
    lg_ref[...] = (jnp.dot(h_hi, w_hi, preferred_element_type=F32)
                   + jnp.dot(h_lo, w_hi, preferred_element_type=F32)
                   + jnp.dot(h_hi, w_lo, preferred_element_type=F32))


def _outproj(merged, w_out, x_c, x_l, gt_t, g2, sc_t, sh_t, w_router_p):
    mspec = pl.BlockSpec((1, 1, D_MODEL), lambda i: (i, 0, 0))
    return pl.pallas_call(
        _outproj_body,
        grid=(N_TOK // OUT_TM,),
        in_specs=[pl.BlockSpec((OUT_TM, D_MODEL), lambda i: (i, 0)),
                  pl.BlockSpec((D_MODEL, D_MODEL), lambda i: (0, 0))]
                 + _ctx_lat_specs(OUT_TM, D_MODEL, 1) + [
                  mspec,
                  pl.BlockSpec((1, D_MODEL), lambda i: (0, 0)),
                  mspec, mspec,
                  pl.BlockSpec((D_MODEL, LANES), lambda i: (0, 0))],
        out_specs=[pl.BlockSpec((OUT_TM, D_MODEL), lambda i: (i, 0)),
                   pl.BlockSpec((OUT_TM, D_MODEL), lambda i: (i, 0)),
                   pl.BlockSpec((OUT_TM, LANES), lambda i: (i, 0))],
        out_shape=[jax.ShapeDtypeStruct((N_TOK, D_MODEL), F32),
                   jax.ShapeDtypeStruct((N_TOK, D_MODEL), BF16),
                   jax.ShapeDtypeStruct((N_TOK, LANES), F32)],
        compiler_params=_cparams("parallel"),
        name="out_proj",
    )(merged, w_out, x_c, x_l, gt_t, g2, sc_t, sh_t, w_router_p)


PREFIX_BLK = 256
EXP_SEARCH_STEPS = 7
BISECT_STEPS = 40


def _prefix_count(mask):
    n = mask.shape[1]
    upper = (lax.broadcasted_iota(jnp.int32, (PREFIX_BLK, PREFIX_BLK), 0)
             < lax.broadcasted_iota(jnp.int32, (PREFIX_BLK, PREFIX_BLK), 1)).astype(BF16)
    run = jnp.zeros((mask.shape[0], 1), F32)
    outs = []
    for k in range(0, n, PREFIX_BLK):
        blk = mask[:, k:k + PREFIX_BLK]
        outs.append(jnp.dot(blk.astype(BF16), upper, preferred_element_type=F32) + run)
        run = run + jnp.sum(blk, axis=1, keepdims=True)
    return outs[0] if len(outs) == 1 else jnp.concatenate(outs, axis=1)


def _select_body(n_seq, seq_len, cap, lg_ref, slot_t_ref, slot_ref, aff_ref):
    lane = lax.broadcasted_iota(jnp.int32, (seq_len, LANES), 1)
    rows = []
    for s in range(n_seq):
        lg = jnp.where(lane < N_EXPERTS, lg_ref[s * seq_len:(s + 1) * seq_len, :], -jnp.inf)
        e = jnp.exp(lg - lg.max(axis=1, keepdims=True))
        aff = e / e.sum(axis=1, keepdims=True)
        rows.append(aff.T[:N_EXPERTS, :])
    a = rows[0] if n_seq == 1 else jnp.concatenate(rows, axis=0)

    def count_ge(t):
        return jnp.sum(jnp.where(a >= t, 1.0, 0.0), axis=1, keepdims=True)

    hi = jnp.full((a.shape[0], 1), 2.0, F32)
    for i in reversed(range(EXP_SEARCH_STEPS)):
        cand = hi * (2.0 ** -(2 ** i))
        hi = jnp.where(count_ge(cand) < cap, cand, hi)
    lo = jnp.where(hi <= 2.0 ** -126, 0.0, 0.5 * hi)
    for _ in range(BISECT_STEPS):
        mid = 0.5 * (lo + hi)
        ge = count_ge(mid) >= cap
        lo = jnp.where(ge, mid, lo)
        hi = jnp.where(ge, hi, mid)
    above = jnp.where(a >= hi, 1.0, 0.0)
    tied = jnp.where((a >= lo) & (a < hi), 1.0, 0.0)
    need = cap - jnp.sum(above, axis=1, keepdims=True)
    sel = above + tied * jnp.where(_prefix_count(tied) < need, 1.0, 0.0)
    slot = jnp.where(sel > 0.0, _prefix_count(sel), -1.0)
    slot_ref[...] = slot
    aff_ref[...] = a
    pad = jnp.full((LANES - N_EXPERTS, seq_len), -1.0, F32)
    for s in range(n_seq):
        blk = jnp.concatenate([slot[s * N_EXPERTS:(s + 1) * N_EXPERTS, :], pad], axis=0)
        slot_t_ref[s * seq_len:(s + 1) * seq_len, :] = blk.T.astype(jnp.int32)


def _select(logits, group, n_seq, seq_len, cap):
    rows = n_seq * N_EXPERTS
    n_rows = n_seq * seq_len
    return pl.pallas_call(
        functools.partial(_select_body, n_seq, seq_len, cap),
        grid=(1,),
        in_specs=[pl.BlockSpec((n_rows, LANES), lambda i: (group, 0))],
        out_specs=[pl.BlockSpec((n_rows, LANES), lambda i: (0, 0)),
                   pl.BlockSpec((rows, seq_len), lambda i: (0, 0)),
                   pl.BlockSpec((rows, seq_len), lambda i: (0, 0))],
        out_shape=[jax.ShapeDtypeStruct((n_rows, LANES), jnp.int32),
                   jax.ShapeDtypeStruct((rows, seq_len), F32),
                   jax.ShapeDtypeStruct((rows, seq_len), F32)],
        compiler_params=_cparams("arbitrary"),
        name="moe_select",
    )(logits)


def _gather_body(cap, epb, slot_ref, aff_ref, h_ref, xe_ref, ge_ref):
    n = h_ref.shape[0]
    step = pl.program_id(1)
    row = lax.broadcasted_iota(jnp.int32, (cap, n), 0).astype(F32)
    for k in range(epb):
        x = step * epb + k
        onehot = slot_ref[pl.ds(x, 1), :] == row
        xe_ref[k] = jnp.dot(onehot.astype(BF16), h_ref[...], preferred_element_type=F32).astype(BF16)
        gate = jnp.sum(jnp.where(onehot, aff_ref[pl.ds(x, 1), :], 0.0), axis=1, keepdims=True)
        ge_ref[k] = jnp.broadcast_to(gate, (cap, LANES))


def _gather(slot, aff, h2, n_seq, seq_len, cap, row0, epb):
    s0 = row0 // seq_len
    return pl.pallas_call(
        functools.partial(_gather_body, cap, epb),
        grid=(n_seq, N_EXPERTS // epb),
        in_specs=[pl.BlockSpec((N_EXPERTS, seq_len), lambda b, x: (b, 0)),
                  pl.BlockSpec((N_EXPERTS, seq_len), lambda b, x: (b, 0)),
                  pl.BlockSpec((seq_len, D_MODEL), lambda b, x: (s0 + b, 0))],
        out_specs=[pl.BlockSpec((epb, cap, D_MODEL), lambda b, x: (x, b, 0)),
                   pl.BlockSpec((epb, cap, LANES), lambda b, x: (x, b, 0))],
        out_shape=[jax.ShapeDtypeStruct((N_EXPERTS, n_seq * cap, D_MODEL), BF16),
                   jax.ShapeDtypeStruct((N_EXPERTS, n_seq * cap, LANES), F32)],
        compiler_params=_cparams("parallel", "arbitrary"),
        name="moe_gather",
    )(slot, aff, h2)


FFN_TF = 256


def _ffn_body(xc_ref, xl_ref, gc_ref, gl_ref, wg_ref, wu_ref, wd_ref, yc_ref, yl_ref, acc):
    f = pl.program_id(1)
    x = jnp.concatenate([xc_ref[0], xl_ref[0]], axis=0)
    gate = jnp.dot(x, wg_ref[0, 0].astype(BF16), preferred_element_type=F32)
    up = jnp.dot(x, wu_ref[0, 0].astype(BF16), preferred_element_type=F32)
    hid = (gate * jax.nn.sigmoid(gate) * up).astype(BF16)
    part = jnp.dot(hid, wd_ref[0, 0].astype(BF16), preferred_element_type=F32)

    @pl.when(f == 0)
    def _():
        acc[...] = part

    @pl.when(f > 0)
    def _():
        acc[...] += part

    @pl.when(f == pl.num_programs(1) - 1)
    def _():
        yc_ref[0] = (acc[:ROWS_CTX, :] * gc_ref[0][:, :1]).astype(BF16)
        yl_ref[0] = (acc[ROWS_CTX:, :] * gl_ref[0][:, :1]).astype(BF16)


def _expert_ffn(xe_c, xe_l, ge_c, ge_l, w_gate, w_up, w_down, layer):
    def xspec(rows, width):
        return pl.BlockSpec((1, rows, width), lambda i, f: (i, 0, 0))

    return pl.pallas_call(
        _ffn_body,
        grid=(N_EXPERTS, EXPERT_FF // FFN_TF),
        in_specs=[xspec(ROWS_CTX, D_MODEL), xspec(ROWS_LAT, D_MODEL),
                  xspec(ROWS_CTX, LANES), xspec(ROWS_LAT, LANES),
                  pl.BlockSpec((1, 1, D_MODEL, FFN_TF), lambda i, f: (layer, i, 0, f)),
                  pl.BlockSpec((1, 1, D_MODEL, FFN_TF), lambda i, f: (layer, i, 0, f)),
                  pl.BlockSpec((1, 1, FFN_TF, D_MODEL), lambda i, f: (layer, i, f, 0))],
        out_specs=[xspec(ROWS_CTX, D_MODEL), xspec(ROWS_LAT, D_MODEL)],
        out_shape=[jax.ShapeDtypeStruct((N_EXPERTS, ROWS_CTX, D_MODEL), BF16),
                   jax.ShapeDtypeStruct((N_EXPERTS, ROWS_LAT, D_MODEL), BF16)],
        scratch_shapes=[pltpu.VMEM((ROWS_CTX + ROWS_LAT, D_MODEL), F32)],
        compiler_params=_cparams("parallel", "arbitrary"),
        name="expert_ffn",
    )(xe_c, xe_l, ge_c, ge_l, w_gate, w_up, w_down)


CMB_TN = 512


def _combine_body(cap, slot_ref, y_ref, x_ref, gt_ref, o_ref, acc):
    n = slot_ref.shape[0]
    slot = slot_ref[...]
    col = lax.broadcasted_iota(jnp.int32, (n, cap), 1)
    for x in range(N_EXPERTS):
        onehot = (slot[:, x:x + 1] == col).astype(BF16)
        part = jnp.dot(onehot, y_ref[x], preferred_element_type=F32)
        if x == 0:
            acc[...] = part
        else:
            acc[...] += part
    o_ref[...] = x_ref[...] + gt_ref[0] * acc[...]


def _combine(slot_t, y, x1, gt_t, n_seq, seq_len, cap, row0):
    s0 = row0 // seq_len
    return pl.pallas_call(
        functools.partial(_combine_body, cap),
        grid=(n_seq, D_MODEL // CMB_TN),
        in_specs=[pl.BlockSpec((seq_len, LANES), lambda b, j: (b, 0)),
                  pl.BlockSpec((N_EXPERTS, cap, CMB_TN), lambda b, j: (0, b, j)),
                  pl.BlockSpec((seq_len, CMB_TN), lambda b, j: (s0 + b, j)),
                  pl.BlockSpec((1, 1, CMB_TN), lambda b, j: (b, 0, j))],
        out_specs=pl.BlockSpec((seq_len, CMB_TN), lambda b, j: (b, j)),
        out_shape=jax.ShapeDtypeStruct((n_seq * seq_len, D_MODEL), F32),
        scratch_shapes=[pltpu.VMEM((seq_len, CMB_TN), F32)],
        compiler_params=_cparams("parallel", "arbitrary"),
        name="moe_combine",
    )(slot_t, y, x1, gt_t)


CONV_TN = 256
CONV_HALO = 8
DT_LANE = ROPE_DIM


def _conv_body(x_ref, w_ref, b_ref, o_ref):
    seq_len = x_ref.shape[0]
    halo = jnp.zeros((CONV_HALO, CONV_TN), F32)
    ext = jnp.concatenate([halo, x_ref[...].astype(F32), halo], axis=0)
    w = w_ref[...]
    y = b_ref[...]
    for k in range(SSM_CONV):
        lo = CONV_HALO - SSM_CONV // 2 + k
        y = y + w[k:k + 1, :] * ext[lo:lo + seq_len, :]
    o_ref[...] = (y * jax.nn.sigmoid(y)).astype(BF16)


def _conv_silu(zb, conv_w8, conv_b, n_seq, seq_len, row0):
    s0 = row0 // seq_len
    c0 = ZB_XBC // CONV_TN
    return pl.pallas_call(
        _conv_body,
        grid=(n_seq, SSM_XBC // CONV_TN),
        in_specs=[pl.BlockSpec((seq_len, CONV_TN), lambda b, j: (s0 + b, c0 + j)),
                  pl.BlockSpec((8, CONV_TN), lambda b, j: (0, j)),
                  pl.BlockSpec((1, CONV_TN), lambda b, j: (0, j))],
        out_specs=pl.BlockSpec((seq_len, CONV_TN), lambda b, j: (b, j)),
        out_shape=jax.ShapeDtypeStruct((n_seq * seq_len, SSM_XBC), BF16),
        compiler_params=_cparams("parallel", "arbitrary"),
        name="ssm_conv",
    )(zb, conv_w8, conv_b)


def _split3(x):
    hi = x.astype(BF16)
    r = x - hi.astype(F32)
    mid = r.astype(BF16)
    return hi, mid, (r - mid.astype(F32)).astype(BF16)


def _ssd_body(nc, xa_ref, zk_ref, z_ref, h0_ref, bias_ref, a_ref, d_ref, g_ref, ef_ref, eb_ref,
              o_ref, st_ref, hb_in, hf_cur, hb_cur):
    q = SSM_CHUNK
    half = SSM_INNER // SSM_GROUPS
    ii = lax.broadcasted_iota(jnp.int32, (q, q), 0)
    jj = lax.broadcasted_iota(jnp.int32, (q, q), 1)
    lower = ii >= jj
    upper = ii <= jj
    lower_b = lower.astype(BF16)
    upper_b = upper.astype(BF16)
    lane = lax.broadcasted_iota(jnp.int32, (q, LANES), 1)
    is_dt = (lane >= DT_LANE) & (lane < DT_LANE + 2 * SSM_HEADS)
    is_fwd = lane < DT_LANE + SSM_HEADS

    def tri_cumsum(tri, v):
        hi, mid, lo = _split3(v)
        return (jnp.dot(tri, hi, preferred_element_type=F32) + jnp.dot(tri, mid, preferred_element_type=F32)
                + jnp.dot(tri, lo, preferred_element_type=F32))

    def expand(v, e_ref):
        hi, lo = _split_bf16(v)
        return (jnp.dot(hi, e_ref[...], preferred_element_type=F32)
                + jnp.dot(lo, e_ref[...], preferred_element_type=F32))

    def chunk_factors(r0):
        dt = jnp.where(is_dt, jax.nn.softplus(zk_ref[pl.ds(r0, q), :] + bias_ref[...]), 0.0)
        dta = dt * a_ref[...]
        cum = jnp.where(is_fwd, tri_cumsum(lower_b, dta), tri_cumsum(upper_b, dta))
        tot = jnp.sum(dta, axis=0, keepdims=True)
        return dt, cum, tot

    def state_update(r0, dend_x, cdec_x, h_prev):
        x = xa_ref[pl.ds(r0, q), 0:SSM_INNER].astype(F32)
        xs = (x * dend_x).astype(BF16)
        parts = []
        for g in range(SSM_GROUPS):
            lo = SSM_INNER + g * SSM_STATE
            b_t = xa_ref[pl.ds(r0, q), lo:lo + SSM_STATE].astype(F32).T.astype(BF16)
            parts.append(jnp.dot(b_t, xs[:, g * half:(g + 1) * half], preferred_element_type=F32))
        return cdec_x * h_prev + jnp.concatenate(parts, axis=1)

    hf_cur[...] = h0_ref[0, 0]
    hb_cur[...] = h0_ref[0, 1]

    def bwd_step(t, carry):
        c = nc - 1 - t
        r0 = pl.multiple_of(c * q, q)
        dt, cum, tot = chunk_factors(r0)
        hb_in[c] = hb_cur[...]
        dend_x = expand(jnp.exp(tot - cum) * dt, eb_ref)
        cdec_x = expand(jnp.broadcast_to(jnp.exp(tot), (8, LANES)), eb_ref)[0:1]
        hb_cur[...] = state_update(r0, dend_x, cdec_x, hb_cur[...])
        return carry

    lax.fori_loop(0, nc, bwd_step, 0)

    def fwd_step(c, carry):
        r0 = pl.multiple_of(c * q, q)
        dt, cum, tot = chunk_factors(r0)
        cum_t = cum.T
        dt_t = dt.T
        eoff = jnp.exp(cum)
        x_bf = xa_ref[pl.ds(r0, q), 0:SSM_INNER]
        cb = []
        c_bf = []
        for g in range(SSM_GROUPS):
            lo_b = SSM_INNER + g * SSM_STATE
            lo_c = SSM_INNER + (SSM_GROUPS + g) * SSM_STATE
            c_g = xa_ref[pl.ds(r0, q), lo_c:lo_c + SSM_STATE]
            b_g = xa_ref[pl.ds(r0, q), lo_b:lo_b + SSM_STATE]
            c_bf.append(c_g)
            cb.append(lax.dot_general(c_g, b_g, (((1,), (1,)), ((), ())), preferred_element_type=F32))

        def head_matrix(h):
            f = DT_LANE + h
            b = DT_LANE + SSM_HEADS + h
            lf = jnp.where(lower, jnp.exp(cum[:, f:f + 1] - cum_t[f:f + 1, :]), 0.0) * dt_t[f:f + 1, :]
            lb = jnp.where(upper, jnp.exp(cum[:, b:b + 1] - cum_t[b:b + 1, :]), 0.0) * dt_t[b:b + 1, :]
            return (cb[h // (SSM_HEADS // SSM_GROUPS)] * (lf + lb)).astype(BF16)

        lane_lo = lane < SSM_HEADDIM
        pairs = []
        for hp in range(SSM_HEADS // 2):
            x_pair = x_bf[:, hp * LANES:(hp + 1) * LANES]
            y0 = jnp.dot(head_matrix(2 * hp), x_pair, preferred_element_type=F32)
            y1 = jnp.dot(head_matrix(2 * hp + 1), x_pair, preferred_element_type=F32)
            pairs.append(jnp.where(lane_lo, y0, y1))
        y = jnp.concatenate(pairs, axis=1)

        def off_diag(h_t, factor_x):
            h_bf = h_t.astype(BF16)
            parts = [jnp.dot(c_bf[g], h_bf[:, g * half:(g + 1) * half], preferred_element_type=F32)
                     for g in range(SSM_GROUPS)]
            return jnp.concatenate(parts, axis=1) * factor_x

        y = y + off_diag(hf_cur[...], expand(eoff, ef_ref)) + off_diag(hb_in[c], expand(eoff, eb_ref))
        y = y + d_ref[...] * x_bf.astype(F32)
        zg = z_ref[pl.ds(r0, q), :].astype(F32)
        o_ref[pl.ds(r0, q), :] = _rms(y * (zg * jax.nn.sigmoid(zg)), g_ref[...]).astype(BF16)

        dend_x = expand(jnp.exp(tot - cum) * dt, ef_ref)
        cdec_x = expand(jnp.broadcast_to(jnp.exp(tot), (8, LANES)), ef_ref)[0:1]
        hf_cur[...] = state_update(r0, dend_x, cdec_x, hf_cur[...])
        return carry

    lax.fori_loop(0, nc, fwd_step, 0)
    st_ref[0, 0] = hf_cur[...]
    st_ref[0, 1] = hb_cur[...]


def _ssd(xa, zk, zb, h0_t, dt_bias_p, a_p, d_x, g_ssm, e_f, e_b, n_seq, seq_len, row0):
    s0 = row0 // seq_len
    nc = seq_len // SSM_CHUNK
    vec = lambda w: pl.BlockSpec((1, w), lambda b: (0, 0))
    return pl.pallas_call(
        functools.partial(_ssd_body, nc),
        grid=(n_seq,),
        in_specs=[pl.BlockSpec((seq_len, SSM_XBC), lambda b: (b, 0)),
                  pl.BlockSpec((seq_len, ZK_WIDTH), lambda b: (s0 + b, 0)),
                  pl.BlockSpec((seq_len, SSM_INNER), lambda b: (s0 + b, ZB_Z // SSM_INNER)),
                  pl.BlockSpec((1, 2, SSM_STATE, SSM_INNER), lambda b: (b, 0, 0, 0)),
                  vec(LANES), vec(LANES), vec(SSM_INNER), vec(SSM_INNER),
                  pl.BlockSpec((LANES, SSM_INNER), lambda b: (0, 0)),
                  pl.BlockSpec((LANES, SSM_INNER), lambda b: (0, 0))],
        out_specs=[pl.BlockSpec((seq_len, SSM_INNER), lambda b: (b, 0)),
                   pl.BlockSpec((1, 2, SSM_STATE, SSM_INNER), lambda b: (b, 0, 0, 0))],
        out_shape=[jax.ShapeDtypeStruct((n_seq * seq_len, SSM_INNER), BF16),
                   jax.ShapeDtypeStruct((n_seq, 2, SSM_STATE, SSM_INNER), F32)],
        scratch_shapes=[pltpu.VMEM((nc, SSM_STATE, SSM_INNER), F32),
                        pltpu.VMEM((SSM_STATE, SSM_INNER), F32),
                        pltpu.VMEM((SSM_STATE, SSM_INNER), F32)],
        compiler_params=_cparams("parallel"),
        name="ssd",
    )(xa, zk, zb, h0_t, dt_bias_p, a_p, d_x, g_ssm, e_f, e_b)


GM_TM = 512
GM_GROUP_W = GM_WIDTH // GM_GROUPS


def _gmlp_body(u_ref, v_ref, g_ref, w_ref, b_ref, o_ref):
    for r in range(0, GM_TM, GM_CHUNK):
        u = jax.nn.gelu(u_ref[r:r + GM_CHUNK, :].astype(F32))
        vg = _rms(jax.nn.gelu(v_ref[r:r + GM_CHUNK, :].astype(F32)), g_ref[...]).astype(BF16)
        sv = jnp.concatenate(
            [jnp.dot(w_ref[k], vg[:, k * GM_GROUP_W:(k + 1) * GM_GROUP_W], preferred_element_type=F32)
             for k in range(GM_GROUPS)], axis=1)
        o_ref[r:r + GM_CHUNK, :] = (u * (sv + b_ref[...])).astype(BF16)


def _gmlp(zb, g_gv, w_sp, b_x):
    return pl.pallas_call(
        _gmlp_body,
        grid=(N_TOK // GM_TM,),
        in_specs=[pl.BlockSpec((GM_TM, GM_WIDTH), lambda i: (i, ZB_GM // GM_WIDTH)),
                  pl.BlockSpec((GM_TM, GM_WIDTH), lambda i: (i, ZB_GM // GM_WIDTH + 1)),
                  pl.BlockSpec((1, GM_WIDTH), lambda i: (0, 0)),
                  pl.BlockSpec((GM_GROUPS, GM_CHUNK, GM_CHUNK), lambda i: (0, 0, 0)),
                  pl.BlockSpec((GM_CHUNK, GM_WIDTH), lambda i: (0, 0))],
        out_specs=pl.BlockSpec((GM_TM, GM_WIDTH), lambda i: (i, 0)),
        out_shape=jax.ShapeDtypeStruct((N_TOK, GM_WIDTH), BF16),
        compiler_params=_cparams("parallel"),
        name="gmlp",
    )(zb, zb, g_gv, w_sp, b_x)


FN_TM = 512


def _final_norm_body(x_ref, g_ref, o_ref):
    o_ref[...] = _rms(x_ref[...], g_ref[...])


def _final_norm(x, g):
    rows = x.shape[0]
    return pl.pallas_call(
        _final_norm_body,
        grid=(rows // FN_TM,),
        in_specs=[pl.BlockSpec((FN_TM, D_MODEL), lambda i: (i, 0)),
                  pl.BlockSpec((1, D_MODEL), lambda i: (0, 0))],
        out_specs=pl.BlockSpec((FN_TM, D_MODEL), lambda i: (i, 0)),
        out_shape=jax.ShapeDtypeStruct((rows, D_MODEL), F32),
        compiler_params=_cparams("parallel"),
        name="final_norm",
    )(x, g)


def _dt_lanes(v):
    return jnp.pad(v.reshape(1, 2 * SSM_HEADS).astype(F32),
                   ((0, 0), (DT_LANE, LANES - DT_LANE - 2 * SSM_HEADS)))


def _head_expanders():
    lane = lax.broadcasted_iota(jnp.int32, (LANES, SSM_INNER), 0)
    head = lax.broadcasted_iota(jnp.int32, (LANES, SSM_INNER), 1) // SSM_HEADDIM
    e_f = (lane == head + DT_LANE).astype(BF16)
    e_b = (lane == head + DT_LANE + SSM_HEADS).astype(BF16)
    return e_f, e_b


W_IN_SEGS = {}
_off = 0
for _name, _w in (("cq", Q_RANK), ("ckv", KV_RANK), ("kr", ROPE_DIM), ("z", SSM_INNER), ("xbc", SSM_XBC),
                  ("dt", 2 * SSM_HEADS), ("gm", 2 * GM_WIDTH), ("gates", N_BRANCH * D_MODEL)):
    W_IN_SEGS[_name] = (_off, _w)
    _off += _w
N_IN = _off
PREP_TK = 256


def _prep_w_in_body(w_ref, wb_ref, ws_ref):
    def seg(name):
        start, width = W_IN_SEGS[name]
        return w_ref[0, :, start:start + width].astype(BF16)

    o = 0
    for name in ("z", "cq", "xbc", "gm", "gates"):
        width = W_IN_SEGS[name][1]
        wb_ref[:, o:o + width] = seg(name)
        o += width
    o = 0
    for name in ("ckv", "kr", "dt"):
        width = W_IN_SEGS[name][1]
        ws_ref[:, o:o + width] = seg(name)
        o += width
    ws_ref[:, o:] = jnp.zeros((PREP_TK, KV_RANK + ZK_WIDTH - o), BF16)


def _prep_w_in(w, layer):
    nsmall = KV_RANK + ZK_WIDTH
    return pl.pallas_call(
        _prep_w_in_body,
        grid=(D_MODEL // PREP_TK,),
        in_specs=[pl.BlockSpec((1, PREP_TK, N_IN), lambda i: (layer, i, 0))],
        out_specs=[pl.BlockSpec((PREP_TK, ZB_WIDTH), lambda i: (i, 0)),
                   pl.BlockSpec((PREP_TK, nsmall), lambda i: (i, 0))],
        out_shape=[jax.ShapeDtypeStruct((D_MODEL, ZB_WIDTH), BF16),
                   jax.ShapeDtypeStruct((D_MODEL, nsmall), BF16)],
        compiler_params=_cparams("parallel"),
        name="w_in_prep",
    )(w)


MOD_TN = 1024
MOD_ROWS = 8
N_COND = 1 + DEC_BATCH


def _mod_body(ct_ref, w_ref, b_ref, o_ref):
    c = ct_ref[...]
    act = c * jax.nn.sigmoid(c)
    sub = lax.broadcasted_iota(jnp.int32, (MOD_ROWS, MOD_TN), 0)
    out = jnp.zeros((MOD_ROWS, MOD_TN), F32)
    for r in range(N_COND):
        y = jnp.sum(act[:, r:r + 1] * w_ref[0], axis=0, keepdims=True)
        out = jnp.where(sub == r, y, out)
    o_ref[0] = out + b_ref[0]


def _modulation(cond_t, w_mod, b_mod):
    n = 6 * D_MODEL
    return pl.pallas_call(
        _mod_body,
        grid=(DEPTH, n // MOD_TN),
        in_specs=[pl.BlockSpec((D_MODEL, MOD_ROWS), lambda l, j: (0, 0)),
                  pl.BlockSpec((1, D_MODEL, MOD_TN), lambda l, j: (l, 0, j)),
                  pl.BlockSpec((1, 1, MOD_TN), lambda l, j: (l, 0, j))],
        out_specs=pl.BlockSpec((1, MOD_ROWS, MOD_TN), lambda l, j: (l, 0, j)),
        out_shape=jax.ShapeDtypeStruct((DEPTH, MOD_ROWS, n), F32),
        compiler_params=_cparams("parallel", "arbitrary"),
        name="modulation",
    )(cond_t, w_mod, b_mod[:, None, :])


def _prep_w_uq(w):
    w = w.reshape(Q_RANK, MLA_HEADS, NOPE_DIM + ROPE_DIM)
    pad = jnp.zeros((Q_RANK, MLA_HEADS, QK_HEAD - NOPE_DIM - ROPE_DIM), w.dtype)
    return jnp.concatenate([w, pad], axis=2).reshape(Q_RANK, MLA_HEADS * QK_HEAD).astype(BF16)


def _prep_w_ukv(w):
    w = w.reshape(KV_RANK, MLA_HEADS, NOPE_DIM + V_DIM)
    return jnp.concatenate([w[:, :, :NOPE_DIM].reshape(KV_RANK, MLA_WIDTH),
                            w[:, :, NOPE_DIM:].reshape(KV_RANK, MLA_WIDTH)], axis=1).astype(BF16)


def _rope_tables(n_lat):
    rows = n_lat // GRID_W
    row = jnp.repeat(jnp.arange(rows), GRID_W).astype(F32)
    col = jnp.tile(jnp.arange(GRID_W), rows).astype(F32)
    nf = ROPE_DIM // 4
    freqs = jnp.power(ROPE_THETA, -jnp.arange(nf, dtype=F32) / nf)
    ang = jnp.stack([row[:, None] * freqs, col[:, None] * freqs], axis=1)
    cos, sin = jnp.cos(ang), jnp.sin(ang)
    c64 = jnp.concatenate([cos[:, 0], cos[:, 0], cos[:, 1], cos[:, 1]], axis=1)
    s64 = jnp.concatenate([-sin[:, 0], sin[:, 0], -sin[:, 1], sin[:, 1]], axis=1)
    c_lat = jnp.concatenate([c64, jnp.ones((n_lat, LANES - ROPE_DIM), F32)], axis=1)
    s_lat = jnp.concatenate([s64, jnp.zeros((n_lat, LANES - ROPE_DIM), F32)], axis=1)
    rope_c = jnp.concatenate([jnp.ones((N_CTX, LANES), F32), jnp.tile(c_lat, (DEC_BATCH, 1))], axis=0)
    rope_s = jnp.concatenate([jnp.zeros((N_CTX, LANES), F32), jnp.tile(s_lat, (DEC_BATCH, 1))], axis=0)
    return rope_c, rope_s


def _rows_mod(vec3, tm):
    idx = [0] * (N_CTX // tm) + [1] * (DEC_SEQ // tm) + [2] * (DEC_SEQ // tm)
    return vec3[jnp.array(idx)][:, None, :]


def kernel(x_prompt, x_sample, c, cache_ckv, cache_krope, state_ssm, c_ctx, w_mod, b_mod,
           g_norm1, g_norm2, w_in, g_qn, w_uq, g_kvn, w_ukv, conv_w, conv_b, dt_bias, a_log,
           d_skip, g_ssm, g_gv, w_sp, b_sp, w_br_attn, w_br_ssm, w_br_gmlp, w_out, w_router,
           w_gate, w_up, w_down, g_final):
    rope_c, rope_s = _rope_tables(DEC_SEQ)
    x_c = x_prompt.reshape(N_CTX, D_MODEL)
    x_l = x_sample.reshape(N_LAT, D_MODEL)
    cond_t = jnp.pad(jnp.concatenate([c_ctx[None, :], c], axis=0).T, ((0, 0), (0, MOD_ROWS - N_COND)))
    mod_all = _modulation(cond_t, w_mod, b_mod)
    e_f, e_b = _head_expanders()
    h0_c = jnp.zeros((BATCH, 2, SSM_STATE, SSM_INNER), F32)

    ckvs, krs, sts = [], [], []
    for l in range(DEPTH):
        mod = mod_all[l, :1 + DEC_BATCH].reshape(1 + DEC_BATCH, 6, D_MODEL)
        sh1, sc1, gt1, sh2, sc2, gt2 = [mod[:, k] for k in range(6)]

        wb, ws = _prep_w_in(w_in, l)
        zb, ckv_raw, zk = _inproj(x_c, x_l, g_norm1[l][None, :], _rows_mod(sc1, IN_TM), _rows_mod(sh1, IN_TM),
                                  wb, ws)

        q = _qproj(zb, g_qn[l][None, :], _prep_w_uq(w_uq[l]), rope_c, rope_s)
        w_ukv_p = _prep_w_ukv(w_ukv[l])
        ckv, kv, krot = _kvproj(ckv_raw, zk, g_kvn[l][None, :], w_ukv_p, rope_c, rope_s)
        kv_cache = _matmul(cache_ckv[:, l].reshape(DEC_BATCH * PAST_LEN, KV_RANK), w_ukv_p,
                           out_dtype=BF16, tn=2 * MLA_WIDTH, name="mla_kv_cache")
        kr_cache = jnp.pad(cache_krope[:, l].reshape(DEC_BATCH * PAST_LEN, ROPE_DIM),
                           ((0, 0), (0, LANES - ROPE_DIM))).astype(BF16)
        attn_c = _attention(q, kv, krot, BATCH, SEQ, SEQ, 0)
        attn_l = _attention(q, kv, krot, DEC_BATCH, DEC_SEQ, 256, N_CTX, cache=(kv_cache, kr_cache, PAST_LEN))

        conv_w8 = jnp.pad(conv_w[l], ((0, 8 - SSM_CONV), (0, 0)))
        ssd_par = (_dt_lanes(dt_bias[l]), _dt_lanes(-jnp.exp(a_log[l])),
                   jnp.repeat(d_skip[l], SSM_HEADDIM)[None, :], g_ssm[l][None, :], e_f, e_b)
        xa_c = _conv_silu(zb, conv_w8, conv_b[l][None, :], BATCH, SEQ, 0)
        xa_l = _conv_silu(zb, conv_w8, conv_b[l][None, :], DEC_BATCH, DEC_SEQ, N_CTX)
        h0_l = jnp.transpose(state_ssm[:, l], (0, 1, 4, 2, 3)).reshape(DEC_BATCH, 2, SSM_STATE, SSM_INNER)
        ssm_c, st_c = _ssd(xa_c, zk, zb, h0_c, *ssd_par, BATCH, SEQ, 0)
        ssm_l, _ = _ssd(xa_l, zk, zb, h0_l, *ssd_par, DEC_BATCH, DEC_SEQ, N_CTX)
        gmo = _gmlp(zb, g_gv[l][None, :], w_sp[l].astype(BF16), jnp.repeat(b_sp[l].T, GM_GROUP_W, axis=1))
        st_c = jnp.transpose(st_c.reshape(BATCH, 2, SSM_STATE, SSM_HEADS, SSM_HEADDIM), (0, 1, 3, 4, 2))

        merged = _merge(attn_c, attn_l, ssm_c, ssm_l, gmo, w_br_attn, w_br_ssm, w_br_gmlp, zb, l)
        w_router_p = jnp.pad(w_router[l], ((0, 0), (0, LANES - N_EXPERTS)))
        x1, h2, logits = _outproj(merged, w_out[l].astype(BF16), x_c, x_l, _rows_mod(gt1, OUT_TM),
                                  g_norm2[l][None, :], _rows_mod(sc2, OUT_TM), _rows_mod(sh2, OUT_TM),
                                  w_router_p)

        slot_c, srow_c, arow_c = _select(logits, 0, BATCH, SEQ, CAP_CTX)
        slot_l, srow_l, arow_l = _select(logits, 1, DEC_BATCH, DEC_SEQ, CAP_LAT)
        xe_c, ge_c = _gather(srow_c, arow_c, h2, BATCH, SEQ, CAP_CTX, 0, N_EXPERTS)
        xe_l, ge_l = _gather(srow_l, arow_l, h2, DEC_BATCH, DEC_SEQ, CAP_LAT, N_CTX, 2)
        y_c, y_l = _expert_ffn(xe_c, xe_l, ge_c, ge_l, w_gate, w_up, w_down, l)
        gt2_c = jnp.broadcast_to(gt2[0][None, None, :], (BATCH, 1, D_MODEL))
        gt2_l = gt2[1:][:, None, :]
        x_c = _combine(slot_c, y_c, x1, gt2_c, BATCH, SEQ, CAP_CTX, 0)
        x_l = _combine(slot_l, y_l, x1, gt2_l, DEC_BATCH, DEC_SEQ, CAP_LAT, N_CTX)

        ckvs.append(ckv[:N_CTX].reshape(BATCH, SEQ, KV_RANK))
        krs.append(zk[:N_CTX, :ROPE_DIM].reshape(BATCH, SEQ, ROPE_DIM))
        sts.append(st_c)

    y_prompt = _final_norm(x_c, g_final[None, :]).reshape(BATCH, SEQ, D_MODEL)
    y_sample = _final_norm(x_l, g_final[None, :]).reshape(DEC_BATCH, DEC_SEQ, D_MODEL)
    return (y_prompt, y_sample, jnp.stack(ckvs, axis=1), jnp.stack(krs, axis=1), jnp.stack(sts, axis=1))
```

```python
import functools
import math

import jax
import jax.numpy as jnp
from jax import lax
from jax.experimental import pallas as pl
from jax.experimental.pallas import tpu as pltpu

D_MODEL = 2048
BATCH = 16
SEQ = 256
DEPTH = 2
DEC_BATCH = 2
DEC_SEQ = 2048
PAST_LEN = 512
GRID_W = 64
ROPE_THETA = 10000.0
EPS = 1e-6
MLA_HEADS = 8
Q_RANK = 512
KV_RANK = 256
NOPE_DIM = 128
ROPE_DIM = 64
V_DIM = 128
MLA_WIDTH = MLA_HEADS * V_DIM
SSM_HEADS = 16
SSM_HEADDIM = 64
SSM_INNER = SSM_HEADS * SSM_HEADDIM
SSM_GROUPS = 2
SSM_STATE = 128
SSM_CONV = 5
SSM_CHUNK = 128
SSM_XBC = SSM_INNER + 2 * SSM_GROUPS * SSM_STATE
GM_WIDTH = 1024
GM_GROUPS = 4
GM_CHUNK = 128
N_BRANCH = 3
N_EXPERTS = 16
EXPERT_FF = 1024
EC_CAPACITY = 2

N_CTX = BATCH * SEQ
N_LAT = DEC_BATCH * DEC_SEQ
N_TOK = N_CTX + N_LAT
CAP_CTX = EC_CAPACITY * SEQ // N_EXPERTS
CAP_LAT = EC_CAPACITY * DEC_SEQ // N_EXPERTS
ROWS_CTX = BATCH * CAP_CTX
ROWS_LAT = DEC_BATCH * CAP_LAT

LANES = 128
QK_HEAD = 2 * LANES

ZB_Z = 0
ZB_CQ = SSM_INNER
ZB_XBC = ZB_CQ + Q_RANK
ZB_GM = ZB_XBC + SSM_XBC
ZB_GATE = ZB_GM + 2 * GM_WIDTH
ZB_WIDTH = ZB_GATE + N_BRANCH * D_MODEL
ZK_WIDTH = LANES

V7X_VMEM_LIMIT_BYTES = 56 * 1024 * 1024

BF16 = jnp.bfloat16
F32 = jnp.float32


def _cparams(*sem):
    return pltpu.CompilerParams(dimension_semantics=sem, vmem_limit_bytes=V7X_VMEM_LIMIT_BYTES)


def _rms(x, g):
    return x * lax.rsqrt(jnp.mean(x * x, axis=-1, keepdims=True) + EPS) * g


def _swap16(x):
    lane = lax.broadcasted_iota(jnp.int32, x.shape, 1)
    return jnp.where((lane % 32) < 16, pltpu.roll(x, LANES - 16, 1), pltpu.roll(x, 16, 1))


def _mm_body(x_ref, w_ref, o_ref):
    o_ref[...] = jnp.dot(x_ref[...].astype(BF16), w_ref[...].astype(BF16),
                         preferred_element_type=F32).astype(o_ref.dtype)


def _matmul(x, w, out_dtype=F32, tm=512, tn=512, name="matmul"):
    m, k = x.shape
    _, n = w.shape
    tn = min(tn, n)
    return pl.pallas_call(
        _mm_body,
        grid=(m // tm, n // tn),
        in_specs=[pl.BlockSpec((tm, k), lambda i, j: (i, 0)),
                  pl.BlockSpec((k, tn), lambda i, j: (0, j))],
        out_specs=pl.BlockSpec((tm, tn), lambda i, j: (i, j)),
        out_shape=jax.ShapeDtypeStruct((m, n), out_dtype),
        compiler_params=_cparams("parallel", "arbitrary"),
        name=name,
    )(x, w)


IN_TM = 1024
IN_TN = 512
NORM_ROWS = 256


def _ctx_lat_specs(tm, width, n_grid_axes):
    del n_grid_axes
    n_ctx = N_CTX // tm
    return [pl.BlockSpec((tm, width), lambda i, *_: (jnp.minimum(i, n_ctx - 1), 0)),
            pl.BlockSpec((tm, width), lambda i, *_: (jnp.maximum(i - n_ctx, 0), 0))]


def _ctx_or_lat(tm, c_ref, l_ref, rows=slice(None)):
    return jnp.where(pl.program_id(0) < N_CTX // tm, c_ref[rows, :], l_ref[rows, :])


def _norm_mod_to(h_scr, xc_ref, xl_ref, g_ref, sc_ref, sh_ref):
    g = g_ref[...]
    mul = 1.0 + sc_ref[0]
    add = sh_ref[0]
    for r in range(0, IN_TM, NORM_ROWS):
        x = _ctx_or_lat(IN_TM, xc_ref, xl_ref, slice(r, r + NORM_ROWS))
        h_scr[r:r + NORM_ROWS, :] = (_rms(x, g) * mul + add).astype(BF16)


W_IN_SEGS = {}
_off = 0
for _name, _w in (("cq", Q_RANK), ("ckv", KV_RANK), ("kr", ROPE_DIM), ("z", SSM_INNER), ("xbc", SSM_XBC),
                  ("dt", 2 * SSM_HEADS), ("gm", 2 * GM_WIDTH), ("gates", N_BRANCH * D_MODEL)):
    W_IN_SEGS[_name] = (_off, _w)
    _off += _w
ROW_UNIT = 32
ZB_TILE_STARTS = [(W_IN_SEGS[_name][0] + _k) // ROW_UNIT
                  for _name in ("z", "cq", "xbc", "gm", "gates")
                  for _k in range(0, W_IN_SEGS[_name][1], IN_TN)]
NT_DIMS = (((1,), (1,)), ((), ()))


def _inproj_body(tile_ref, xc_ref, xl_ref, g_ref, sc_ref, sh_ref, w_ref, o_ref, h_scr):
    del tile_ref
    @pl.when(pl.program_id(1) == 0)
    def _():
        _norm_mod_to(h_scr, xc_ref, xl_ref, g_ref, sc_ref, sh_ref)

    o_ref[...] = lax.dot_general(h_scr[...], w_ref[0].astype(BF16), NT_DIMS,
                                 preferred_element_type=F32).astype(o_ref.dtype)


def _inproj_small_body(xc_ref, xl_ref, g_ref, sc_ref, sh_ref, wckv_ref, wkr_ref, wdt_ref,
                       ckv_ref, zk_ref, h_scr, wzk_scr):
    _norm_mod_to(h_scr, xc_ref, xl_ref, g_ref, sc_ref, sh_ref)
    ckv_ref[...] = lax.dot_general(h_scr[...], wckv_ref[0].astype(BF16), NT_DIMS, preferred_element_type=F32)
    n_dt = 2 * SSM_HEADS
    wzk_scr[0:ROPE_DIM, :] = wkr_ref[0]
    wzk_scr[ROPE_DIM:ROPE_DIM + n_dt, :] = wdt_ref[0]
    wzk_scr[ROPE_DIM + n_dt:, :] = jnp.zeros((ZK_WIDTH - ROPE_DIM - n_dt, D_MODEL), F32)
    zk_ref[...] = lax.dot_general(h_scr[...], wzk_scr[...].astype(BF16), NT_DIMS, preferred_element_type=F32)


def _w_rows(layer, name):
    start, width = W_IN_SEGS[name]
    return pl.BlockSpec((pl.Element(1), pl.Element(width), pl.Element(D_MODEL)),
                        lambda i, j: (layer, start, 0))


def _inproj(x_c, x_l, g, sc_t, sh_t, w_in_t, layer):
    nt = N_TOK // IN_TM
    common = _ctx_lat_specs(IN_TM, D_MODEL, 2) + [
              pl.BlockSpec((1, D_MODEL), lambda i, j, *_: (0, 0)),
              pl.BlockSpec((1, 1, D_MODEL), lambda i, j, *_: (i, 0, 0)),
              pl.BlockSpec((1, 1, D_MODEL), lambda i, j, *_: (i, 0, 0))]
    tile_rows = pl.BlockSpec((pl.Element(1), pl.Element(IN_TN), pl.Element(D_MODEL)),
                             lambda i, j, tile: (layer, tile[j] * ROW_UNIT, 0))
    zb = pl.pallas_call(
        _inproj_body,
        grid_spec=pltpu.PrefetchScalarGridSpec(
            num_scalar_prefetch=1,
            grid=(nt, len(ZB_TILE_STARTS)),
            in_specs=common + [tile_rows],
            out_specs=pl.BlockSpec((IN_TM, IN_TN), lambda i, j, tile: (i, j)),
            scratch_shapes=[pltpu.VMEM((IN_TM, D_MODEL), BF16)]),
        out_shape=jax.ShapeDtypeStruct((N_TOK, ZB_WIDTH), BF16),
        compiler_params=_cparams("parallel", "arbitrary"),
        name="in_proj",
    )(jnp.asarray(ZB_TILE_STARTS, jnp.int32), x_c, x_l, g, sc_t, sh_t, w_in_t)
    ckv_raw, zk = pl.pallas_call(
        _inproj_small_body,
        grid=(nt, 1),
        in_specs=common + [_w_rows(layer, "ckv"), _w_rows(layer, "kr"), _w_rows(layer, "dt")],
        out_specs=[pl.BlockSpec((IN_TM, KV_RANK), lambda i, j: (i, 0)),
                   pl.BlockSpec((IN_TM, ZK_WIDTH), lambda i, j: (i, 0))],
        out_shape=[jax.ShapeDtypeStruct((N_TOK, KV_RANK), F32),
                   jax.ShapeDtypeStruct((N_TOK, ZK_WIDTH), F32)],
        scratch_shapes=[pltpu.VMEM((IN_TM, D_MODEL), BF16), pltpu.VMEM((ZK_WIDTH, D_MODEL), F32)],
        compiler_params=_cparams("parallel", "arbitrary"),
        name="in_proj_small",
    )(x_c, x_l, g, sc_t, sh_t, w_in_t, w_in_t, w_in_t)
    return zb, ckv_raw, zk


QKV_TM = 512
ATTN_SCALE = 1.0 / math.sqrt(NOPE_DIM + ROPE_DIM)


def _qproj_body(cq_ref, g_ref, w_ref, c_ref, s_ref, o_ref):
    qn = _rms(cq_ref[...].astype(F32), g_ref[...]).astype(BF16)
    q = jnp.dot(qn, w_ref[...], preferred_element_type=F32)
    c = c_ref[...]
    s = s_ref[...]
    for h in range(MLA_HEADS):
        lo = h * QK_HEAD
        r = q[:, lo + LANES:lo + QK_HEAD]
        o_ref[:, lo:lo + LANES] = (q[:, lo:lo + LANES] * ATTN_SCALE).astype(BF16)
        o_ref[:, lo + LANES:lo + QK_HEAD] = ((r * c + _swap16(r) * s) * ATTN_SCALE).astype(BF16)


def _qproj(zb, g_qn, w_uq_p, rope_c, rope_s):
    return pl.pallas_call(
        _qproj_body,
        grid=(N_TOK // QKV_TM,),
        in_specs=[pl.BlockSpec((QKV_TM, Q_RANK), lambda i: (i, ZB_CQ // Q_RANK)),
                  pl.BlockSpec((1, Q_RANK), lambda i: (0, 0)),
                  pl.BlockSpec((Q_RANK, MLA_HEADS * QK_HEAD), lambda i: (0, 0)),
                  pl.BlockSpec((QKV_TM, LANES), lambda i: (i, 0)),
                  pl.BlockSpec((QKV_TM, LANES), lambda i: (i, 0))],
        out_specs=pl.BlockSpec((QKV_TM, MLA_HEADS * QK_HEAD), lambda i: (i, 0)),
        out_shape=jax.ShapeDtypeStruct((N_TOK, MLA_HEADS * QK_HEAD), BF16),
        compiler_params=_cparams("parallel"),
        name="mla_q",
    )(zb, g_qn, w_uq_p, rope_c, rope_s)


def _kvproj_body(ckv_ref, zk_ref, g_ref, w_ref, c_ref, s_ref, ckv_o, kv_o, kr_o):
    ckv = _rms(ckv_ref[...], g_ref[...])
    ckv_o[...] = ckv
    kv_o[...] = jnp.dot(ckv.astype(BF16), w_ref[...], preferred_element_type=F32).astype(BF16)
    zk = zk_ref[...]
    is_rope = lax.broadcasted_iota(jnp.int32, zk.shape, 1) < ROPE_DIM
    kr = jnp.where(is_rope, zk, 0.0)
    rot = kr * c_ref[...] + _swap16(kr) * s_ref[...]
    kr_o[...] = jnp.where(is_rope, rot, 0.0).astype(BF16)


def _kvproj(ckv_raw, zk, g_kvn, w_ukv_p, rope_c, rope_s):
    return pl.pallas_call(
        _kvproj_body,
        grid=(N_TOK // QKV_TM,),
        in_specs=[pl.BlockSpec((QKV_TM, KV_RANK), lambda i: (i, 0)),
                  pl.BlockSpec((QKV_TM, ZK_WIDTH), lambda i: (i, 0)),
                  pl.BlockSpec((1, KV_RANK), lambda i: (0, 0)),
                  pl.BlockSpec((KV_RANK, 2 * MLA_WIDTH), lambda i: (0, 0)),
                  pl.BlockSpec((QKV_TM, LANES), lambda i: (i, 0)),
                  pl.BlockSpec((QKV_TM, LANES), lambda i: (i, 0))],
        out_specs=[pl.BlockSpec((QKV_TM, KV_RANK), lambda i: (i, 0)),
                   pl.BlockSpec((QKV_TM, 2 * MLA_WIDTH), lambda i: (i, 0)),
                   pl.BlockSpec((QKV_TM, LANES), lambda i: (i, 0))],
        out_shape=[jax.ShapeDtypeStruct((N_TOK, KV_RANK), F32),
                   jax.ShapeDtypeStruct((N_TOK, 2 * MLA_WIDTH), BF16),
                   jax.ShapeDtypeStruct((N_TOK, LANES), BF16)],
        compiler_params=_cparams("parallel"),
        name="mla_kv",
    )(ckv_raw, zk, g_kvn, w_ukv_p, rope_c, rope_s)


def _attn_body(n_parts, q_ref, *refs):
    o_ref = refs[3 * n_parts]
    for h in range(MLA_HEADS):
        qh = q_ref[:, h * QK_HEAD:(h + 1) * QK_HEAD]
        scores = []
        for p in range(n_parts):
            kn_ref, kr_ref = refs[3 * p], refs[3 * p + 1]
            kh = jnp.concatenate([kn_ref[:, h * LANES:(h + 1) * LANES], kr_ref[...]], axis=1)
            scores.append(lax.dot_general(qh, kh, (((1,), (1,)), ((), ())),
                                          preferred_element_type=F32))
        m = scores[0].max(axis=1, keepdims=True)
        for s in scores[1:]:
            m = jnp.maximum(m, s.max(axis=1, keepdims=True))
        den = 0.0
        acc = 0.0
        for p in range(n_parts):
            e = jnp.exp(scores[p] - m)
            den = den + e.sum(axis=1, keepdims=True)
            v_ref = refs[3 * p + 2]
            acc = acc + jnp.dot(e.astype(BF16), v_ref[:, h * LANES:(h + 1) * LANES],
                                preferred_element_type=F32)
        o_ref[:, h * LANES:(h + 1) * LANES] = (acc / den).astype(BF16)


def _attention(q, kv, kr, n_seq, seq_len, tq, row0, cache=None):
    nq = seq_len // tq
    q0 = row0 // tq
    s0 = row0 // seq_len
    in_specs = [pl.BlockSpec((tq, MLA_HEADS * QK_HEAD), lambda b, i: (q0 + b * nq + i, 0))]
    args = [q]
    if cache is not None:
        kv_c, kr_c, len_c = cache
        in_specs += [pl.BlockSpec((len_c, MLA_WIDTH), lambda b, i: (b, 0)),
                     pl.BlockSpec((len_c, LANES), lambda b, i: (b, 0)),
                     pl.BlockSpec((len_c, MLA_WIDTH), lambda b, i: (b, 1))]
        args += [kv_c, kr_c, kv_c]
    in_specs += [pl.BlockSpec((seq_len, MLA_WIDTH), lambda b, i: (s0 + b, 0)),
                 pl.BlockSpec((seq_len, LANES), lambda b, i: (s0 + b, 0)),
                 pl.BlockSpec((seq_len, MLA_WIDTH), lambda b, i: (s0 + b, 1))]
    args += [kv, kr, kv]
    n_parts = 1 if cache is None else 2
    return pl.pallas_call(
        functools.partial(_attn_body, n_parts),
        grid=(n_seq, nq),
        in_specs=in_specs,
        out_specs=pl.BlockSpec((tq, MLA_WIDTH), lambda b, i: (b * nq + i, 0)),
        out_shape=jax.ShapeDtypeStruct((n_seq * seq_len, MLA_WIDTH), BF16),
        compiler_params=_cparams("parallel", "arbitrary"),
        name="mla_attn_cache" if cache is not None else "mla_attn",
    )(*args)


MG_TM = 1024
MG_TN = 512
OUT_TM = 256


def _merge_body(ac_ref, al_ref, sc_ref, sl_ref, c_ref, wa_ref, ws_ref, wc_ref, ga_ref, gs_ref, gc_ref,
                o_ref):
    def branch(x, w_ref, gate_ref):
        y = jnp.dot(x, w_ref[0].astype(BF16), preferred_element_type=F32)
        return jax.nn.sigmoid(gate_ref[...].astype(F32)) * y

    o_ref[...] = (branch(_ctx_or_lat(MG_TM, ac_ref, al_ref), wa_ref, ga_ref)
                  + branch(_ctx_or_lat(MG_TM, sc_ref, sl_ref), ws_ref, gs_ref)
                  + branch(c_ref[...], wc_ref, gc_ref)).astype(BF16)


def _merge(attn_c, attn_l, ssm_c, ssm_l, gmo, wa, ws, wc, zb, layer):
    g0 = ZB_GATE // MG_TN
    gstep = D_MODEL // MG_TN
    pair = _ctx_lat_specs(MG_TM, MLA_WIDTH, 2)
    wspec = pl.BlockSpec((1, MLA_WIDTH, MG_TN), lambda i, j: (layer, 0, j))
    return pl.pallas_call(
        _merge_body,
        grid=(N_TOK // MG_TM, gstep),
        in_specs=pair + pair + [pl.BlockSpec((MG_TM, GM_WIDTH), lambda i, j: (i, 0)),
                                wspec, wspec, wspec,
                                pl.BlockSpec((MG_TM, MG_TN), lambda i, j: (i, g0 + j)),
                                pl.BlockSpec((MG_TM, MG_TN), lambda i, j: (i, g0 + gstep + j)),
                                pl.BlockSpec((MG_TM, MG_TN), lambda i, j: (i, g0 + 2 * gstep + j))],
        out_specs=pl.BlockSpec((MG_TM, MG_TN), lambda i, j: (i, j)),
        out_shape=jax.ShapeDtypeStruct((N_TOK, D_MODEL), BF16),
        compiler_params=_cparams("parallel", "arbitrary"),
        name="branch_merge",
    )(attn_c, attn_l, ssm_c, ssm_l, gmo, wa, ws, wc, zb, zb, zb)


def _split_bf16(x):
    hi = x.astype(BF16)
    return hi, (x - hi.astype(F32)).astype(BF16)


def _outproj_body(m_ref, w_ref, xc_ref, xl_ref, gt_ref, g_ref, sc_ref, sh_ref, wr_ref,
                  x1_ref, h2_ref, lg_ref):
    mix = jnp.dot(m_ref[...], w_ref[...], preferred_element_type=F32)
    x1 = _ctx_or_lat(OUT_TM, xc_ref, xl_ref) + gt_ref[0] * mix
    x1_ref[...] = x1
    h2 = _rms(x1, g_ref[...]) * (1.0 + sc_ref[0]) + sh_ref[0]
    h2_ref[...] = h2.astype(BF16)
    h_hi, h_lo = _split_bf16(h2)
    w_hi, w_lo = _split_bf16(wr_ref[...])
    lg_ref[...] = (jnp.dot(h_hi, w_hi, preferred_element_type=F32)
                   + jnp.dot(h_lo, w_hi, preferred_element_type=F32)
                   + jnp.dot(h_hi, w_lo, preferred_element_type=F32))


def _outproj(merged, w_out, x_c, x_l, gt_t, g2, sc_t, sh_t, w_router_p):
    mspec = pl.BlockSpec((1, 1, D_MODEL), lambda i: (i, 0, 0))
    return pl.pallas_call(
        _outproj_body,
        grid=(N_TOK // OUT_TM,),
        in_specs=[pl.BlockSpec((OUT_TM, D_MODEL), lambda i: (i, 0)),
                  pl.BlockSpec((D_MODEL, D_MODEL), lambda i: (0, 0))]
                 + _ctx_lat_specs(OUT_TM, D_MODEL, 1) + [
                  mspec,
                  pl.BlockSpec((1, D_MODEL), lambda i: (0, 0)),
                  mspec, mspec,
                  pl.BlockSpec((D_MODEL, LANES), lambda i: (0, 0))],
        out_specs=[pl.BlockSpec((OUT_TM, D_MODEL), lambda i: (i, 0)),
                   pl.BlockSpec((OUT_TM, D_MODEL), lambda i: (i, 0)),
                   pl.BlockSpec((OUT_TM, LANES), lambda i: (i, 0))],
        out_shape=[jax.ShapeDtypeStruct((N_TOK, D_MODEL), F32),
                   jax.ShapeDtypeStruct((N_TOK, D_MODEL), BF16),
                   jax.ShapeDtypeStruct((N_TOK, LANES), F32)],
        compiler_params=_cparams("parallel"),
        name="out_proj",
    )(merged, w_out, x_c, x_l, gt_t, g2, sc_t, sh_t, w_router_p)


PREFIX_BLK = 256
EXP_SEARCH_STEPS = 7
BISECT_STEPS = 40


def _prefix_count(mask):
    n = mask.shape[1]
    upper = (lax.broadcasted_iota(jnp.int32, (PREFIX_BLK, PREFIX_BLK), 0)
             < lax.broadcasted_iota(jnp.int32, (PREFIX_BLK, PREFIX_BLK), 1)).astype(BF16)
    run = jnp.zeros((mask.shape[0], 1), F32)
    outs = []
    for k in range(0, n, PREFIX_BLK):
        blk = mask[:, k:k + PREFIX_BLK]
        outs.append(jnp.dot(blk.astype(BF16), upper, preferred_element_type=F32) + run)
        run = run + jnp.sum(blk, axis=1, keepdims=True)
    return outs[0] if len(outs) == 1 else jnp.concatenate(outs, axis=1)


def _select_body(n_seq, seq_len, cap, lg_ref, slot_t_ref, slot_ref, aff_ref):
    lane = lax.broadcasted_iota(jnp.int32, (seq_len, LANES), 1)
    rows = []
    for s in range(n_seq):
        lg = jnp.where(lane < N_EXPERTS, lg_ref[s * seq_len:(s + 1) * seq_len, :], -jnp.inf)
        e = jnp.exp(lg - lg.max(axis=1, keepdims=True))
        aff = e / e.sum(axis=1, keepdims=True)
        rows.append(aff.T[:N_EXPERTS, :])
    a = rows[0] if n_seq == 1 else jnp.concatenate(rows, axis=0)

    def count_ge(t):
        return jnp.sum(jnp.where(a >= t, 1.0, 0.0), axis=1, keepdims=True)

    hi = jnp.full((a.shape[0], 1), 2.0, F32)
    for i in reversed(range(EXP_SEARCH_STEPS)):
        cand = hi * (2.0 ** -(2 ** i))
        hi = jnp.where(count_ge(cand) < cap, cand, hi)
    lo = jnp.where(hi <= 2.0 ** -126, 0.0, 0.5 * hi)
    for _ in range(BISECT_STEPS):
        mid = 0.5 * (lo + hi)
        ge = count_ge(mid) >= cap
        lo = jnp.where(ge, mid, lo)
        hi = jnp.where(ge, hi, mid)
    above = jnp.where(a >= hi, 1.0, 0.0)
    tied = jnp.where((a >= lo) & (a < hi), 1.0, 0.0)
    need = cap - jnp.sum(above, axis=1, keepdims=True)
    sel = above + tied * jnp.where(_prefix_count(tied) < need, 1.0, 0.0)
    slot = jnp.where(sel > 0.0, _prefix_count(sel), -1.0)
    slot_ref[...] = slot
    aff_ref[...] = a
    pad = jnp.full((LANES - N_EXPERTS, seq_len), -1.0, F32)
    for s in range(n_seq):
        blk = jnp.concatenate([slot[s * N_EXPERTS:(s + 1) * N_EXPERTS, :], pad], axis=0)
        slot_t_ref[s * seq_len:(s + 1) * seq_len, :] = blk.T.astype(jnp.int32)


def _select(logits, group, n_seq, seq_len, cap):
    rows = n_seq * N_EXPERTS
    n_rows = n_seq * seq_len
    return pl.pallas_call(
        functools.partial(_select_body, n_seq, seq_len, cap),
        grid=(1,),
        in_specs=[pl.BlockSpec((n_rows, LANES), lambda i: (group, 0))],
        out_specs=[pl.BlockSpec((n_rows, LANES), lambda i: (0, 0)),
                   pl.BlockSpec((rows, seq_len), lambda i: (0, 0)),
                   pl.BlockSpec((rows, seq_len), lambda i: (0, 0))],
        out_shape=[jax.ShapeDtypeStruct((n_rows, LANES), jnp.int32),
                   jax.ShapeDtypeStruct((rows, seq_len), F32),
                   jax.ShapeDtypeStruct((rows, seq_len), F32)],
        compiler_params=_cparams("arbitrary"),
        name="moe_select",
    )(logits)


def _gather_body(cap, epb, slot_ref, aff_ref, h_ref, xe_ref, ge_ref):
    n = h_ref.shape[0]
    if cap < LANES and epb == N_EXPERTS:
        assert cap & (cap - 1) == 0
        width = N_EXPERTS * cap
        row_expert = lax.shift_right_logical(lax.broadcasted_iota(jnp.int32, (width, N_EXPERTS), 0),
                                             cap.bit_length() - 1)
        spread = (lax.broadcasted_iota(jnp.int32, (width, N_EXPERTS), 1) == row_expert).astype(BF16)
        slot_x = jnp.dot(spread, slot_ref[...].astype(BF16), preferred_element_type=F32)
        target = (lax.broadcasted_iota(jnp.int32, (width, n), 0) & (cap - 1)).astype(F32)
        onehot = slot_x == target
        xe = jnp.dot(onehot.astype(BF16), h_ref[...], preferred_element_type=F32).astype(BF16)
        xe_ref[...] = xe.reshape(N_EXPERTS, cap, xe.shape[1])
        aff_x = sum(jnp.dot(spread, part, preferred_element_type=F32) for part in _split3(aff_ref[...]))
        gate = jnp.sum(jnp.where(onehot, aff_x, 0.0), axis=1, keepdims=True)
        ge_ref[...] = jnp.broadcast_to(gate, (width, LANES)).reshape(N_EXPERTS, cap, LANES)
        return
    step = pl.program_id(1)
    row = lax.broadcasted_iota(jnp.int32, (cap, n), 0).astype(F32)
    for k in range(epb):
        x = step * epb + k
        onehot = slot_ref[pl.ds(x, 1), :] == row
        xe_ref[k] = jnp.dot(onehot.astype(BF16), h_ref[...], preferred_element_type=F32).astype(BF16)
        gate = jnp.sum(jnp.where(onehot, aff_ref[pl.ds(x, 1), :], 0.0), axis=1, keepdims=True)
        ge_ref[k] = jnp.broadcast_to(gate, (cap, LANES))


def _gather(slot, aff, h2, n_seq, seq_len, cap, row0, epb):
    s0 = row0 // seq_len
    return pl.pallas_call(
        functools.partial(_gather_body, cap, epb),
        grid=(n_seq, N_EXPERTS // epb),
        in_specs=[pl.BlockSpec((N_EXPERTS, seq_len), lambda b, x: (b, 0)),
                  pl.BlockSpec((N_EXPERTS, seq_len), lambda b, x: (b, 0)),
                  pl.BlockSpec((seq_len, D_MODEL), lambda b, x: (s0 + b, 0))],
        out_specs=[pl.BlockSpec((epb, cap, D_MODEL), lambda b, x: (x, b, 0)),
                   pl.BlockSpec((epb, cap, LANES), lambda b, x: (x, b, 0))],
        out_shape=[jax.ShapeDtypeStruct((N_EXPERTS, n_seq * cap, D_MODEL), BF16),
                   jax.ShapeDtypeStruct((N_EXPERTS, n_seq * cap, LANES), F32)],
        compiler_params=_cparams("parallel", "arbitrary"),
        name="moe_gather",
    )(slot, aff, h2)


FFN_TF = 256


def _ffn_body(xc_ref, xl_ref, gc_ref, gl_ref, wg_ref, wu_ref, wd_ref, yc_ref, yl_ref, acc):
    f = pl.program_id(1)
    x = jnp.concatenate([xc_ref[0], xl_ref[0]], axis=0)
    gate = jnp.dot(x, wg_ref[0, 0].astype(BF16), preferred_element_type=F32)
    up = jnp.dot(x, wu_ref[0, 0].astype(BF16), preferred_element_type=F32)
    hid = (gate * jax.nn.sigmoid(gate) * up).astype(BF16)
    part = jnp.dot(hid, wd_ref[0, 0].astype(BF16), preferred_element_type=F32)

    @pl.when(f == 0)
    def _():
        acc[...] = part

    @pl.when(f > 0)
    def _():
        acc[...] += part

    @pl.when(f == pl.num_programs(1) - 1)
    def _():
        yc_ref[0] = (acc[:ROWS_CTX, :] * gc_ref[0][:, :1]).astype(BF16)
        yl_ref[0] = (acc[ROWS_CTX:, :] * gl_ref[0][:, :1]).astype(BF16)


def _expert_ffn(xe_c, xe_l, ge_c, ge_l, w_gate, w_up, w_down, layer):
    def xspec(rows, width):
        return pl.BlockSpec((1, rows, width), lambda i, f: (i, 0, 0))

    return pl.pallas_call(
        _ffn_body,
        grid=(N_EXPERTS, EXPERT_FF // FFN_TF),
        in_specs=[xspec(ROWS_CTX, D_MODEL), xspec(ROWS_LAT, D_MODEL),
                  xspec(ROWS_CTX, LANES), xspec(ROWS_LAT, LANES),
                  pl.BlockSpec((1, 1, D_MODEL, FFN_TF), lambda i, f: (layer, i, 0, f)),
                  pl.BlockSpec((1, 1, D_MODEL, FFN_TF), lambda i, f: (layer, i, 0, f)),
                  pl.BlockSpec((1, 1, FFN_TF, D_MODEL), lambda i, f: (layer, i, f, 0))],
        out_specs=[xspec(ROWS_CTX, D_MODEL), xspec(ROWS_LAT, D_MODEL)],
        out_shape=[jax.ShapeDtypeStruct((N_EXPERTS, ROWS_CTX, D_MODEL), BF16),
                   jax.ShapeDtypeStruct((N_EXPERTS, ROWS_LAT, D_MODEL), BF16)],
        scratch_shapes=[pltpu.VMEM((ROWS_CTX + ROWS_LAT, D_MODEL), F32)],
        compiler_params=_cparams("parallel", "arbitrary"),
        name="expert_ffn",
    )(xe_c, xe_l, ge_c, ge_l, w_gate, w_up, w_down)


CMB_TN = 512


def _combine_body(cap, slot_ref, y_ref, x_ref, gt_ref, o_ref, acc):
    n = slot_ref.shape[0]
    slot = slot_ref[...]
    if cap < LANES:
        assert cap & (cap - 1) == 0
        width = N_EXPERTS * cap
        lane_expert = lax.shift_right_logical(lax.broadcasted_iota(jnp.int32, (LANES, width), 1),
                                              cap.bit_length() - 1)
        spread = (lax.broadcasted_iota(jnp.int32, (LANES, width), 0) == lane_expert).astype(BF16)
        slot_x = jnp.dot(slot.astype(F32).astype(BF16), spread, preferred_element_type=F32)
        target = (lax.broadcasted_iota(jnp.int32, (n, width), 1) & (cap - 1)).astype(F32)
        onehot = (slot_x == target).astype(BF16)
        acc[...] = jnp.dot(onehot, y_ref[...].reshape(width, y_ref.shape[2]), preferred_element_type=F32)
    else:
        col = lax.broadcasted_iota(jnp.int32, (n, cap), 1)
        for x in range(N_EXPERTS):
            onehot = (slot[:, x:x + 1] == col).astype(BF16)
            part = jnp.dot(onehot, y_ref[x], preferred_element_type=F32)
            if x == 0:
                acc[...] = part
            else:
                acc[...] += part
    o_ref[...] = x_ref[...] + gt_ref[0] * acc[...]


def _combine(slot_t, y, x1, gt_t, n_seq, seq_len, cap, row0, tn):
    s0 = row0 // seq_len
    return pl.pallas_call(
        functools.partial(_combine_body, cap),
        grid=(n_seq, D_MODEL // tn),
        in_specs=[pl.BlockSpec((seq_len, LANES), lambda b, j: (b, 0)),
                  pl.BlockSpec((N_EXPERTS, cap, tn), lambda b, j: (0, b, j)),
                  pl.BlockSpec((seq_len, tn), lambda b, j: (s0 + b, j)),
                  pl.BlockSpec((1, 1, tn), lambda b, j: (b, 0, j))],
        out_specs=pl.BlockSpec((seq_len, tn), lambda b, j: (b, j)),
        out_shape=jax.ShapeDtypeStruct((n_seq * seq_len, D_MODEL), F32),
        scratch_shapes=[pltpu.VMEM((seq_len, tn), F32)],
        compiler_params=_cparams("parallel", "arbitrary"),
        name="moe_combine",
    )(slot_t, y, x1, gt_t)


CONV_TN = 256
CONV_HALO = 8
DT_LANE = ROPE_DIM


def _conv_body(x_ref, w_ref, b_ref, o_ref):
    seq_len = x_ref.shape[0]
    halo = jnp.zeros((CONV_HALO, x_ref.shape[1]), F32)
    ext = jnp.concatenate([halo, x_ref[...].astype(F32), halo], axis=0)
    w = w_ref[...]
    y = b_ref[...]
    for k in range(SSM_CONV):
        lo = CONV_HALO - SSM_CONV // 2 + k
        y = y + w[k:k + 1, :] * ext[lo:lo + seq_len, :]
    o_ref[...] = (y * jax.nn.sigmoid(y)).astype(BF16)


def _conv_silu(zb, conv_w8, conv_b, n_seq, seq_len, row0, tn):
    s0 = row0 // seq_len
    c0 = ZB_XBC // tn
    return pl.pallas_call(
        _conv_body,
        grid=(n_seq, SSM_XBC // tn),
        in_specs=[pl.BlockSpec((seq_len, tn), lambda b, j: (s0 + b, c0 + j)),
                  pl.BlockSpec((8, tn), lambda b, j: (0, j)),
                  pl.BlockSpec((1, tn), lambda b, j: (0, j))],
        out_specs=pl.BlockSpec((seq_len, tn), lambda b, j: (b, j)),
        out_shape=jax.ShapeDtypeStruct((n_seq * seq_len, SSM_XBC), BF16),
        compiler_params=_cparams("parallel", "arbitrary"),
        name="ssm_conv",
    )(zb, conv_w8, conv_b)


def _split3(x):
    hi = x.astype(BF16)
    r = x - hi.astype(F32)
    mid = r.astype(BF16)
    return hi, mid, (r - mid.astype(F32)).astype(BF16)


def _ssd_body(nc, xa_ref, zk_ref, z_ref, h0_ref, bias_ref, a_ref, d_ref, g_ref, ef_ref, eb_ref,
              o_ref, st_ref, hb_in, hf_cur, hb_cur):
    q = SSM_CHUNK
    half = SSM_INNER // SSM_GROUPS
    ii = lax.broadcasted_iota(jnp.int32, (q, q), 0)
    jj = lax.broadcasted_iota(jnp.int32, (q, q), 1)
    lower = ii >= jj
    upper = ii <= jj
    lower_b = lower.astype(BF16)
    upper_b = upper.astype(BF16)
    lane = lax.broadcasted_iota(jnp.int32, (q, LANES), 1)
    is_dt = (lane >= DT_LANE) & (lane < DT_LANE + 2 * SSM_HEADS)
    is_fwd = lane < DT_LANE + SSM_HEADS

    def tri_cumsum(tri, v):
        hi, mid, lo = _split3(v)
        return (jnp.dot(tri, hi, preferred_element_type=F32) + jnp.dot(tri, mid, preferred_element_type=F32)
                + jnp.dot(tri, lo, preferred_element_type=F32))

    def expand(v, e_ref):
        hi, lo = _split_bf16(v)
        return (jnp.dot(hi, e_ref[...], preferred_element_type=F32)
                + jnp.dot(lo, e_ref[...], preferred_element_type=F32))

    def chunk_factors(r0):
        dt = jnp.where(is_dt, jax.nn.softplus(zk_ref[pl.ds(r0, q), :] + bias_ref[...]), 0.0)
        dta = dt * a_ref[...]
        cum = jnp.where(is_fwd, tri_cumsum(lower_b, dta), tri_cumsum(upper_b, dta))
        tot = jnp.sum(dta, axis=0, keepdims=True)
        return dt, cum, tot

    def state_update(r0, dend_x, cdec_x, h_prev):
        x = xa_ref[pl.ds(r0, q), 0:SSM_INNER].astype(F32)
        xs = (x * dend_x).astype(BF16)
        parts = []
        for g in range(SSM_GROUPS):
            lo = SSM_INNER + g * SSM_STATE
            b_t = xa_ref[pl.ds(r0, q), lo:lo + SSM_STATE].astype(F32).T.astype(BF16)
            parts.append(jnp.dot(b_t, xs[:, g * half:(g + 1) * half], preferred_element_type=F32))
        return cdec_x * h_prev + jnp.concatenate(parts, axis=1)

    hf_cur[...] = h0_ref[0, 0]
    hb_cur[...] = h0_ref[0, 1]

    def bwd_step(t, carry):
        c = nc - 1 - t
        r0 = pl.multiple_of(c * q, q)
        dt, cum, tot = chunk_factors(r0)
        hb_in[c] = hb_cur[...]
        dend_x = expand(jnp.exp(tot - cum) * dt, eb_ref)
        cdec_x = expand(jnp.broadcast_to(jnp.exp(tot), (8, LANES)), eb_ref)[0:1]
        hb_cur[...] = state_update(r0, dend_x, cdec_x, hb_cur[...])
        return carry

    lax.fori_loop(0, nc, bwd_step, 0)

    def fwd_step(c, carry):
        r0 = pl.multiple_of(c * q, q)
        dt, cum, tot = chunk_factors(r0)
        cum_t = cum.T
        dt_t = dt.T
        eoff = jnp.exp(cum)
        x_bf = xa_ref[pl.ds(r0, q), 0:SSM_INNER]
        cb = []
        c_bf = []
        for g in range(SSM_GROUPS):
            lo_b = SSM_INNER + g * SSM_STATE
            lo_c = SSM_INNER + (SSM_GROUPS + g) * SSM_STATE
            c_g = xa_ref[pl.ds(r0, q), lo_c:lo_c + SSM_STATE]
            b_g = xa_ref[pl.ds(r0, q), lo_b:lo_b + SSM_STATE]
            c_bf.append(c_g)
            cb.append(lax.dot_general(c_g, b_g, (((1,), (1,)), ((), ())), preferred_element_type=F32))

        def head_matrix(h):
            f = DT_LANE + h
            b = DT_LANE + SSM_HEADS + h
            lf = jnp.where(lower, jnp.exp(cum[:, f:f + 1] - cum_t[f:f + 1, :]), 0.0) * dt_t[f:f + 1, :]
            lb = jnp.where(upper, jnp.exp(cum[:, b:b + 1] - cum_t[b:b + 1, :]), 0.0) * dt_t[b:b + 1, :]
            return (cb[h // (SSM_HEADS // SSM_GROUPS)] * (lf + lb)).astype(BF16)

        lane_lo = lane < SSM_HEADDIM
        pairs = []
        for hp in range(SSM_HEADS // 2):
            x_pair = x_bf[:, hp * LANES:(hp + 1) * LANES]
            y0 = jnp.dot(head_matrix(2 * hp), x_pair, preferred_element_type=F32)
            y1 = jnp.dot(head_matrix(2 * hp + 1), x_pair, preferred_element_type=F32)
            pairs.append(jnp.where(lane_lo, y0, y1))
        y = jnp.concatenate(pairs, axis=1)

        def off_diag(h_t, factor_x):
            h_bf = h_t.astype(BF16)
            parts = [jnp.dot(c_bf[g], h_bf[:, g * half:(g + 1) * half], preferred_element_type=F32)
                     for g in range(SSM_GROUPS)]
            return jnp.concatenate(parts, axis=1) * factor_x

        y = y + off_diag(hf_cur[...], expand(eoff, ef_ref)) + off_diag(hb_in[c], expand(eoff, eb_ref))
        y = y + d_ref[...] * x_bf.astype(F32)
        zg = z_ref[pl.ds(r0, q), :].astype(F32)
        o_ref[pl.ds(r0, q), :] = _rms(y * (zg * jax.nn.sigmoid(zg)), g_ref[...]).astype(BF16)

        dend_x = expand(jnp.exp(tot - cum) * dt, ef_ref)
        cdec_x = expand(jnp.broadcast_to(jnp.exp(tot), (8, LANES)), ef_ref)[0:1]
        hf_cur[...] = state_update(r0, dend_x, cdec_x, hf_cur[...])
        return carry

    lax.fori_loop(0, nc, fwd_step, 0)
    st_ref[0, 0] = hf_cur[...]
    st_ref[0, 1] = hb_cur[...]


def _ssd(xa, zk, zb, h0_t, dt_bias_p, a_p, d_x, g_ssm, e_f, e_b, n_seq, seq_len, row0):
    s0 = row0 // seq_len
    nc = seq_len // SSM_CHUNK
    vec = lambda w: pl.BlockSpec((1, w), lambda b: (0, 0))
    return pl.pallas_call(
        functools.partial(_ssd_body, nc),
        grid=(n_seq,),
        in_specs=[pl.BlockSpec((seq_len, SSM_XBC), lambda b: (b, 0)),
                  pl.BlockSpec((seq_len, ZK_WIDTH), lambda b: (s0 + b, 0)),
                  pl.BlockSpec((seq_len, SSM_INNER), lambda b: (s0 + b, ZB_Z // SSM_INNER)),
                  pl.BlockSpec((1, 2, SSM_STATE, SSM_INNER), lambda b: (b, 0, 0, 0)),
                  vec(LANES), vec(LANES), vec(SSM_INNER), vec(SSM_INNER),
                  pl.BlockSpec((LANES, SSM_INNER), lambda b: (0, 0)),
                  pl.BlockSpec((LANES, SSM_INNER), lambda b: (0, 0))],
        out_specs=[pl.BlockSpec((seq_len, SSM_INNER), lambda b: (b, 0)),
                   pl.BlockSpec((1, 2, SSM_STATE, SSM_INNER), lambda b: (b, 0, 0, 0))],
        out_shape=[jax.ShapeDtypeStruct((n_seq * seq_len, SSM_INNER), BF16),
                   jax.ShapeDtypeStruct((n_seq, 2, SSM_STATE, SSM_INNER), F32)],
        scratch_shapes=[pltpu.VMEM((nc, SSM_STATE, SSM_INNER), F32),
                        pltpu.VMEM((SSM_STATE, SSM_INNER), F32),
                        pltpu.VMEM((SSM_STATE, SSM_INNER), F32)],
        compiler_params=_cparams("parallel"),
        name="ssd",
    )(xa, zk, zb, h0_t, dt_bias_p, a_p, d_x, g_ssm, e_f, e_b)


GM_TM = 512
GM_GROUP_W = GM_WIDTH // GM_GROUPS


def _gmlp_body(u_ref, v_ref, g_ref, w_ref, b_ref, o_ref):
    for r in range(0, GM_TM, GM_CHUNK):
        u = jax.nn.gelu(u_ref[r:r + GM_CHUNK, :].astype(F32))
        vg = _rms(jax.nn.gelu(v_ref[r:r + GM_CHUNK, :].astype(F32)), g_ref[...]).astype(BF16)
        sv = jnp.concatenate(
            [jnp.dot(w_ref[k], vg[:, k * GM_GROUP_W:(k + 1) * GM_GROUP_W], preferred_element_type=F32)
             for k in range(GM_GROUPS)], axis=1)
        o_ref[r:r + GM_CHUNK, :] = (u * (sv + b_ref[...])).astype(BF16)


def _gmlp(zb, g_gv, w_sp, b_x):
    return pl.pallas_call(
        _gmlp_body,
        grid=(N_TOK // GM_TM,),
        in_specs=[pl.BlockSpec((GM_TM, GM_WIDTH), lambda i: (i, ZB_GM // GM_WIDTH)),
                  pl.BlockSpec((GM_TM, GM_WIDTH), lambda i: (i, ZB_GM // GM_WIDTH + 1)),
                  pl.BlockSpec((1, GM_WIDTH), lambda i: (0, 0)),
                  pl.BlockSpec((GM_GROUPS, GM_CHUNK, GM_CHUNK), lambda i: (0, 0, 0)),
                  pl.BlockSpec((GM_CHUNK, GM_WIDTH), lambda i: (0, 0))],
        out_specs=pl.BlockSpec((GM_TM, GM_WIDTH), lambda i: (i, 0)),
        out_shape=jax.ShapeDtypeStruct((N_TOK, GM_WIDTH), BF16),
        compiler_params=_cparams("parallel"),
        name="gmlp",
    )(zb, zb, g_gv, w_sp, b_x)


FN_TM = 512


def _final_norm_body(x_ref, g_ref, o_ref):
    o_ref[...] = _rms(x_ref[...], g_ref[...])


def _final_norm(x, g):
    rows = x.shape[0]
    return pl.pallas_call(
        _final_norm_body,
        grid=(rows // FN_TM,),
        in_specs=[pl.BlockSpec((FN_TM, D_MODEL), lambda i: (i, 0)),
                  pl.BlockSpec((1, D_MODEL), lambda i: (0, 0))],
        out_specs=pl.BlockSpec((FN_TM, D_MODEL), lambda i: (i, 0)),
        out_shape=jax.ShapeDtypeStruct((rows, D_MODEL), F32),
        compiler_params=_cparams("parallel"),
        name="final_norm",
    )(x, g)


def _dt_lanes(v):
    return jnp.pad(v.reshape(1, 2 * SSM_HEADS).astype(F32),
                   ((0, 0), (DT_LANE, LANES - DT_LANE - 2 * SSM_HEADS)))


def _head_expanders():
    lane = lax.broadcasted_iota(jnp.int32, (LANES, SSM_INNER), 0)
    head = lax.broadcasted_iota(jnp.int32, (LANES, SSM_INNER), 1) // SSM_HEADDIM
    e_f = (lane == head + DT_LANE).astype(BF16)
    e_b = (lane == head + DT_LANE + SSM_HEADS).astype(BF16)
    return e_f, e_b


MOD_TN = 1024
MOD_ROWS = 8
N_COND = 1 + DEC_BATCH


def _mod_body(ct_ref, w_ref, b_ref, o_ref):
    c = ct_ref[...]
    act = c * jax.nn.sigmoid(c)
    sub = lax.broadcasted_iota(jnp.int32, (MOD_ROWS, MOD_TN), 0)
    out = jnp.zeros((MOD_ROWS, MOD_TN), F32)
    for r in range(N_COND):
        y = jnp.sum(act[:, r:r + 1] * w_ref[0], axis=0, keepdims=True)
        out = jnp.where(sub == r, y, out)
    o_ref[0] = out + b_ref[0]


def _modulation(cond_t, w_mod, b_mod):
    n = 6 * D_MODEL
    return pl.pallas_call(
        _mod_body,
        grid=(DEPTH, n // MOD_TN),
        in_specs=[pl.BlockSpec((D_MODEL, MOD_ROWS), lambda l, j: (0, 0)),
                  pl.BlockSpec((1, D_MODEL, MOD_TN), lambda l, j: (l, 0, j)),
                  pl.BlockSpec((1, 1, MOD_TN), lambda l, j: (l, 0, j))],
        out_specs=pl.BlockSpec((1, MOD_ROWS, MOD_TN), lambda l, j: (l, 0, j)),
        out_shape=jax.ShapeDtypeStruct((DEPTH, MOD_ROWS, n), F32),
        compiler_params=_cparams("parallel", "arbitrary"),
        name="modulation",
    )(cond_t, w_mod, b_mod[:, None, :])


def _prep_w_uq(w):
    w = w.reshape(Q_RANK, MLA_HEADS, NOPE_DIM + ROPE_DIM)
    pad = jnp.zeros((Q_RANK, MLA_HEADS, QK_HEAD - NOPE_DIM - ROPE_DIM), w.dtype)
    return jnp.concatenate([w, pad], axis=2).reshape(Q_RANK, MLA_HEADS * QK_HEAD).astype(BF16)


def _prep_w_ukv(w):
    w = w.reshape(KV_RANK, MLA_HEADS, NOPE_DIM + V_DIM)
    return jnp.concatenate([w[:, :, :NOPE_DIM].reshape(KV_RANK, MLA_WIDTH),
                            w[:, :, NOPE_DIM:].reshape(KV_RANK, MLA_WIDTH)], axis=1).astype(BF16)


def _rope_tables(n_lat):
    rows = n_lat // GRID_W
    row = jnp.repeat(jnp.arange(rows), GRID_W).astype(F32)
    col = jnp.tile(jnp.arange(GRID_W), rows).astype(F32)
    nf = ROPE_DIM // 4
    freqs = jnp.power(ROPE_THETA, -jnp.arange(nf, dtype=F32) / nf)
    ang = jnp.stack([row[:, None] * freqs, col[:, None] * freqs], axis=1)
    cos, sin = jnp.cos(ang), jnp.sin(ang)
    c64 = jnp.concatenate([cos[:, 0], cos[:, 0], cos[:, 1], cos[:, 1]], axis=1)
    s64 = jnp.concatenate([-sin[:, 0], sin[:, 0], -sin[:, 1], sin[:, 1]], axis=1)
    c_lat = jnp.concatenate([c64, jnp.ones((n_lat, LANES - ROPE_DIM), F32)], axis=1)
    s_lat = jnp.concatenate([s64, jnp.zeros((n_lat, LANES - ROPE_DIM), F32)], axis=1)
    rope_c = jnp.concatenate([jnp.ones((N_CTX, LANES), F32), jnp.tile(c_lat, (DEC_BATCH, 1))], axis=0)
    rope_s = jnp.concatenate([jnp.zeros((N_CTX, LANES), F32), jnp.tile(s_lat, (DEC_BATCH, 1))], axis=0)
    return rope_c, rope_s


def _rows_mod(vec3, tm):
    idx = [0] * (N_CTX // tm) + [1] * (DEC_SEQ // tm) + [2] * (DEC_SEQ // tm)
    return vec3[jnp.array(idx)][:, None, :]


def kernel(x_prompt, x_sample, c, cache_ckv, cache_krope, state_ssm, c_ctx, w_mod, b_mod,
           g_norm1, g_norm2, w_in, g_qn, w_uq, g_kvn, w_ukv, conv_w, conv_b, dt_bias, a_log,
           d_skip, g_ssm, g_gv, w_sp, b_sp, w_br_attn, w_br_ssm, w_br_gmlp, w_out, w_router,
           w_gate, w_up, w_down, g_final):
    rope_c, rope_s = _rope_tables(DEC_SEQ)
    x_c = x_prompt.reshape(N_CTX, D_MODEL)
    x_l = x_sample.reshape(N_LAT, D_MODEL)
    cond_t = jnp.pad(jnp.concatenate([c_ctx[None, :], c], axis=0).T, ((0, 0), (0, MOD_ROWS - N_COND)))
    mod_all = _modulation(cond_t, w_mod, b_mod)
    e_f, e_b = _head_expanders()
    w_in_t = jnp.swapaxes(w_in, 1, 2)
    h0_c = jnp.zeros((BATCH, 2, SSM_STATE, SSM_INNER), F32)

    ckvs, krs, sts = [], [], []
    for l in range(DEPTH):
        mod = mod_all[l, :1 + DEC_BATCH].reshape(1 + DEC_BATCH, 6, D_MODEL)
        sh1, sc1, gt1, sh2, sc2, gt2 = [mod[:, k] for k in range(6)]

        zb, ckv_raw, zk = _inproj(x_c, x_l, g_norm1[l][None, :], _rows_mod(sc1, IN_TM), _rows_mod(sh1, IN_TM),
                                  w_in_t, l)

        q = _qproj(zb, g_qn[l][None, :], _prep_w_uq(w_uq[l]), rope_c, rope_s)
        w_ukv_p = _prep_w_ukv(w_ukv[l])
        ckv, kv, krot = _kvproj(ckv_raw, zk, g_kvn[l][None, :], w_ukv_p, rope_c, rope_s)
        kv_cache = _matmul(cache_ckv[:, l].reshape(DEC_BATCH * PAST_LEN, KV_RANK), w_ukv_p,
                           out_dtype=BF16, tn=2 * MLA_WIDTH, name="mla_kv_cache")
        kr_cache = jnp.pad(cache_krope[:, l].reshape(DEC_BATCH * PAST_LEN, ROPE_DIM),
                           ((0, 0), (0, LANES - ROPE_DIM))).astype(BF16)
        attn_c = _attention(q, kv, krot, BATCH, SEQ, SEQ, 0)
        attn_l = _attention(q, kv, krot, DEC_BATCH, DEC_SEQ, 256, N_CTX, cache=(kv_cache, kr_cache, PAST_LEN))

        conv_w8 = jnp.pad(conv_w[l], ((0, 8 - SSM_CONV), (0, 0)))
        ssd_par = (_dt_lanes(dt_bias[l]), _dt_lanes(-jnp.exp(a_log[l])),
                   jnp.repeat(d_skip[l], SSM_HEADDIM)[None, :], g_ssm[l][None, :], e_f, e_b)
        xa_c = _conv_silu(zb, conv_w8, conv_b[l][None, :], BATCH, SEQ, 0, SSM_XBC)
        xa_l = _conv_silu(zb, conv_w8, conv_b[l][None, :], DEC_BATCH, DEC_SEQ, N_CTX, CONV_TN)
        h0_l = jnp.transpose(state_ssm[:, l], (0, 1, 4, 2, 3)).reshape(DEC_BATCH, 2, SSM_STATE, SSM_INNER)
        ssm_c, st_c = _ssd(xa_c, zk, zb, h0_c, *ssd_par, BATCH, SEQ, 0)
        ssm_l, _ = _ssd(xa_l, zk, zb, h0_l, *ssd_par, DEC_BATCH, DEC_SEQ, N_CTX)
        gmo = _gmlp(zb, g_gv[l][None, :], w_sp[l].astype(BF16), jnp.repeat(b_sp[l].T, GM_GROUP_W, axis=1))
        st_c = jnp.transpose(st_c.reshape(BATCH, 2, SSM_STATE, SSM_HEADS, SSM_HEADDIM), (0, 1, 3, 4, 2))

        merged = _merge(attn_c, attn_l, ssm_c, ssm_l, gmo, w_br_attn, w_br_ssm, w_br_gmlp, zb, l)
        w_router_p = jnp.pad(w_router[l], ((0, 0), (0, LANES - N_EXPERTS)))
        x1, h2, logits = _outproj(merged, w_out[l].astype(BF16), x_c, x_l, _rows_mod(gt1, OUT_TM),
                                  g_norm2[l][None, :], _rows_mod(sc2, OUT_TM), _rows_mod(sh2, OUT_TM),
                                  w_router_p)

        slot_c, srow_c, arow_c = _select(logits, 0, BATCH, SEQ, CAP_CTX)
        slot_l, srow_l, arow_l = _select(logits, 1, DEC_BATCH, DEC_SEQ, CAP_LAT)
        xe_c, ge_c = _gather(srow_c, arow_c, h2, BATCH, SEQ, CAP_CTX, 0, N_EXPERTS)
        xe_l, ge_l = _gather(srow_l, arow_l, h2, DEC_BATCH, DEC_SEQ, CAP_LAT, N_CTX, 2)
        y_c, y_l = _expert_ffn(xe_c, xe_l, ge_c, ge_l, w_gate, w_up, w_down, l)
        gt2_c = jnp.broadcast_to(gt2[0][None, None, :], (BATCH, 1, D_MODEL))
        gt2_l = gt2[1:][:, None, :]
        x_c = _combine(slot_c, y_c, x1, gt2_c, BATCH, SEQ, CAP_CTX, 0, D_MODEL)
        x_l = _combine(slot_l, y_l, x1, gt2_l, DEC_BATCH, DEC_SEQ, CAP_LAT, N_CTX, CMB_TN)

        ckvs.append(ckv[:N_CTX].reshape(BATCH, SEQ, KV_RANK))
        krs.append(zk[:N_CTX, :ROPE_DIM].reshape(BATCH, SEQ, ROPE_DIM))
        sts.append(st_c)

    y_prompt = _final_norm(x_c, g_final[None, :]).reshape(BATCH, SEQ, D_MODEL)
    y_sample = _final_norm(x_l, g_final[None, :]).reshape(DEC_BATCH, DEC_SEQ, D_MODEL)
    return (y_prompt, y_sample, jnp.stack(ckvs, axis=1), jnp.stack(krs, axis=1), jnp.stack(sts, axis=1))
```

```python
import functools
import math

import jax
import jax.numpy as jnp
from jax import lax
from jax.experimental import pallas as pl
from jax.experimental.pallas import tpu as pltpu

D_MODEL = 2048
BATCH = 16
SEQ = 256
DEPTH = 2
DEC_BATCH = 2
DEC_SEQ = 2048
PAST_LEN = 512
GRID_W = 64
ROPE_THETA = 10000.0
EPS = 1e-6
MLA_HEADS = 8
Q_RANK = 512
KV_RANK = 256
NOPE_DIM = 128
ROPE_DIM = 64
V_DIM = 128
MLA_WIDTH = MLA_HEADS * V_DIM
SSM_HEADS = 16
SSM_HEADDIM = 64
SSM_INNER = SSM_HEADS * SSM_HEADDIM
SSM_GROUPS = 2
SSM_STATE = 128
SSM_CONV = 5
SSM_CHUNK = 128
SSM_XBC = SSM_INNER + 2 * SSM_GROUPS * SSM_STATE
GM_WIDTH = 1024
GM_GROUPS = 4
GM_CHUNK = 128
N_BRANCH = 3
N_EXPERTS = 16
EXPERT_FF = 1024
EC_CAPACITY = 2

N_CTX = BATCH * SEQ
N_LAT = DEC_BATCH * DEC_SEQ
N_TOK = N_CTX + N_LAT
CAP_CTX = EC_CAPACITY * SEQ // N_EXPERTS
CAP_LAT = EC_CAPACITY * DEC_SEQ // N_EXPERTS
ROWS_CTX = BATCH * CAP_CTX
ROWS_LAT = DEC_BATCH * CAP_LAT

LANES = 128
QK_HEAD = 2 * LANES

ZB_Z = 0
ZB_CQ = SSM_INNER
ZB_XBC = ZB_CQ + Q_RANK
ZB_GM = ZB_XBC + SSM_XBC
ZB_GATE = ZB_GM + 2 * GM_WIDTH
ZB_WIDTH = ZB_GATE + N_BRANCH * D_MODEL
ZK_WIDTH = LANES

V7X_VMEM_LIMIT_BYTES = 56 * 1024 * 1024

BF16 = jnp.bfloat16
F32 = jnp.float32


def _cparams(*sem):
    return pltpu.CompilerParams(dimension_semantics=sem, vmem_limit_bytes=V7X_VMEM_LIMIT_BYTES)


def _rms(x, g):
    return x * lax.rsqrt(jnp.mean(x * x, axis=-1, keepdims=True) + EPS) * g


def _swap16(x):
    lane = lax.broadcasted_iota(jnp.int32, x.shape, 1)
    return jnp.where((lane % 32) < 16, pltpu.roll(x, LANES - 16, 1), pltpu.roll(x, 16, 1))


def _mm_body(x_ref, w_ref, o_ref):
    o_ref[...] = jnp.dot(x_ref[...].astype(BF16), w_ref[...].astype(BF16),
                         preferred_element_type=F32).astype(o_ref.dtype)


def _matmul(x, w, out_dtype=F32, tm=512, tn=512, name="matmul"):
    m, k = x.shape
    _, n = w.shape
    tn = min(tn, n)
    return pl.pallas_call(
        _mm_body,
        grid=(m // tm, n // tn),
        in_specs=[pl.BlockSpec((tm, k), lambda i, j: (i, 0)),
                  pl.BlockSpec((k, tn), lambda i, j: (0, j))],
        out_specs=pl.BlockSpec((tm, tn), lambda i, j: (i, j)),
        out_shape=jax.ShapeDtypeStruct((m, n), out_dtype),
        compiler_params=_cparams("parallel", "arbitrary"),
        name=name,
    )(x, w)


IN_TM = 1024
IN_TN = 512
NORM_ROWS = 256


def _ctx_lat_specs(tm, width, n_grid_axes):
    del n_grid_axes
    n_ctx = N_CTX // tm
    return [pl.BlockSpec((tm, width), lambda i, *_: (jnp.minimum(i, n_ctx - 1), 0)),
            pl.BlockSpec((tm, width), lambda i, *_: (jnp.maximum(i - n_ctx, 0), 0))]


def _ctx_or_lat(tm, c_ref, l_ref, rows=slice(None)):
    return jnp.where(pl.program_id(0) < N_CTX // tm, c_ref[rows, :], l_ref[rows, :])


def _norm_mod_to(h_scr, xc_ref, xl_ref, g_ref, sc_ref, sh_ref):
    g = g_ref[...]
    mul = 1.0 + sc_ref[0]
    add = sh_ref[0]
    for r in range(0, IN_TM, NORM_ROWS):
        x = _ctx_or_lat(IN_TM, xc_ref, xl_ref, slice(r, r + NORM_ROWS))
        h_scr[r:r + NORM_ROWS, :] = (_rms(x, g) * mul + add).astype(BF16)


W_IN_SEGS = {}
_off = 0
for _name, _w in (("cq", Q_RANK), ("ckv", KV_RANK), ("kr", ROPE_DIM), ("z", SSM_INNER), ("xbc", SSM_XBC),
                  ("dt", 2 * SSM_HEADS), ("gm", 2 * GM_WIDTH), ("gates", N_BRANCH * D_MODEL)):
    W_IN_SEGS[_name] = (_off, _w)
    _off += _w
ROW_UNIT = 32
ZB_TILE_STARTS = [(W_IN_SEGS[_name][0] + _k) // ROW_UNIT
                  for _name in ("z", "cq", "xbc", "gm", "gates")
                  for _k in range(0, W_IN_SEGS[_name][1], IN_TN)]
NT_DIMS = (((1,), (1,)), ((), ()))


def _inproj_body(tile_ref, xc_ref, xl_ref, g_ref, sc_ref, sh_ref, w_ref, o_ref, h_scr):
    del tile_ref
    @pl.when(pl.program_id(1) == 0)
    def _():
        _norm_mod_to(h_scr, xc_ref, xl_ref, g_ref, sc_ref, sh_ref)

    acc = lax.dot_general(h_scr[...], w_ref[0].astype(BF16), NT_DIMS, preferred_element_type=F32)
    is_gate = pl.program_id(1) >= ZB_GATE // IN_TN
    o_ref[...] = jnp.where(is_gate, jax.nn.sigmoid(acc), acc).astype(o_ref.dtype)


def _inproj_small_body(xc_ref, xl_ref, g_ref, sc_ref, sh_ref, wckv_ref, wkr_ref, wdt_ref,
                       ckv_ref, zk_ref, h_scr, wzk_scr):
    _norm_mod_to(h_scr, xc_ref, xl_ref, g_ref, sc_ref, sh_ref)
    ckv_ref[...] = lax.dot_general(h_scr[...], wckv_ref[0].astype(BF16), NT_DIMS, preferred_element_type=F32)
    n_dt = 2 * SSM_HEADS
    wzk_scr[0:ROPE_DIM, :] = wkr_ref[0]
    wzk_scr[ROPE_DIM:ROPE_DIM + n_dt, :] = wdt_ref[0]
    wzk_scr[ROPE_DIM + n_dt:, :] = jnp.zeros((ZK_WIDTH - ROPE_DIM - n_dt, D_MODEL), F32)
    zk_ref[...] = lax.dot_general(h_scr[...], wzk_scr[...].astype(BF16), NT_DIMS, preferred_element_type=F32)


def _w_rows(layer, name):
    start, width = W_IN_SEGS[name]
    return pl.BlockSpec((pl.Element(1), pl.Element(width), pl.Element(D_MODEL)),
                        lambda i, j: (layer, start, 0))


def _inproj(x_c, x_l, g, sc_t, sh_t, w_in_t, layer):
    nt = N_TOK // IN_TM
    common = _ctx_lat_specs(IN_TM, D_MODEL, 2) + [
              pl.BlockSpec((1, D_MODEL), lambda i, j, *_: (0, 0)),
              pl.BlockSpec((1, 1, D_MODEL), lambda i, j, *_: (i, 0, 0)),
              pl.BlockSpec((1, 1, D_MODEL), lambda i, j, *_: (i, 0, 0))]
    tile_rows = pl.BlockSpec((pl.Element(1), pl.Element(IN_TN), pl.Element(D_MODEL)),
                             lambda i, j, tile: (layer, tile[j] * ROW_UNIT, 0))
    zb = pl.pallas_call(
        _inproj_body,
        grid_spec=pltpu.PrefetchScalarGridSpec(
            num_scalar_prefetch=1,
            grid=(nt, len(ZB_TILE_STARTS)),
            in_specs=common + [tile_rows],
            out_specs=pl.BlockSpec((IN_TM, IN_TN), lambda i, j, tile: (i, j)),
            scratch_shapes=[pltpu.VMEM((IN_TM, D_MODEL), BF16)]),
        out_shape=jax.ShapeDtypeStruct((N_TOK, ZB_WIDTH), BF16),
        compiler_params=_cparams("parallel", "arbitrary"),
        name="in_proj",
    )(jnp.asarray(ZB_TILE_STARTS, jnp.int32), x_c, x_l, g, sc_t, sh_t, w_in_t)
    ckv_raw, zk = pl.pallas_call(
        _inproj_small_body,
        grid=(nt, 1),
        in_specs=common + [_w_rows(layer, "ckv"), _w_rows(layer, "kr"), _w_rows(layer, "dt")],
        out_specs=[pl.BlockSpec((IN_TM, KV_RANK), lambda i, j: (i, 0)),
                   pl.BlockSpec((IN_TM, ZK_WIDTH), lambda i, j: (i, 0))],
        out_shape=[jax.ShapeDtypeStruct((N_TOK, KV_RANK), F32),
                   jax.ShapeDtypeStruct((N_TOK, ZK_WIDTH), F32)],
        scratch_shapes=[pltpu.VMEM((IN_TM, D_MODEL), BF16), pltpu.VMEM((ZK_WIDTH, D_MODEL), F32)],
        compiler_params=_cparams("parallel", "arbitrary"),
        name="in_proj_small",
    )(x_c, x_l, g, sc_t, sh_t, w_in_t, w_in_t, w_in_t)
    return zb, ckv_raw, zk


QKV_TM = 512
ATTN_SCALE = 1.0 / math.sqrt(NOPE_DIM + ROPE_DIM)


def _qproj_body(cq_ref, g_ref, w_ref, c_ref, s_ref, o_ref):
    qn = _rms(cq_ref[...].astype(F32), g_ref[...]).astype(BF16)
    q = jnp.dot(qn, w_ref[...], preferred_element_type=F32)
    c = c_ref[...]
    s = s_ref[...]
    for h in range(MLA_HEADS):
        lo = h * QK_HEAD
        r = q[:, lo + LANES:lo + QK_HEAD]
        o_ref[:, lo:lo + LANES] = (q[:, lo:lo + LANES] * ATTN_SCALE).astype(BF16)
        o_ref[:, lo + LANES:lo + QK_HEAD] = ((r * c + _swap16(r) * s) * ATTN_SCALE).astype(BF16)


def _qproj(zb, g_qn, w_uq_p, rope_c, rope_s):
    return pl.pallas_call(
        _qproj_body,
        grid=(N_TOK // QKV_TM,),
        in_specs=[pl.BlockSpec((QKV_TM, Q_RANK), lambda i: (i, ZB_CQ // Q_RANK)),
                  pl.BlockSpec((1, Q_RANK), lambda i: (0, 0)),
                  pl.BlockSpec((Q_RANK, MLA_HEADS * QK_HEAD), lambda i: (0, 0)),
                  pl.BlockSpec((QKV_TM, LANES), lambda i: (i, 0)),
                  pl.BlockSpec((QKV_TM, LANES), lambda i: (i, 0))],
        out_specs=pl.BlockSpec((QKV_TM, MLA_HEADS * QK_HEAD), lambda i: (i, 0)),
        out_shape=jax.ShapeDtypeStruct((N_TOK, MLA_HEADS * QK_HEAD), BF16),
        compiler_params=_cparams("parallel"),
        name="mla_q",
    )(zb, g_qn, w_uq_p, rope_c, rope_s)


def _kvproj_body(ckv_ref, zk_ref, g_ref, w_ref, c_ref, s_ref, ckv_o, kv_o, kr_o):
    ckv = _rms(ckv_ref[...], g_ref[...])
    ckv_o[...] = ckv
    kv_o[...] = jnp.dot(ckv.astype(BF16), w_ref[...], preferred_element_type=F32).astype(BF16)
    zk = zk_ref[...]
    is_rope = lax.broadcasted_iota(jnp.int32, zk.shape, 1) < ROPE_DIM
    kr = jnp.where(is_rope, zk, 0.0)
    rot = kr * c_ref[...] + _swap16(kr) * s_ref[...]
    kr_o[...] = jnp.where(is_rope, rot, 0.0).astype(BF16)


def _kvproj(ckv_raw, zk, g_kvn, w_ukv_p, rope_c, rope_s):
    return pl.pallas_call(
        _kvproj_body,
        grid=(N_TOK // QKV_TM,),
        in_specs=[pl.BlockSpec((QKV_TM, KV_RANK), lambda i: (i, 0)),
                  pl.BlockSpec((QKV_TM, ZK_WIDTH), lambda i: (i, 0)),
                  pl.BlockSpec((1, KV_RANK), lambda i: (0, 0)),
                  pl.BlockSpec((KV_RANK, 2 * MLA_WIDTH), lambda i: (0, 0)),
                  pl.BlockSpec((QKV_TM, LANES), lambda i: (i, 0)),
                  pl.BlockSpec((QKV_TM, LANES), lambda i: (i, 0))],
        out_specs=[pl.BlockSpec((QKV_TM, KV_RANK), lambda i: (i, 0)),
                   pl.BlockSpec((QKV_TM, 2 * MLA_WIDTH), lambda i: (i, 0)),
                   pl.BlockSpec((QKV_TM, LANES), lambda i: (i, 0))],
        out_shape=[jax.ShapeDtypeStruct((N_TOK, KV_RANK), F32),
                   jax.ShapeDtypeStruct((N_TOK, 2 * MLA_WIDTH), BF16),
                   jax.ShapeDtypeStruct((N_TOK, LANES), BF16)],
        compiler_params=_cparams("parallel"),
        name="mla_kv",
    )(ckv_raw, zk, g_kvn, w_ukv_p, rope_c, rope_s)


def _attn_body(n_parts, q_ref, *refs):
    o_ref = refs[3 * n_parts]
    for h in range(MLA_HEADS):
        qh = q_ref[:, h * QK_HEAD:(h + 1) * QK_HEAD]
        scores = []
        for p in range(n_parts):
            kn_ref, kr_ref = refs[3 * p], refs[3 * p + 1]
            kh = jnp.concatenate([kn_ref[:, h * LANES:(h + 1) * LANES], kr_ref[...]], axis=1)
            scores.append(lax.dot_general(qh, kh, (((1,), (1,)), ((), ())),
                                          preferred_element_type=F32))
        m = scores[0].max(axis=1, keepdims=True)
        for s in scores[1:]:
            m = jnp.maximum(m, s.max(axis=1, keepdims=True))
        den = 0.0
        acc = 0.0
        for p in range(n_parts):
            e = jnp.exp(scores[p] - m)
            den = den + e.sum(axis=1, keepdims=True)
            v_ref = refs[3 * p + 2]
            acc = acc + jnp.dot(e.astype(BF16), v_ref[:, h * LANES:(h + 1) * LANES],
                                preferred_element_type=F32)
        o_ref[:, h * LANES:(h + 1) * LANES] = (acc / den).astype(BF16)


def _attention(q, kv, kr, n_seq, seq_len, tq, row0, cache=None):
    nq = seq_len // tq
    q0 = row0 // tq
    s0 = row0 // seq_len
    in_specs = [pl.BlockSpec((tq, MLA_HEADS * QK_HEAD), lambda b, i: (q0 + b * nq + i, 0))]
    args = [q]
    if cache is not None:
        kv_c, kr_c, len_c = cache
        in_specs += [pl.BlockSpec((len_c, MLA_WIDTH), lambda b, i: (b, 0)),
                     pl.BlockSpec((len_c, LANES), lambda b, i: (b, 0)),
                     pl.BlockSpec((len_c, MLA_WIDTH), lambda b, i: (b, 1))]
        args += [kv_c, kr_c, kv_c]
    in_specs += [pl.BlockSpec((seq_len, MLA_WIDTH), lambda b, i: (s0 + b, 0)),
                 pl.BlockSpec((seq_len, LANES), lambda b, i: (s0 + b, 0)),
                 pl.BlockSpec((seq_len, MLA_WIDTH), lambda b, i: (s0 + b, 1))]
    args += [kv, kr, kv]
    n_parts = 1 if cache is None else 2
    return pl.pallas_call(
        functools.partial(_attn_body, n_parts),
        grid=(n_seq, nq),
        in_specs=in_specs,
        out_specs=pl.BlockSpec((tq, MLA_WIDTH), lambda b, i: (b * nq + i, 0)),
        out_shape=jax.ShapeDtypeStruct((n_seq * seq_len, MLA_WIDTH), BF16),
        compiler_params=_cparams("parallel", "arbitrary"),
        name="mla_attn_cache" if cache is not None else "mla_attn",
    )(*args)


MG_TM = 1024
MG_TN = 512
OUT_TM = 256


def _merge_body(ac_ref, al_ref, sc_ref, sl_ref, c_ref, wa_ref, ws_ref, wc_ref, ga_ref, gs_ref, gc_ref,
                o_ref):
    def branch(x, w_ref, gate_ref):
        y = jnp.dot(x, w_ref[0].astype(BF16), preferred_element_type=F32)
        return gate_ref[...].astype(F32) * y

    o_ref[...] = (branch(_ctx_or_lat(MG_TM, ac_ref, al_ref), wa_ref, ga_ref)
                  + branch(_ctx_or_lat(MG_TM, sc_ref, sl_ref), ws_ref, gs_ref)
                  + branch(c_ref[...], wc_ref, gc_ref)).astype(BF16)


def _merge(attn_c, attn_l, ssm_c, ssm_l, gmo, wa, ws, wc, zb, layer):
    g0 = ZB_GATE // MG_TN
    gstep = D_MODEL // MG_TN
    pair = _ctx_lat_specs(MG_TM, MLA_WIDTH, 2)
    wspec = pl.BlockSpec((1, MLA_WIDTH, MG_TN), lambda i, j: (layer, 0, j))
    return pl.pallas_call(
        _merge_body,
        grid=(N_TOK // MG_TM, gstep),
        in_specs=pair + pair + [pl.BlockSpec((MG_TM, GM_WIDTH), lambda i, j: (i, 0)),
                                wspec, wspec, wspec,
                                pl.BlockSpec((MG_TM, MG_TN), lambda i, j: (i, g0 + j)),
                                pl.BlockSpec((MG_TM, MG_TN), lambda i, j: (i, g0 + gstep + j)),
                                pl.BlockSpec((MG_TM, MG_TN), lambda i, j: (i, g0 + 2 * gstep + j))],
        out_specs=pl.BlockSpec((MG_TM, MG_TN), lambda i, j: (i, j)),
        out_shape=jax.ShapeDtypeStruct((N_TOK, D_MODEL), BF16),
        compiler_params=_cparams("parallel", "arbitrary"),
        name="branch_merge",
    )(attn_c, attn_l, ssm_c, ssm_l, gmo, wa, ws, wc, zb, zb, zb)


def _split_bf16(x):
    hi = x.astype(BF16)
    return hi, (x - hi.astype(F32)).astype(BF16)


def _outproj_body(m_ref, w_ref, xc_ref, xl_ref, gt_ref, g_ref, sc_ref, sh_ref, wr_ref,
                  x1_ref, h2_ref, lg_ref):
    mix = jnp.dot(m_ref[...], w_ref[...], preferred_element_type=F32)
    x1 = _ctx_or_lat(OUT_TM, xc_ref, xl_ref) + gt_ref[0] * mix
    x1_ref[...] = x1
    h2 = _rms(x1, g_ref[...]) * (1.0 + sc_ref[0]) + sh_ref[0]
    h2_ref[...] = h2.astype(BF16)
    h_hi, h_lo = _split_bf16(h2)
    w_hi, w_lo = _split_bf16(wr_ref[...])
    lg_ref[...] = (jnp.dot(h_hi, w_hi, preferred_element_type=F32)
                   + jnp.dot(h_lo, w_hi, preferred_element_type=F32)
                   + jnp.dot(h_hi, w_lo, preferred_element_type=F32))


def _outproj(merged, w_out, x_c, x_l, gt_t, g2, sc_t, sh_t, w_router_p):
    mspec = pl.BlockSpec((1, 1, D_MODEL), lambda i: (i, 0, 0))
    return pl.pallas_call(
        _outproj_body,
        grid=(N_TOK // OUT_TM,),
        in_specs=[pl.BlockSpec((OUT_TM, D_MODEL), lambda i: (i, 0)),
                  pl.BlockSpec((D_MODEL, D_MODEL), lambda i: (0, 0))]
                 + _ctx_lat_specs(OUT_TM, D_MODEL, 1) + [
                  mspec,
                  pl.BlockSpec((1, D_MODEL), lambda i: (0, 0)),
                  mspec, mspec,
                  pl.BlockSpec((D_MODEL, LANES), lambda i: (0, 0))],
        out_specs=[pl.BlockSpec((OUT_TM, D_MODEL), lambda i: (i, 0)),
                   pl.BlockSpec((OUT_TM, D_MODEL), lambda i: (i, 0)),
                   pl.BlockSpec((OUT_TM, LANES), lambda i: (i, 0))],
        out_shape=[jax.ShapeDtypeStruct((N_TOK, D_MODEL), F32),
                   jax.ShapeDtypeStruct((N_TOK, D_MODEL), BF16),
                   jax.ShapeDtypeStruct((N_TOK, LANES), F32)],
        compiler_params=_cparams("parallel"),
        name="out_proj",
    )(merged, w_out, x_c, x_l, gt_t, g2, sc_t, sh_t, w_router_p)


PREFIX_BLK = 256
EXP_SEARCH_STEPS = 7
BISECT_STEPS = 40


def _prefix_count(mask):
    n = mask.shape[1]
    upper = (lax.broadcasted_iota(jnp.int32, (PREFIX_BLK, PREFIX_BLK), 0)
             < lax.broadcasted_iota(jnp.int32, (PREFIX_BLK, PREFIX_BLK), 1)).astype(BF16)
    run = jnp.zeros((mask.shape[0], 1), F32)
    outs = []
    for k in range(0, n, PREFIX_BLK):
        blk = mask[:, k:k + PREFIX_BLK]
        outs.append(jnp.dot(blk.astype(BF16), upper, preferred_element_type=F32) + run)
        run = run + jnp.sum(blk, axis=1, keepdims=True)
    return outs[0] if len(outs) == 1 else jnp.concatenate(outs, axis=1)


def _select_body(n_seq, seq_len, cap, lg_ref, slot_t_ref, slot_ref, aff_ref):
    lane = lax.broadcasted_iota(jnp.int32, (seq_len, LANES), 1)
    rows = []
    for s in range(n_seq):
        lg = jnp.where(lane < N_EXPERTS, lg_ref[s * seq_len:(s + 1) * seq_len, :], -jnp.inf)
        e = jnp.exp(lg - lg.max(axis=1, keepdims=True))
        aff = e / e.sum(axis=1, keepdims=True)
        rows.append(aff.T[:N_EXPERTS, :])
    a = rows[0] if n_seq == 1 else jnp.concatenate(rows, axis=0)

    def count_ge(t):
        return jnp.sum(jnp.where(a >= t, 1.0, 0.0), axis=1, keepdims=True)

    hi = jnp.full((a.shape[0], 1), 2.0, F32)
    for i in reversed(range(EXP_SEARCH_STEPS)):
        cand = hi * (2.0 ** -(2 ** i))
        hi = jnp.where(count_ge(cand) < cap, cand, hi)
    lo = jnp.where(hi <= 2.0 ** -126, 0.0, 0.5 * hi)
    for _ in range(BISECT_STEPS):
        mid = 0.5 * (lo + hi)
        ge = count_ge(mid) >= cap
        lo = jnp.where(ge, mid, lo)
        hi = jnp.where(ge, hi, mid)
    above = jnp.where(a >= hi, 1.0, 0.0)
    tied = jnp.where((a >= lo) & (a < hi), 1.0, 0.0)
    need = cap - jnp.sum(above, axis=1, keepdims=True)
    sel = above + tied * jnp.where(_prefix_count(tied) < need, 1.0, 0.0)
    slot = jnp.where(sel > 0.0, _prefix_count(sel), -1.0)
    slot_ref[...] = slot
    aff_ref[...] = a
    pad = jnp.full((LANES - N_EXPERTS, seq_len), -1.0, F32)
    for s in range(n_seq):
        blk = jnp.concatenate([slot[s * N_EXPERTS:(s + 1) * N_EXPERTS, :], pad], axis=0)
        slot_t_ref[s * seq_len:(s + 1) * seq_len, :] = blk.T.astype(jnp.int32)


def _select(logits, group, n_seq, seq_len, cap):
    rows = n_seq * N_EXPERTS
    n_rows = n_seq * seq_len
    return pl.pallas_call(
        functools.partial(_select_body, n_seq, seq_len, cap),
        grid=(1,),
        in_specs=[pl.BlockSpec((n_rows, LANES), lambda i: (group, 0))],
        out_specs=[pl.BlockSpec((n_rows, LANES), lambda i: (0, 0)),
                   pl.BlockSpec((rows, seq_len), lambda i: (0, 0)),
                   pl.BlockSpec((rows, seq_len), lambda i: (0, 0))],
        out_shape=[jax.ShapeDtypeStruct((n_rows, LANES), jnp.int32),
                   jax.ShapeDtypeStruct((rows, seq_len), F32),
                   jax.ShapeDtypeStruct((rows, seq_len), F32)],
        compiler_params=_cparams("arbitrary"),
        name="moe_select",
    )(logits)


def _gather_body(cap, epb, slot_ref, aff_ref, h_ref, xe_ref, ge_ref):
    n = h_ref.shape[0]
    if cap < LANES and epb == N_EXPERTS:
        assert cap & (cap - 1) == 0
        width = N_EXPERTS * cap
        row_expert = lax.shift_right_logical(lax.broadcasted_iota(jnp.int32, (width, N_EXPERTS), 0),
                                             cap.bit_length() - 1)
        spread = (lax.broadcasted_iota(jnp.int32, (width, N_EXPERTS), 1) == row_expert).astype(BF16)
        slot_x = jnp.dot(spread, slot_ref[...].astype(BF16), preferred_element_type=F32)
        target = (lax.broadcasted_iota(jnp.int32, (width, n), 0) & (cap - 1)).astype(F32)
        onehot = slot_x == target
        xe = jnp.dot(onehot.astype(BF16), h_ref[...], preferred_element_type=F32).astype(BF16)
        xe_ref[...] = xe.reshape(N_EXPERTS, cap, xe.shape[1])
        aff_x = sum(jnp.dot(spread, part, preferred_element_type=F32) for part in _split3(aff_ref[...]))
        gate = jnp.sum(jnp.where(onehot, aff_x, 0.0), axis=1, keepdims=True)
        ge_ref[...] = jnp.broadcast_to(gate, (width, LANES)).reshape(N_EXPERTS, cap, LANES)
        return
    step = pl.program_id(1)
    row = lax.broadcasted_iota(jnp.int32, (cap, n), 0).astype(F32)
    for k in range(epb):
        x = step * epb + k
        onehot = slot_ref[pl.ds(x, 1), :] == row
        xe_ref[k] = jnp.dot(onehot.astype(BF16), h_ref[...], preferred_element_type=F32).astype(BF16)
        gate = jnp.sum(jnp.where(onehot, aff_ref[pl.ds(x, 1), :], 0.0), axis=1, keepdims=True)
        ge_ref[k] = jnp.broadcast_to(gate, (cap, LANES))


def _gather(slot, aff, h2, n_seq, seq_len, cap, row0, epb):
    s0 = row0 // seq_len
    return pl.pallas_call(
        functools.partial(_gather_body, cap, epb),
        grid=(n_seq, N_EXPERTS // epb),
        in_specs=[pl.BlockSpec((N_EXPERTS, seq_len), lambda b, x: (b, 0)),
                  pl.BlockSpec((N_EXPERTS, seq_len), lambda b, x: (b, 0)),
                  pl.BlockSpec((seq_len, D_MODEL), lambda b, x: (s0 + b, 0))],
        out_specs=[pl.BlockSpec((epb, cap, D_MODEL), lambda b, x: (x, b, 0)),
                   pl.BlockSpec((epb, cap, LANES), lambda b, x: (x, b, 0))],
        out_shape=[jax.ShapeDtypeStruct((N_EXPERTS, n_seq * cap, D_MODEL), BF16),
                   jax.ShapeDtypeStruct((N_EXPERTS, n_seq * cap, LANES), F32)],
        compiler_params=_cparams("parallel", "arbitrary"),
        name="moe_gather",
    )(slot, aff, h2)


FFN_TF = 256


FFN_DOWN_TN = 512


def _ffn_body(xc_ref, xl_ref, gc_ref, gl_ref, wg_ref, wu_ref, wd_ref, yc_ref, yl_ref, hid_scr):
    f = pl.program_id(1)
    nf = EXPERT_FF // FFN_TF
    x = jnp.concatenate([xc_ref[0], xl_ref[0]], axis=0)
    gate = jnp.dot(x, wg_ref[0, 0].astype(BF16), preferred_element_type=F32)
    up = jnp.dot(x, wu_ref[0, 0].astype(BF16), preferred_element_type=F32)
    hid_scr[f] = (gate * jax.nn.sigmoid(gate) * up).astype(BF16)

    @pl.when(f == nf - 1)
    def _():
        hid = jnp.concatenate([hid_scr[k] for k in range(nf)], axis=1)
        g_c = gc_ref[0][:, :1]
        g_l = gl_ref[0][:, :1]
        for c in range(0, D_MODEL, FFN_DOWN_TN):
            out = jnp.dot(hid, wd_ref[0, 0, :, c:c + FFN_DOWN_TN].astype(BF16), preferred_element_type=F32)
            yc_ref[0, :, c:c + FFN_DOWN_TN] = (out[:ROWS_CTX, :] * g_c).astype(BF16)
            yl_ref[0, :, c:c + FFN_DOWN_TN] = (out[ROWS_CTX:, :] * g_l).astype(BF16)


def _expert_ffn(xe_c, xe_l, ge_c, ge_l, w_gate, w_up, w_down, layer):
    def xspec(rows, width):
        return pl.BlockSpec((1, rows, width), lambda i, f: (i, 0, 0))

    return pl.pallas_call(
        _ffn_body,
        grid=(N_EXPERTS, EXPERT_FF // FFN_TF),
        in_specs=[xspec(ROWS_CTX, D_MODEL), xspec(ROWS_LAT, D_MODEL),
                  xspec(ROWS_CTX, LANES), xspec(ROWS_LAT, LANES),
                  pl.BlockSpec((1, 1, D_MODEL, FFN_TF), lambda i, f: (layer, i, 0, f)),
                  pl.BlockSpec((1, 1, D_MODEL, FFN_TF), lambda i, f: (layer, i, 0, f)),
                  pl.BlockSpec((1, 1, EXPERT_FF, D_MODEL), lambda i, f: (layer, i, 0, 0))],
        out_specs=[xspec(ROWS_CTX, D_MODEL), xspec(ROWS_LAT, D_MODEL)],
        out_shape=[jax.ShapeDtypeStruct((N_EXPERTS, ROWS_CTX, D_MODEL), BF16),
                   jax.ShapeDtypeStruct((N_EXPERTS, ROWS_LAT, D_MODEL), BF16)],
        scratch_shapes=[pltpu.VMEM((EXPERT_FF // FFN_TF, ROWS_CTX + ROWS_LAT, FFN_TF), BF16)],
        compiler_params=_cparams("parallel", "arbitrary"),
        name="expert_ffn",
    )(xe_c, xe_l, ge_c, ge_l, w_gate, w_up, w_down)


CMB_TN = 512


def _combine_body(cap, slot_ref, y_ref, x_ref, gt_ref, o_ref, acc):
    n = slot_ref.shape[0]
    slot = slot_ref[...]
    if cap < LANES:
        assert cap & (cap - 1) == 0
        width = N_EXPERTS * cap
        lane_expert = lax.shift_right_logical(lax.broadcasted_iota(jnp.int32, (LANES, width), 1),
                                              cap.bit_length() - 1)
        spread = (lax.broadcasted_iota(jnp.int32, (LANES, width), 0) == lane_expert).astype(BF16)
        slot_x = jnp.dot(slot.astype(F32).astype(BF16), spread, preferred_element_type=F32)
        target = (lax.broadcasted_iota(jnp.int32, (n, width), 1) & (cap - 1)).astype(F32)
        onehot = (slot_x == target).astype(BF16)
        acc[...] = jnp.dot(onehot, y_ref[...].reshape(width, y_ref.shape[2]), preferred_element_type=F32)
    else:
        col = lax.broadcasted_iota(jnp.int32, (n, cap), 1)
        for x in range(N_EXPERTS):
            onehot = (slot[:, x:x + 1] == col).astype(BF16)
            part = jnp.dot(onehot, y_ref[x], preferred_element_type=F32)
            if x == 0:
                acc[...] = part
            else:
                acc[...] += part
    o_ref[...] = x_ref[...] + gt_ref[0] * acc[...]


def _combine(slot_t, y, x1, gt_t, n_seq, seq_len, cap, row0, tn):
    s0 = row0 // seq_len
    return pl.pallas_call(
        functools.partial(_combine_body, cap),
        grid=(n_seq, D_MODEL // tn),
        in_specs=[pl.BlockSpec((seq_len, LANES), lambda b, j: (b, 0)),
                  pl.BlockSpec((N_EXPERTS, cap, tn), lambda b, j: (0, b, j)),
                  pl.BlockSpec((seq_len, tn), lambda b, j: (s0 + b, j)),
                  pl.BlockSpec((1, 1, tn), lambda b, j: (b, 0, j))],
        out_specs=pl.BlockSpec((seq_len, tn), lambda b, j: (b, j)),
        out_shape=jax.ShapeDtypeStruct((n_seq * seq_len, D_MODEL), F32),
        scratch_shapes=[pltpu.VMEM((seq_len, tn), F32)],
        compiler_params=_cparams("parallel", "arbitrary"),
        name="moe_combine",
    )(slot_t, y, x1, gt_t)


CONV_TN = 256
CONV_HALO = 8
DT_LANE = ROPE_DIM


def _conv_body(x_ref, w_ref, b_ref, o_ref):
    seq_len = x_ref.shape[0]
    halo = jnp.zeros((CONV_HALO, x_ref.shape[1]), F32)
    ext = jnp.concatenate([halo, x_ref[...].astype(F32), halo], axis=0)
    w = w_ref[...]
    y = b_ref[...]
    for k in range(SSM_CONV):
        lo = CONV_HALO - SSM_CONV // 2 + k
        y = y + w[k:k + 1, :] * ext[lo:lo + seq_len, :]
    o_ref[...] = (y * jax.nn.sigmoid(y)).astype(BF16)


def _conv_silu(zb, conv_w8, conv_b, n_seq, seq_len, row0, tn):
    s0 = row0 // seq_len
    c0 = ZB_XBC // tn
    return pl.pallas_call(
        _conv_body,
        grid=(n_seq, SSM_XBC // tn),
        in_specs=[pl.BlockSpec((seq_len, tn), lambda b, j: (s0 + b, c0 + j)),
                  pl.BlockSpec((8, tn), lambda b, j: (0, j)),
                  pl.BlockSpec((1, tn), lambda b, j: (0, j))],
        out_specs=pl.BlockSpec((seq_len, tn), lambda b, j: (b, j)),
        out_shape=jax.ShapeDtypeStruct((n_seq * seq_len, SSM_XBC), BF16),
        compiler_params=_cparams("parallel", "arbitrary"),
        name="ssm_conv",
    )(zb, conv_w8, conv_b)


def _split3(x):
    hi = x.astype(BF16)
    r = x - hi.astype(F32)
    mid = r.astype(BF16)
    return hi, mid, (r - mid.astype(F32)).astype(BF16)


def _ssd_body(nc, xa_ref, zk_ref, z_ref, h0_ref, bias_ref, a_ref, d_ref, g_ref, ef_ref, eb_ref,
              o_ref, st_ref, hb_in, hf_cur, hb_cur):
    q = SSM_CHUNK
    half = SSM_INNER // SSM_GROUPS
    ii = lax.broadcasted_iota(jnp.int32, (q, q), 0)
    jj = lax.broadcasted_iota(jnp.int32, (q, q), 1)
    lower = ii >= jj
    upper = ii <= jj
    lower_b = lower.astype(BF16)
    upper_b = upper.astype(BF16)
    lane = lax.broadcasted_iota(jnp.int32, (q, LANES), 1)
    is_dt = (lane >= DT_LANE) & (lane < DT_LANE + 2 * SSM_HEADS)
    is_fwd = lane < DT_LANE + SSM_HEADS

    def tri_cumsum(tri, v):
        hi, mid, lo = _split3(v)
        return (jnp.dot(tri, hi, preferred_element_type=F32) + jnp.dot(tri, mid, preferred_element_type=F32)
                + jnp.dot(tri, lo, preferred_element_type=F32))

    def expand(v, e_ref):
        hi, lo = _split_bf16(v)
        return (jnp.dot(hi, e_ref[...], preferred_element_type=F32)
                + jnp.dot(lo, e_ref[...], preferred_element_type=F32))

    def chunk_factors(r0):
        dt = jnp.where(is_dt, jax.nn.softplus(zk_ref[pl.ds(r0, q), :] + bias_ref[...]), 0.0)
        dta = dt * a_ref[...]
        cum = jnp.where(is_fwd, tri_cumsum(lower_b, dta), tri_cumsum(upper_b, dta))
        tot = jnp.sum(dta, axis=0, keepdims=True)
        return dt, cum, tot

    def state_update(r0, dend_x, cdec_x, h_prev):
        x = xa_ref[pl.ds(r0, q), 0:SSM_INNER].astype(F32)
        xs = (x * dend_x).astype(BF16)
        parts = []
        for g in range(SSM_GROUPS):
            lo = SSM_INNER + g * SSM_STATE
            b_t = xa_ref[pl.ds(r0, q), lo:lo + SSM_STATE].astype(F32).T.astype(BF16)
            parts.append(jnp.dot(b_t, xs[:, g * half:(g + 1) * half], preferred_element_type=F32))
        return cdec_x * h_prev + jnp.concatenate(parts, axis=1)

    hf_cur[...] = h0_ref[0, 0]
    hb_cur[...] = h0_ref[0, 1]

    def bwd_step(t, carry):
        c = nc - 1 - t
        r0 = pl.multiple_of(c * q, q)
        dt, cum, tot = chunk_factors(r0)
        hb_in[c] = hb_cur[...]
        dend_x = expand(jnp.exp(tot - cum) * dt, eb_ref)
        cdec_x = expand(jnp.broadcast_to(jnp.exp(tot), (8, LANES)), eb_ref)[0:1]
        hb_cur[...] = state_update(r0, dend_x, cdec_x, hb_cur[...])
        return carry

    lax.fori_loop(0, nc, bwd_step, 0)

    def fwd_step(c, carry):
        r0 = pl.multiple_of(c * q, q)
        dt, cum, tot = chunk_factors(r0)
        cum_t = cum.T
        dt_t = dt.T
        eoff = jnp.exp(cum)
        x_bf = xa_ref[pl.ds(r0, q), 0:SSM_INNER]
        cb = []
        c_bf = []
        for g in range(SSM_GROUPS):
            lo_b = SSM_INNER + g * SSM_STATE
            lo_c = SSM_INNER + (SSM_GROUPS + g) * SSM_STATE
            c_g = xa_ref[pl.ds(r0, q), lo_c:lo_c + SSM_STATE]
            b_g = xa_ref[pl.ds(r0, q), lo_b:lo_b + SSM_STATE]
            c_bf.append(c_g)
            cb.append(lax.dot_general(c_g, b_g, (((1,), (1,)), ((), ())), preferred_element_type=F32))

        def head_matrix(h):
            f = DT_LANE + h
            b = DT_LANE + SSM_HEADS + h
            lf = jnp.where(lower, jnp.exp(cum[:, f:f + 1] - cum_t[f:f + 1, :]), 0.0) * dt_t[f:f + 1, :]
            lb = jnp.where(upper, jnp.exp(cum[:, b:b + 1] - cum_t[b:b + 1, :]), 0.0) * dt_t[b:b + 1, :]
            return (cb[h // (SSM_HEADS // SSM_GROUPS)] * (lf + lb)).astype(BF16)

        lane_lo = lane < SSM_HEADDIM
        pairs = []
        for hp in range(SSM_HEADS // 2):
            x_pair = x_bf[:, hp * LANES:(hp + 1) * LANES]
            y0 = jnp.dot(head_matrix(2 * hp), x_pair, preferred_element_type=F32)
            y1 = jnp.dot(head_matrix(2 * hp + 1), x_pair, preferred_element_type=F32)
            pairs.append(jnp.where(lane_lo, y0, y1))
        y = jnp.concatenate(pairs, axis=1)

        def off_diag(h_t, factor_x):
            h_bf = h_t.astype(BF16)
            parts = [jnp.dot(c_bf[g], h_bf[:, g * half:(g + 1) * half], preferred_element_type=F32)
                     for g in range(SSM_GROUPS)]
            return jnp.concatenate(parts, axis=1) * factor_x

        y = y + off_diag(hf_cur[...], expand(eoff, ef_ref)) + off_diag(hb_in[c], expand(eoff, eb_ref))
        y = y + d_ref[...] * x_bf.astype(F32)
        zg = z_ref[pl.ds(r0, q), :].astype(F32)
        o_ref[pl.ds(r0, q), :] = _rms(y * (zg * jax.nn.sigmoid(zg)), g_ref[...]).astype(BF16)

        dend_x = expand(jnp.exp(tot - cum) * dt, ef_ref)
        cdec_x = expand(jnp.broadcast_to(jnp.exp(tot), (8, LANES)), ef_ref)[0:1]
        hf_cur[...] = state_update(r0, dend_x, cdec_x, hf_cur[...])
        return carry

    lax.fori_loop(0, nc, fwd_step, 0)
    st_ref[0, 0] = hf_cur[...]
    st_ref[0, 1] = hb_cur[...]


def _ssd(xa, zk, zb, h0_t, dt_bias_p, a_p, d_x, g_ssm, e_f, e_b, n_seq, seq_len, row0):
    s0 = row0 // seq_len
    nc = seq_len // SSM_CHUNK
    vec = lambda w: pl.BlockSpec((1, w), lambda b: (0, 0))
    return pl.pallas_call(
        functools.partial(_ssd_body, nc),
        grid=(n_seq,),
        in_specs=[pl.BlockSpec((seq_len, SSM_XBC), lambda b: (b, 0)),
                  pl.BlockSpec((seq_len, ZK_WIDTH), lambda b: (s0 + b, 0)),
                  pl.BlockSpec((seq_len, SSM_INNER), lambda b: (s0 + b, ZB_Z // SSM_INNER)),
                  pl.BlockSpec((1, 2, SSM_STATE, SSM_INNER), lambda b: (b, 0, 0, 0)),
                  vec(LANES), vec(LANES), vec(SSM_INNER), vec(SSM_INNER),
                  pl.BlockSpec((LANES, SSM_INNER), lambda b: (0, 0)),
                  pl.BlockSpec((LANES, SSM_INNER), lambda b: (0, 0))],
        out_specs=[pl.BlockSpec((seq_len, SSM_INNER), lambda b: (b, 0)),
                   pl.BlockSpec((1, 2, SSM_STATE, SSM_INNER), lambda b: (b, 0, 0, 0))],
        out_shape=[jax.ShapeDtypeStruct((n_seq * seq_len, SSM_INNER), BF16),
                   jax.ShapeDtypeStruct((n_seq, 2, SSM_STATE, SSM_INNER), F32)],
        scratch_shapes=[pltpu.VMEM((nc, SSM_STATE, SSM_INNER), F32),
                        pltpu.VMEM((SSM_STATE, SSM_INNER), F32),
                        pltpu.VMEM((SSM_STATE, SSM_INNER), F32)],
        compiler_params=_cparams("parallel"),
        name="ssd",
    )(xa, zk, zb, h0_t, dt_bias_p, a_p, d_x, g_ssm, e_f, e_b)


GM_TM = 512
GM_GROUP_W = GM_WIDTH // GM_GROUPS


def _gmlp_body(u_ref, v_ref, g_ref, w_ref, b_ref, o_ref):
    for r in range(0, GM_TM, GM_CHUNK):
        u = jax.nn.gelu(u_ref[r:r + GM_CHUNK, :].astype(F32))
        vg = _rms(jax.nn.gelu(v_ref[r:r + GM_CHUNK, :].astype(F32)), g_ref[...]).astype(BF16)
        sv = jnp.concatenate(
            [jnp.dot(w_ref[k], vg[:, k * GM_GROUP_W:(k + 1) * GM_GROUP_W], preferred_element_type=F32)
             for k in range(GM_GROUPS)], axis=1)
        o_ref[r:r + GM_CHUNK, :] = (u * (sv + b_ref[...])).astype(BF16)


def _gmlp(zb, g_gv, w_sp, b_x):
    return pl.pallas_call(
        _gmlp_body,
        grid=(N_TOK // GM_TM,),
        in_specs=[pl.BlockSpec((GM_TM, GM_WIDTH), lambda i: (i, ZB_GM // GM_WIDTH)),
                  pl.BlockSpec((GM_TM, GM_WIDTH), lambda i: (i, ZB_GM // GM_WIDTH + 1)),
                  pl.BlockSpec((1, GM_WIDTH), lambda i: (0, 0)),
                  pl.BlockSpec((GM_GROUPS, GM_CHUNK, GM_CHUNK), lambda i: (0, 0, 0)),
                  pl.BlockSpec((GM_CHUNK, GM_WIDTH), lambda i: (0, 0))],
        out_specs=pl.BlockSpec((GM_TM, GM_WIDTH), lambda i: (i, 0)),
        out_shape=jax.ShapeDtypeStruct((N_TOK, GM_WIDTH), BF16),
        compiler_params=_cparams("parallel"),
        name="gmlp",
    )(zb, zb, g_gv, w_sp, b_x)


FN_TM = 512


def _final_norm_body(x_ref, g_ref, o_ref):
    o_ref[...] = _rms(x_ref[...], g_ref[...])


def _final_norm(x, g):
    rows = x.shape[0]
    return pl.pallas_call(
        _final_norm_body,
        grid=(rows // FN_TM,),
        in_specs=[pl.BlockSpec((FN_TM, D_MODEL), lambda i: (i, 0)),
                  pl.BlockSpec((1, D_MODEL), lambda i: (0, 0))],
        out_specs=pl.BlockSpec((FN_TM, D_MODEL), lambda i: (i, 0)),
        out_shape=jax.ShapeDtypeStruct((rows, D_MODEL), F32),
        compiler_params=_cparams("parallel"),
        name="final_norm",
    )(x, g)


def _dt_lanes(v):
    return jnp.pad(v.reshape(1, 2 * SSM_HEADS).astype(F32),
                   ((0, 0), (DT_LANE, LANES - DT_LANE - 2 * SSM_HEADS)))


def _head_expanders():
    lane = lax.broadcasted_iota(jnp.int32, (LANES, SSM_INNER), 0)
    head = lax.broadcasted_iota(jnp.int32, (LANES, SSM_INNER), 1) // SSM_HEADDIM
    e_f = (lane == head + DT_LANE).astype(BF16)
    e_b = (lane == head + DT_LANE + SSM_HEADS).astype(BF16)
    return e_f, e_b


MOD_TN = 1024
MOD_ROWS = 8
N_COND = 1 + DEC_BATCH


def _mod_body(ct_ref, w_ref, b_ref, o_ref):
    c = ct_ref[...]
    act = c * jax.nn.sigmoid(c)
    sub = lax.broadcasted_iota(jnp.int32, (MOD_ROWS, MOD_TN), 0)
    out = jnp.zeros((MOD_ROWS, MOD_TN), F32)
    for r in range(N_COND):
        y = jnp.sum(act[:, r:r + 1] * w_ref[0], axis=0, keepdims=True)
        out = jnp.where(sub == r, y, out)
    o_ref[0] = out + b_ref[0]


def _modulation(cond_t, w_mod, b_mod):
    n = 6 * D_MODEL
    return pl.pallas_call(
        _mod_body,
        grid=(DEPTH, n // MOD_TN),
        in_specs=[pl.BlockSpec((D_MODEL, MOD_ROWS), lambda l, j: (0, 0)),
                  pl.BlockSpec((1, D_MODEL, MOD_TN), lambda l, j: (l, 0, j)),
                  pl.BlockSpec((1, 1, MOD_TN), lambda l, j: (l, 0, j))],
        out_specs=pl.BlockSpec((1, MOD_ROWS, MOD_TN), lambda l, j: (l, 0, j)),
        out_shape=jax.ShapeDtypeStruct((DEPTH, MOD_ROWS, n), F32),
        compiler_params=_cparams("parallel", "arbitrary"),
        name="modulation",
    )(cond_t, w_mod, b_mod[:, None, :])


def _prep_w_uq(w):
    w = w.reshape(Q_RANK, MLA_HEADS, NOPE_DIM + ROPE_DIM)
    pad = jnp.zeros((Q_RANK, MLA_HEADS, QK_HEAD - NOPE_DIM - ROPE_DIM), w.dtype)
    return jnp.concatenate([w, pad], axis=2).reshape(Q_RANK, MLA_HEADS * QK_HEAD).astype(BF16)


def _prep_w_ukv(w):
    w = w.reshape(KV_RANK, MLA_HEADS, NOPE_DIM + V_DIM)
    return jnp.concatenate([w[:, :, :NOPE_DIM].reshape(KV_RANK, MLA_WIDTH),
                            w[:, :, NOPE_DIM:].reshape(KV_RANK, MLA_WIDTH)], axis=1).astype(BF16)


def _rope_tables(n_lat):
    rows = n_lat // GRID_W
    row = jnp.repeat(jnp.arange(rows), GRID_W).astype(F32)
    col = jnp.tile(jnp.arange(GRID_W), rows).astype(F32)
    nf = ROPE_DIM // 4
    freqs = jnp.power(ROPE_THETA, -jnp.arange(nf, dtype=F32) / nf)
    ang = jnp.stack([row[:, None] * freqs, col[:, None] * freqs], axis=1)
    cos, sin = jnp.cos(ang), jnp.sin(ang)
    c64 = jnp.concatenate([cos[:, 0], cos[:, 0], cos[:, 1], cos[:, 1]], axis=1)
    s64 = jnp.concatenate([-sin[:, 0], sin[:, 0], -sin[:, 1], sin[:, 1]], axis=1)
    c_lat = jnp.concatenate([c64, jnp.ones((n_lat, LANES - ROPE_DIM), F32)], axis=1)
    s_lat = jnp.concatenate([s64, jnp.zeros((n_lat, LANES - ROPE_DIM), F32)], axis=1)
    rope_c = jnp.concatenate([jnp.ones((N_CTX, LANES), F32), jnp.tile(c_lat, (DEC_BATCH, 1))], axis=0)
    rope_s = jnp.concatenate([jnp.zeros((N_CTX, LANES), F32), jnp.tile(s_lat, (DEC_BATCH, 1))], axis=0)
    return rope_c, rope_s


def _rows_mod(vec3, tm):
    idx = [0] * (N_CTX // tm) + [1] * (DEC_SEQ // tm) + [2] * (DEC_SEQ // tm)
    return vec3[jnp.array(idx)][:, None, :]


def kernel(x_prompt, x_sample, c, cache_ckv, cache_krope, state_ssm, c_ctx, w_mod, b_mod,
           g_norm1, g_norm2, w_in, g_qn, w_uq, g_kvn, w_ukv, conv_w, conv_b, dt_bias, a_log,
           d_skip, g_ssm, g_gv, w_sp, b_sp, w_br_attn, w_br_ssm, w_br_gmlp, w_out, w_router,
           w_gate, w_up, w_down, g_final):
    rope_c, rope_s = _rope_tables(DEC_SEQ)
    x_c = x_prompt.reshape(N_CTX, D_MODEL)
    x_l = x_sample.reshape(N_LAT, D_MODEL)
    cond_t = jnp.pad(jnp.concatenate([c_ctx[None, :], c], axis=0).T, ((0, 0), (0, MOD_ROWS - N_COND)))
    mod_all = _modulation(cond_t, w_mod, b_mod)
    e_f, e_b = _head_expanders()
    w_in_t = jnp.swapaxes(w_in, 1, 2)
    h0_c = jnp.zeros((BATCH, 2, SSM_STATE, SSM_INNER), F32)

    ckvs, krs, sts = [], [], []
    for l in range(DEPTH):
        mod = mod_all[l, :1 + DEC_BATCH].reshape(1 + DEC_BATCH, 6, D_MODEL)
        sh1, sc1, gt1, sh2, sc2, gt2 = [mod[:, k] for k in range(6)]

        zb, ckv_raw, zk = _inproj(x_c, x_l, g_norm1[l][None, :], _rows_mod(sc1, IN_TM), _rows_mod(sh1, IN_TM),
                                  w_in_t, l)

        q = _qproj(zb, g_qn[l][None, :], _prep_w_uq(w_uq[l]), rope_c, rope_s)
        w_ukv_p = _prep_w_ukv(w_ukv[l])
        ckv, kv, krot = _kvproj(ckv_raw, zk, g_kvn[l][None, :], w_ukv_p, rope_c, rope_s)
        kv_cache = _matmul(cache_ckv[:, l].reshape(DEC_BATCH * PAST_LEN, KV_RANK), w_ukv_p,
                           out_dtype=BF16, tn=2 * MLA_WIDTH, name="mla_kv_cache")
        kr_cache = jnp.pad(cache_krope[:, l].reshape(DEC_BATCH * PAST_LEN, ROPE_DIM),
                           ((0, 0), (0, LANES - ROPE_DIM))).astype(BF16)
        attn_c = _attention(q, kv, krot, BATCH, SEQ, SEQ, 0)
        attn_l = _attention(q, kv, krot, DEC_BATCH, DEC_SEQ, 256, N_CTX, cache=(kv_cache, kr_cache, PAST_LEN))

        conv_w8 = jnp.pad(conv_w[l], ((0, 8 - SSM_CONV), (0, 0)))
        ssd_par = (_dt_lanes(dt_bias[l]), _dt_lanes(-jnp.exp(a_log[l])),
                   jnp.repeat(d_skip[l], SSM_HEADDIM)[None, :], g_ssm[l][None, :], e_f, e_b)
        xa_c = _conv_silu(zb, conv_w8, conv_b[l][None, :], BATCH, SEQ, 0, SSM_XBC)
        xa_l = _conv_silu(zb, conv_w8, conv_b[l][None, :], DEC_BATCH, DEC_SEQ, N_CTX, CONV_TN)
        h0_l = jnp.transpose(state_ssm[:, l], (0, 1, 4, 2, 3)).reshape(DEC_BATCH, 2, SSM_STATE, SSM_INNER)
        ssm_c, st_c = _ssd(xa_c, zk, zb, h0_c, *ssd_par, BATCH, SEQ, 0)
        ssm_l, _ = _ssd(xa_l, zk, zb, h0_l, *ssd_par, DEC_BATCH, DEC_SEQ, N_CTX)
        gmo = _gmlp(zb, g_gv[l][None, :], w_sp[l].astype(BF16), jnp.repeat(b_sp[l].T, GM_GROUP_W, axis=1))
        st_c = jnp.transpose(st_c.reshape(BATCH, 2, SSM_STATE, SSM_HEADS, SSM_HEADDIM), (0, 1, 3, 4, 2))

        merged = _merge(attn_c, attn_l, ssm_c, ssm_l, gmo, w_br_attn, w_br_ssm, w_br_gmlp, zb, l)
        w_router_p = jnp.pad(w_router[l], ((0, 0), (0, LANES - N_EXPERTS)))
        x1, h2, logits = _outproj(merged, w_out[l].astype(BF16), x_c, x_l, _rows_mod(gt1, OUT_TM),
                                  g_norm2[l][None, :], _rows_mod(sc2, OUT_TM), _rows_mod(sh2, OUT_TM),
                                  w_router_p)

        slot_c, srow_c, arow_c = _select(logits, 0, BATCH, SEQ, CAP_CTX)
        slot_l, srow_l, arow_l = _select(logits, 1, DEC_BATCH, DEC_SEQ, CAP_LAT)
        xe_c, ge_c = _gather(srow_c, arow_c, h2, BATCH, SEQ, CAP_CTX, 0, N_EXPERTS)
        xe_l, ge_l = _gather(srow_l, arow_l, h2, DEC_BATCH, DEC_SEQ, CAP_LAT, N_CTX, 2)
        y_c, y_l = _expert_ffn(xe_c, xe_l, ge_c, ge_l, w_gate, w_up, w_down, l)
        gt2_c = jnp.broadcast_to(gt2[0][None, None, :], (BATCH, 1, D_MODEL))
        gt2_l = gt2[1:][:, None, :]
        x_c = _combine(slot_c, y_c, x1, gt2_c, BATCH, SEQ, CAP_CTX, 0, D_MODEL)
        x_l = _combine(slot_l, y_l, x1, gt2_l, DEC_BATCH, DEC_SEQ, CAP_LAT, N_CTX, CMB_TN)

        ckvs.append(ckv[:N_CTX].reshape(BATCH, SEQ, KV_RANK))
        krs.append(zk[:N_CTX, :ROPE_DIM].reshape(BATCH, SEQ, ROPE_DIM))
        sts.append(st_c)

    y_prompt = _final_norm(x_c, g_final[None, :]).reshape(BATCH, SEQ, D_MODEL)
    y_sample = _final_norm(x_l, g_final[None, :]).reshape(DEC_BATCH, DEC_SEQ, D_MODEL)
    return (y_prompt, y_sample, jnp.stack(ckvs, axis=1), jnp.stack(krs, axis=1), jnp.stack(sts, axis=1))
```

```python
import functools
import math

import jax
import jax.numpy as jnp
from jax import lax
from jax.experimental import pallas as pl
from jax.experimental.pallas import tpu as pltpu

D_MODEL = 2048
BATCH = 16
SEQ = 256
DEPTH = 2
DEC_BATCH = 2
DEC_SEQ = 2048
PAST_LEN = 512
GRID_W = 64
ROPE_THETA = 10000.0
EPS = 1e-6
MLA_HEADS = 8
Q_RANK = 512
KV_RANK = 256
NOPE_DIM = 128
ROPE_DIM = 64
V_DIM = 128
MLA_WIDTH = MLA_HEADS * V_DIM
SSM_HEADS = 16
SSM_HEADDIM = 64
SSM_INNER = SSM_HEADS * SSM_HEADDIM
SSM_GROUPS = 2
SSM_STATE = 128
SSM_CONV = 5
SSM_CHUNK = 128
SSM_XBC = SSM_INNER + 2 * SSM_GROUPS * SSM_STATE
GM_WIDTH = 1024
GM_GROUPS = 4
GM_CHUNK = 128
N_BRANCH = 3
N_EXPERTS = 16
EXPERT_FF = 1024
EC_CAPACITY = 2

N_CTX = BATCH * SEQ
N_LAT = DEC_BATCH * DEC_SEQ
N_TOK = N_CTX + N_LAT
CAP_CTX = EC_CAPACITY * SEQ // N_EXPERTS
CAP_LAT = EC_CAPACITY * DEC_SEQ // N_EXPERTS
ROWS_CTX = BATCH * CAP_CTX
ROWS_LAT = DEC_BATCH * CAP_LAT

LANES = 128
QK_HEAD = 2 * LANES

ZB_Z = 0
ZB_CQ = SSM_INNER
ZB_XBC = ZB_CQ + Q_RANK
ZB_GM = ZB_XBC + SSM_XBC
ZB_GATE = ZB_GM + 2 * GM_WIDTH
ZB_WIDTH = ZB_GATE + N_BRANCH * D_MODEL
ZK_WIDTH = LANES

V7X_VMEM_LIMIT_BYTES = 56 * 1024 * 1024

BF16 = jnp.bfloat16
F32 = jnp.float32


def _cparams(*sem):
    return pltpu.CompilerParams(dimension_semantics=sem, vmem_limit_bytes=V7X_VMEM_LIMIT_BYTES)


def _rms(x, g):
    return x * lax.rsqrt(jnp.mean(x * x, axis=-1, keepdims=True) + EPS) * g


def _swap16(x):
    lane = lax.broadcasted_iota(jnp.int32, x.shape, 1)
    return jnp.where((lane % 32) < 16, pltpu.roll(x, LANES - 16, 1), pltpu.roll(x, 16, 1))


def _mm_body(x_ref, w_ref, o_ref):
    o_ref[...] = jnp.dot(x_ref[...].astype(BF16), w_ref[...].astype(BF16),
                         preferred_element_type=F32).astype(o_ref.dtype)


def _matmul(x, w, out_dtype=F32, tm=512, tn=512, name="matmul"):
    m, k = x.shape
    _, n = w.shape
    tn = min(tn, n)
    return pl.pallas_call(
        _mm_body,
        grid=(m // tm, n // tn),
        in_specs=[pl.BlockSpec((tm, k), lambda i, j: (i, 0)),
                  pl.BlockSpec((k, tn), lambda i, j: (0, j))],
        out_specs=pl.BlockSpec((tm, tn), lambda i, j: (i, j)),
        out_shape=jax.ShapeDtypeStruct((m, n), out_dtype),
        compiler_params=_cparams("parallel", "arbitrary"),
        name=name,
    )(x, w)


IN_TM = 1024
IN_TN = 512
NORM_ROWS = 256


def _ctx_lat_specs(tm, width, n_grid_axes):
    del n_grid_axes
    n_ctx = N_CTX // tm
    return [pl.BlockSpec((tm, width), lambda i, *_: (jnp.minimum(i, n_ctx - 1), 0)),
            pl.BlockSpec((tm, width), lambda i, *_: (jnp.maximum(i - n_ctx, 0), 0))]


def _ctx_or_lat(tm, c_ref, l_ref, rows=slice(None)):
    return jnp.where(pl.program_id(0) < N_CTX // tm, c_ref[rows, :], l_ref[rows, :])


def _norm_mod_to(h_scr, xc_ref, xl_ref, g_ref, sc_ref, sh_ref):
    g = g_ref[...]
    mul = 1.0 + sc_ref[0]
    add = sh_ref[0]
    for r in range(0, IN_TM, NORM_ROWS):
        x = _ctx_or_lat(IN_TM, xc_ref, xl_ref, slice(r, r + NORM_ROWS))
        h_scr[r:r + NORM_ROWS, :] = (_rms(x, g) * mul + add).astype(BF16)


W_IN_SEGS = {}
_off = 0
for _name, _w in (("cq", Q_RANK), ("ckv", KV_RANK), ("kr", ROPE_DIM), ("z", SSM_INNER), ("xbc", SSM_XBC),
                  ("dt", 2 * SSM_HEADS), ("gm", 2 * GM_WIDTH), ("gates", N_BRANCH * D_MODEL)):
    W_IN_SEGS[_name] = (_off, _w)
    _off += _w
ROW_UNIT = 32
ZB_TILE_STARTS = [(W_IN_SEGS[_name][0] + _k) // ROW_UNIT
                  for _name in ("z", "cq", "xbc", "gm", "gates")
                  for _k in range(0, W_IN_SEGS[_name][1], IN_TN)]
NT_DIMS = (((1,), (1,)), ((), ()))


def _inproj_body(tile_ref, xc_ref, xl_ref, g_ref, sc_ref, sh_ref, w_ref, o_ref, h_scr):
    del tile_ref
    @pl.when(pl.program_id(1) == 0)
    def _():
        _norm_mod_to(h_scr, xc_ref, xl_ref, g_ref, sc_ref, sh_ref)

    o_ref[...] = lax.dot_general(h_scr[...], w_ref[0].astype(BF16), NT_DIMS,
                                 preferred_element_type=F32).astype(o_ref.dtype)


def _inproj_small_body(xc_ref, xl_ref, g_ref, sc_ref, sh_ref, wckv_ref, wkr_ref, wdt_ref,
                       ckv_ref, zk_ref, h_scr, wzk_scr):
    _norm_mod_to(h_scr, xc_ref, xl_ref, g_ref, sc_ref, sh_ref)
    ckv_ref[...] = lax.dot_general(h_scr[...], wckv_ref[0].astype(BF16), NT_DIMS, preferred_element_type=F32)
    n_dt = 2 * SSM_HEADS
    wzk_scr[0:ROPE_DIM, :] = wkr_ref[0]
    wzk_scr[ROPE_DIM:ROPE_DIM + n_dt, :] = wdt_ref[0]
    wzk_scr[ROPE_DIM + n_dt:, :] = jnp.zeros((ZK_WIDTH - ROPE_DIM - n_dt, D_MODEL), F32)
    zk_ref[...] = lax.dot_general(h_scr[...], wzk_scr[...].astype(BF16), NT_DIMS, preferred_element_type=F32)


def _w_rows(layer, name):
    start, width = W_IN_SEGS[name]
    return pl.BlockSpec((pl.Element(1), pl.Element(width), pl.Element(D_MODEL)),
                        lambda i, j: (layer, start, 0))


def _inproj(x_c, x_l, g, sc_t, sh_t, w_in_t, layer):
    nt = N_TOK // IN_TM
    common = _ctx_lat_specs(IN_TM, D_MODEL, 2) + [
              pl.BlockSpec((1, D_MODEL), lambda i, j, *_: (0, 0)),
              pl.BlockSpec((1, 1, D_MODEL), lambda i, j, *_: (i, 0, 0)),
              pl.BlockSpec((1, 1, D_MODEL), lambda i, j, *_: (i, 0, 0))]
    tile_rows = pl.BlockSpec((pl.Element(1), pl.Element(IN_TN), pl.Element(D_MODEL)),
                             lambda i, j, tile: (layer, tile[j] * ROW_UNIT, 0))
    zb = pl.pallas_call(
        _inproj_body,
        grid_spec=pltpu.PrefetchScalarGridSpec(
            num_scalar_prefetch=1,
            grid=(nt, len(ZB_TILE_STARTS)),
            in_specs=common + [tile_rows],
            out_specs=pl.BlockSpec((IN_TM, IN_TN), lambda i, j, tile: (i, j)),
            scratch_shapes=[pltpu.VMEM((IN_TM, D_MODEL), BF16)]),
        out_shape=jax.ShapeDtypeStruct((N_TOK, ZB_WIDTH), BF16),
        compiler_params=_cparams("parallel", "arbitrary"),
        name="in_proj",
    )(jnp.asarray(ZB_TILE_STARTS, jnp.int32), x_c, x_l, g, sc_t, sh_t, w_in_t)
    ckv_raw, zk = pl.pallas_call(
        _inproj_small_body,
        grid=(nt, 1),
        in_specs=common + [_w_rows(layer, "ckv"), _w_rows(layer, "kr"), _w_rows(layer, "dt")],
        out_specs=[pl.BlockSpec((IN_TM, KV_RANK), lambda i, j: (i, 0)),
                   pl.BlockSpec((IN_TM, ZK_WIDTH), lambda i, j: (i, 0))],
        out_shape=[jax.ShapeDtypeStruct((N_TOK, KV_RANK), F32),
                   jax.ShapeDtypeStruct((N_TOK, ZK_WIDTH), F32)],
        scratch_shapes=[pltpu.VMEM((IN_TM, D_MODEL), BF16), pltpu.VMEM((ZK_WIDTH, D_MODEL), F32)],
        compiler_params=_cparams("parallel", "arbitrary"),
        name="in_proj_small",
    )(x_c, x_l, g, sc_t, sh_t, w_in_t, w_in_t, w_in_t)
    return zb, ckv_raw, zk


QKV_TM = 512
ATTN_SCALE = 1.0 / math.sqrt(NOPE_DIM + ROPE_DIM)


def _qproj_body(cq_ref, g_ref, w_ref, c_ref, s_ref, o_ref):
    qn = _rms(cq_ref[...].astype(F32), g_ref[...]).astype(BF16)
    q = jnp.dot(qn, w_ref[...], preferred_element_type=F32)
    c = c_ref[...]
    s = s_ref[...]
    for h in range(MLA_HEADS):
        lo = h * QK_HEAD
        r = q[:, lo + LANES:lo + QK_HEAD]
        o_ref[:, lo:lo + LANES] = (q[:, lo:lo + LANES] * ATTN_SCALE).astype(BF16)
        o_ref[:, lo + LANES:lo + QK_HEAD] = ((r * c + _swap16(r) * s) * ATTN_SCALE).astype(BF16)


def _qproj(zb, g_qn, w_uq_p, rope_c, rope_s):
    return pl.pallas_call(
        _qproj_body,
        grid=(N_TOK // QKV_TM,),
        in_specs=[pl.BlockSpec((QKV_TM, Q_RANK), lambda i: (i, ZB_CQ // Q_RANK)),
                  pl.BlockSpec((1, Q_RANK), lambda i: (0, 0)),
                  pl.BlockSpec((Q_RANK, MLA_HEADS * QK_HEAD), lambda i: (0, 0)),
                  pl.BlockSpec((QKV_TM, LANES), lambda i: (i, 0)),
                  pl.BlockSpec((QKV_TM, LANES), lambda i: (i, 0))],
        out_specs=pl.BlockSpec((QKV_TM, MLA_HEADS * QK_HEAD), lambda i: (i, 0)),
        out_shape=jax.ShapeDtypeStruct((N_TOK, MLA_HEADS * QK_HEAD), BF16),
        compiler_params=_cparams("parallel"),
        name="mla_q",
    )(zb, g_qn, w_uq_p, rope_c, rope_s)


def _kvproj_body(ckv_ref, zk_ref, g_ref, w_ref, c_ref, s_ref, ckv_o, kv_o, kr_o):
    ckv = _rms(ckv_ref[...], g_ref[...])
    ckv_o[...] = ckv
    kv_o[...] = jnp.dot(ckv.astype(BF16), w_ref[...], preferred_element_type=F32).astype(BF16)
    zk = zk_ref[...]
    is_rope = lax.broadcasted_iota(jnp.int32, zk.shape, 1) < ROPE_DIM
    kr = jnp.where(is_rope, zk, 0.0)
    rot = kr * c_ref[...] + _swap16(kr) * s_ref[...]
    kr_o[...] = jnp.where(is_rope, rot, 0.0).astype(BF16)


def _kvproj(ckv_raw, zk, g_kvn, w_ukv_p, rope_c, rope_s):
    return pl.pallas_call(
        _kvproj_body,
        grid=(N_TOK // QKV_TM,),
        in_specs=[pl.BlockSpec((QKV_TM, KV_RANK), lambda i: (i, 0)),
                  pl.BlockSpec((QKV_TM, ZK_WIDTH), lambda i: (i, 0)),
                  pl.BlockSpec((1, KV_RANK), lambda i: (0, 0)),
                  pl.BlockSpec((KV_RANK, 2 * MLA_WIDTH), lambda i: (0, 0)),
                  pl.BlockSpec((QKV_TM, LANES), lambda i: (i, 0)),
                  pl.BlockSpec((QKV_TM, LANES), lambda i: (i, 0))],
        out_specs=[pl.BlockSpec((QKV_TM, KV_RANK), lambda i: (i, 0)),
                   pl.BlockSpec((QKV_TM, 2 * MLA_WIDTH), lambda i: (i, 0)),
                   pl.BlockSpec((QKV_TM, LANES), lambda i: (i, 0))],
        out_shape=[jax.ShapeDtypeStruct((N_TOK, KV_RANK), F32),
                   jax.ShapeDtypeStruct((N_TOK, 2 * MLA_WIDTH), BF16),
                   jax.ShapeDtypeStruct((N_TOK, LANES), BF16)],
        compiler_params=_cparams("parallel"),
        name="mla_kv",
    )(ckv_raw, zk, g_kvn, w_ukv_p, rope_c, rope_s)


def _attn_body(n_parts, q_ref, *refs):
    o_ref = refs[3 * n_parts]
    for h in range(MLA_HEADS):
        qh = q_ref[:, h * QK_HEAD:(h + 1) * QK_HEAD]
        scores = []
        for p in range(n_parts):
            kn_ref, kr_ref = refs[3 * p], refs[3 * p + 1]
            kh = jnp.concatenate([kn_ref[:, h * LANES:(h + 1) * LANES], kr_ref[...]], axis=1)
            scores.append(lax.dot_general(qh, kh, (((1,), (1,)), ((), ())),
                                          preferred_element_type=F32))
        m = scores[0].max(axis=1, keepdims=True)
        for s in scores[1:]:
            m = jnp.maximum(m, s.max(axis=1, keepdims=True))
        den = 0.0
        acc = 0.0
        for p in range(n_parts):
            e = jnp.exp(scores[p] - m)
            den = den + e.sum(axis=1, keepdims=True)
            v_ref = refs[3 * p + 2]
            acc = acc + jnp.dot(e.astype(BF16), v_ref[:, h * LANES:(h + 1) * LANES],
                                preferred_element_type=F32)
        o_ref[:, h * LANES:(h + 1) * LANES] = (acc / den).astype(BF16)


def _attention(q, kv, kr, n_seq, seq_len, tq, row0, cache=None):
    nq = seq_len // tq
    q0 = row0 // tq
    s0 = row0 // seq_len
    in_specs = [pl.BlockSpec((tq, MLA_HEADS * QK_HEAD), lambda b, i: (q0 + b * nq + i, 0))]
    args = [q]
    if cache is not None:
        kv_c, kr_c, len_c = cache
        in_specs += [pl.BlockSpec((len_c, MLA_WIDTH), lambda b, i: (b, 0)),
                     pl.BlockSpec((len_c, LANES), lambda b, i: (b, 0)),
                     pl.BlockSpec((len_c, MLA_WIDTH), lambda b, i: (b, 1))]
        args += [kv_c, kr_c, kv_c]
    in_specs += [pl.BlockSpec((seq_len, MLA_WIDTH), lambda b, i: (s0 + b, 0)),
                 pl.BlockSpec((seq_len, LANES), lambda b, i: (s0 + b, 0)),
                 pl.BlockSpec((seq_len, MLA_WIDTH), lambda b, i: (s0 + b, 1))]
    args += [kv, kr, kv]
    n_parts = 1 if cache is None else 2
    return pl.pallas_call(
        functools.partial(_attn_body, n_parts),
        grid=(n_seq, nq),
        in_specs=in_specs,
        out_specs=pl.BlockSpec((tq, MLA_WIDTH), lambda b, i: (b * nq + i, 0)),
        out_shape=jax.ShapeDtypeStruct((n_seq * seq_len, MLA_WIDTH), BF16),
        compiler_params=_cparams("parallel", "arbitrary"),
        name="mla_attn_cache" if cache is not None else "mla_attn",
    )(*args)


MG_TM = 1024
MG_TN = 512
OUT_TM = 256


def _merge_body(ac_ref, al_ref, sc_ref, sl_ref, c_ref, wa_ref, ws_ref, wc_ref, ga_ref, gs_ref, gc_ref,
                o_ref):
    def branch(x, w_ref, gate_ref):
        y = jnp.dot(x, w_ref[0].astype(BF16), preferred_element_type=F32)
        return jax.nn.sigmoid(gate_ref[...].astype(F32)) * y

    o_ref[...] = (branch(_ctx_or_lat(MG_TM, ac_ref, al_ref), wa_ref, ga_ref)
                  + branch(_ctx_or_lat(MG_TM, sc_ref, sl_ref), ws_ref, gs_ref)
                  + branch(c_ref[...], wc_ref, gc_ref)).astype(BF16)


def _merge(attn_c, attn_l, ssm_c, ssm_l, gmo, wa, ws, wc, zb, layer):
    g0 = ZB_GATE // MG_TN
    gstep = D_MODEL // MG_TN
    pair = _ctx_lat_specs(MG_TM, MLA_WIDTH, 2)
    wspec = pl.BlockSpec((1, MLA_WIDTH, MG_TN), lambda i, j: (layer, 0, j))
    return pl.pallas_call(
        _merge_body,
        grid=(N_TOK // MG_TM, gstep),
        in_specs=pair + pair + [pl.BlockSpec((MG_TM, GM_WIDTH), lambda i, j: (i, 0)),
                                wspec, wspec, wspec,
                                pl.BlockSpec((MG_TM, MG_TN), lambda i, j: (i, g0 + j)),
                                pl.BlockSpec((MG_TM, MG_TN), lambda i, j: (i, g0 + gstep + j)),
                                pl.BlockSpec((MG_TM, MG_TN), lambda i, j: (i, g0 + 2 * gstep + j))],
        out_specs=pl.BlockSpec((MG_TM, MG_TN), lambda i, j: (i, j)),
        out_shape=jax.ShapeDtypeStruct((N_TOK, D_MODEL), BF16),
        compiler_params=_cparams("parallel", "arbitrary"),
        name="branch_merge",
    )(attn_c, attn_l, ssm_c, ssm_l, gmo, wa, ws, wc, zb, zb, zb)


def _split_bf16(x):
    hi = x.astype(BF16)
    return hi, (x - hi.astype(F32)).astype(BF16)


def _outproj_body(m_ref, w_ref, xc_ref, xl_ref, gt_ref, g_ref, sc_ref, sh_ref, wr_ref,
                  x1_ref, h2_ref, lg_ref):
    mix = jnp.dot(m_ref[...], w_ref[...], preferred_element_type=F32)
    x1 = _ctx_or_lat(OUT_TM, xc_ref, xl_ref) + gt_ref[0] * mix
    x1_ref[...] = x1
    h2 = _rms(x1, g_ref[...]) * (1.0 + sc_ref[0]) + sh_ref[0]
    h2_ref[...] = h2.astype(BF16)
    h_hi, h_lo = _split_bf16(h2)
    w_hi, w_lo = _split_bf16(wr_ref[...])
    lg_ref[...] = (jnp.dot(h_hi, w_hi, preferred_element_type=F32)
                   + jnp.dot(h_lo, w_hi, preferred_element_type=F32)
                   + jnp.dot(h_hi, w_lo, preferred_element_type=F32))


def _outproj(merged, w_out, x_c, x_l, gt_t, g2, sc_t, sh_t, w_router_p):
    mspec = pl.BlockSpec((1, 1, D_MODEL), lambda i: (i, 0, 0))
    return pl.pallas_call(
        _outproj_body,
        grid=(N_TOK // OUT_TM,),
        in_specs=[pl.BlockSpec((OUT_TM, D_MODEL), lambda i: (i, 0)),
                  pl.BlockSpec((D_MODEL, D_MODEL), lambda i: (0, 0))]
                 + _ctx_lat_specs(OUT_TM, D_MODEL, 1) + [
                  mspec,
                  pl.BlockSpec((1, D_MODEL), lambda i: (0, 0)),
                  mspec, mspec,
                  pl.BlockSpec((D_MODEL, LANES), lambda i: (0, 0))],
        out_specs=[pl.BlockSpec((OUT_TM, D_MODEL), lambda i: (i, 0)),
                   pl.BlockSpec((OUT_TM, D_MODEL), lambda i: (i, 0)),
                   pl.BlockSpec((OUT_TM, LANES), lambda i: (i, 0))],
        out_shape=[jax.ShapeDtypeStruct((N_TOK, D_MODEL), F32),
                   jax.ShapeDtypeStruct((N_TOK, D_MODEL), BF16),
                   jax.ShapeDtypeStruct((N_TOK, LANES), F32)],
        compiler_params=_cparams("parallel"),
        name="out_proj",
    )(merged, w_out, x_c, x_l, gt_t, g2, sc_t, sh_t, w_router_p)


PREFIX_BLK = 256
EXP_SEARCH_STEPS = 7
BISECT_STEPS = 40


def _prefix_count(mask):
    n = mask.shape[1]
    upper = (lax.broadcasted_iota(jnp.int32, (PREFIX_BLK, PREFIX_BLK), 0)
             < lax.broadcasted_iota(jnp.int32, (PREFIX_BLK, PREFIX_BLK), 1)).astype(BF16)
    run = jnp.zeros((mask.shape[0], 1), F32)
    outs = []
    for k in range(0, n, PREFIX_BLK):
        blk = mask[:, k:k + PREFIX_BLK]
        outs.append(jnp.dot(blk.astype(BF16), upper, preferred_element_type=F32) + run)
        run = run + jnp.sum(blk, axis=1, keepdims=True)
    return outs[0] if len(outs) == 1 else jnp.concatenate(outs, axis=1)


def _select_body(n_seq, seq_len, cap, lg_ref, slot_t_ref, slot_ref, aff_ref):
    lane = lax.broadcasted_iota(jnp.int32, (seq_len, LANES), 1)
    rows = []
    for s in range(n_seq):
        lg = jnp.where(lane < N_EXPERTS, lg_ref[s * seq_len:(s + 1) * seq_len, :], -jnp.inf)
        e = jnp.exp(lg - lg.max(axis=1, keepdims=True))
        aff = e / e.sum(axis=1, keepdims=True)
        rows.append(aff.T[:N_EXPERTS, :])
    a = rows[0] if n_seq == 1 else jnp.concatenate(rows, axis=0)

    def count_ge(t):
        return jnp.sum(jnp.where(a >= t, 1.0, 0.0), axis=1, keepdims=True)

    hi = jnp.full((a.shape[0], 1), 2.0, F32)
    for i in reversed(range(EXP_SEARCH_STEPS)):
        cand = hi * (2.0 ** -(2 ** i))
        hi = jnp.where(count_ge(cand) < cap, cand, hi)
    lo = jnp.where(hi <= 2.0 ** -126, 0.0, 0.5 * hi)
    for _ in range(BISECT_STEPS):
        mid = 0.5 * (lo + hi)
        ge = count_ge(mid) >= cap
        lo = jnp.where(ge, mid, lo)
        hi = jnp.where(ge, hi, mid)
    above = jnp.where(a >= hi, 1.0, 0.0)
    tied = jnp.where((a >= lo) & (a < hi), 1.0, 0.0)
    need = cap - jnp.sum(above, axis=1, keepdims=True)
    sel = above + tied * jnp.where(_prefix_count(tied) < need, 1.0, 0.0)
    slot = jnp.where(sel > 0.0, _prefix_count(sel), -1.0)
    slot_ref[...] = slot
    aff_ref[...] = a
    pad = jnp.full((LANES - N_EXPERTS, seq_len), -1.0, F32)
    for s in range(n_seq):
        blk = jnp.concatenate([slot[s * N_EXPERTS:(s + 1) * N_EXPERTS, :], pad], axis=0)
        slot_t_ref[s * seq_len:(s + 1) * seq_len, :] = blk.T.astype(jnp.int32)


def _select(logits, group, n_seq, seq_len, cap):
    rows = n_seq * N_EXPERTS
    n_rows = n_seq * seq_len
    return pl.pallas_call(
        functools.partial(_select_body, n_seq, seq_len, cap),
        grid=(1,),
        in_specs=[pl.BlockSpec((n_rows, LANES), lambda i: (group, 0))],
        out_specs=[pl.BlockSpec((n_rows, LANES), lambda i: (0, 0)),
                   pl.BlockSpec((rows, seq_len), lambda i: (0, 0)),
                   pl.BlockSpec((rows, seq_len), lambda i: (0, 0))],
        out_shape=[jax.ShapeDtypeStruct((n_rows, LANES), jnp.int32),
                   jax.ShapeDtypeStruct((rows, seq_len), F32),
                   jax.ShapeDtypeStruct((rows, seq_len), F32)],
        compiler_params=_cparams("arbitrary"),
        name="moe_select",
    )(logits)


def _gather_body(cap, epb, slot_ref, aff_ref, h_ref, xe_ref, ge_ref):
    n = h_ref.shape[0]
    if cap < LANES and epb == N_EXPERTS:
        assert cap & (cap - 1) == 0
        width = N_EXPERTS * cap
        row_expert = lax.shift_right_logical(lax.broadcasted_iota(jnp.int32, (width, N_EXPERTS), 0),
                                             cap.bit_length() - 1)
        spread = (lax.broadcasted_iota(jnp.int32, (width, N_EXPERTS), 1) == row_expert).astype(BF16)
        slot_x = jnp.dot(spread, slot_ref[...].astype(BF16), preferred_element_type=F32)
        target = (lax.broadcasted_iota(jnp.int32, (width, n), 0) & (cap - 1)).astype(F32)
        onehot = slot_x == target
        xe = jnp.dot(onehot.astype(BF16), h_ref[...], preferred_element_type=F32).astype(BF16)
        xe_ref[...] = xe.reshape(N_EXPERTS, cap, xe.shape[1])
        aff_x = sum(jnp.dot(spread, part, preferred_element_type=F32) for part in _split3(aff_ref[...]))
        gate = jnp.sum(jnp.where(onehot, aff_x, 0.0), axis=1, keepdims=True)
        ge_ref[...] = jnp.broadcast_to(gate, (width, LANES)).reshape(N_EXPERTS, cap, LANES)
        return
    step = pl.program_id(1)
    row = lax.broadcasted_iota(jnp.int32, (cap, n), 0).astype(F32)
    for k in range(epb):
        x = step * epb + k
        onehot = slot_ref[pl.ds(x, 1), :] == row
        xe_ref[k] = jnp.dot(onehot.astype(BF16), h_ref[...], preferred_element_type=F32).astype(BF16)
        gate = jnp.sum(jnp.where(onehot, aff_ref[pl.ds(x, 1), :], 0.0), axis=1, keepdims=True)
        ge_ref[k] = jnp.broadcast_to(gate, (cap, LANES))


def _gather(slot, aff, h2, n_seq, seq_len, cap, row0, epb):
    s0 = row0 // seq_len
    return pl.pallas_call(
        functools.partial(_gather_body, cap, epb),
        grid=(n_seq, N_EXPERTS // epb),
        in_specs=[pl.BlockSpec((N_EXPERTS, seq_len), lambda b, x: (b, 0)),
                  pl.BlockSpec((N_EXPERTS, seq_len), lambda b, x: (b, 0)),
                  pl.BlockSpec((seq_len, D_MODEL), lambda b, x: (s0 + b, 0))],
        out_specs=[pl.BlockSpec((epb, cap, D_MODEL), lambda b, x: (x, b, 0)),
                   pl.BlockSpec((epb, cap, LANES), lambda b, x: (x, b, 0))],
        out_shape=[jax.ShapeDtypeStruct((N_EXPERTS, n_seq * cap, D_MODEL), BF16),
                   jax.ShapeDtypeStruct((N_EXPERTS, n_seq * cap, LANES), F32)],
        compiler_params=_cparams("parallel", "arbitrary"),
        name="moe_gather",
    )(slot, aff, h2)


FFN_TF = 256


FFN_DOWN_TN = 512


def _ffn_body(xc_ref, xl_ref, gc_ref, gl_ref, wg_ref, wu_ref, wd_ref, yc_ref, yl_ref, hid_scr):
    f = pl.program_id(1)
    nf = EXPERT_FF // FFN_TF
    x = jnp.concatenate([xc_ref[0], xl_ref[0]], axis=0)
    gate = jnp.dot(x, wg_ref[0, 0].astype(BF16), preferred_element_type=F32)
    up = jnp.dot(x, wu_ref[0, 0].astype(BF16), preferred_element_type=F32)
    hid_scr[f] = (gate * jax.nn.sigmoid(gate) * up).astype(BF16)

    @pl.when(f == nf - 1)
    def _():
        hid = jnp.concatenate([hid_scr[k] for k in range(nf)], axis=1)
        g_c = gc_ref[0][:, :1]
        g_l = gl_ref[0][:, :1]
        for c in range(0, D_MODEL, FFN_DOWN_TN):
            out = jnp.dot(hid, wd_ref[0, 0, :, c:c + FFN_DOWN_TN].astype(BF16), preferred_element_type=F32)
            yc_ref[0, :, c:c + FFN_DOWN_TN] = (out[:ROWS_CTX, :] * g_c).astype(BF16)
            yl_ref[0, :, c:c + FFN_DOWN_TN] = (out[ROWS_CTX:, :] * g_l).astype(BF16)


def _expert_ffn(xe_c, xe_l, ge_c, ge_l, w_gate, w_up, w_down, layer):
    def xspec(rows, width):
        return pl.BlockSpec((1, rows, width), lambda i, f: (i, 0, 0))

    return pl.pallas_call(
        _ffn_body,
        grid=(N_EXPERTS, EXPERT_FF // FFN_TF),
        in_specs=[xspec(ROWS_CTX, D_MODEL), xspec(ROWS_LAT, D_MODEL),
                  xspec(ROWS_CTX, LANES), xspec(ROWS_LAT, LANES),
                  pl.BlockSpec((1, 1, D_MODEL, FFN_TF), lambda i, f: (layer, i, 0, f)),
                  pl.BlockSpec((1, 1, D_MODEL, FFN_TF), lambda i, f: (layer, i, 0, f)),
                  pl.BlockSpec((1, 1, EXPERT_FF, D_MODEL), lambda i, f: (layer, i, 0, 0))],
        out_specs=[xspec(ROWS_CTX, D_MODEL), xspec(ROWS_LAT, D_MODEL)],
        out_shape=[jax.ShapeDtypeStruct((N_EXPERTS, ROWS_CTX, D_MODEL), BF16),
                   jax.ShapeDtypeStruct((N_EXPERTS, ROWS_LAT, D_MODEL), BF16)],
        scratch_shapes=[pltpu.VMEM((EXPERT_FF // FFN_TF, ROWS_CTX + ROWS_LAT, FFN_TF), BF16)],
        compiler_params=_cparams("parallel", "arbitrary"),
        name="expert_ffn",
    )(xe_c, xe_l, ge_c, ge_l, w_gate, w_up, w_down)


CMB_TN = 512


def _combine_body(cap, slot_ref, y_ref, x_ref, gt_ref, o_ref, acc):
    n = slot_ref.shape[0]
    slot = slot_ref[...]
    if cap < LANES:
        assert cap & (cap - 1) == 0
        width = N_EXPERTS * cap
        lane_expert = lax.shift_right_logical(lax.broadcasted_iota(jnp.int32, (LANES, width), 1),
                                              cap.bit_length() - 1)
        spread = (lax.broadcasted_iota(jnp.int32, (LANES, width), 0) == lane_expert).astype(BF16)
        slot_x = jnp.dot(slot.astype(F32).astype(BF16), spread, preferred_element_type=F32)
        target = (lax.broadcasted_iota(jnp.int32, (n, width), 1) & (cap - 1)).astype(F32)
        onehot = (slot_x == target).astype(BF16)
        acc[...] = jnp.dot(onehot, y_ref[...].reshape(width, y_ref.shape[2]), preferred_element_type=F32)
    else:
        col = lax.broadcasted_iota(jnp.int32, (n, cap), 1)
        for x in range(N_EXPERTS):
            onehot = (slot[:, x:x + 1] == col).astype(BF16)
            part = jnp.dot(onehot, y_ref[x], preferred_element_type=F32)
            if x == 0:
                acc[...] = part
            else:
                acc[...] += part
    o_ref[...] = x_ref[...] + gt_ref[0] * acc[...]


def _combine(slot_t, y, x1, gt_t, n_seq, seq_len, cap, row0, tn):
    s0 = row0 // seq_len
    return pl.pallas_call(
        functools.partial(_combine_body, cap),
        grid=(n_seq, D_MODEL // tn),
        in_specs=[pl.BlockSpec((seq_len, LANES), lambda b, j: (b, 0)),
                  pl.BlockSpec((N_EXPERTS, cap, tn), lambda b, j: (0, b, j)),
                  pl.BlockSpec((seq_len, tn), lambda b, j: (s0 + b, j)),
                  pl.BlockSpec((1, 1, tn), lambda b, j: (b, 0, j))],
        out_specs=pl.BlockSpec((seq_len, tn), lambda b, j: (b, j)),
        out_shape=jax.ShapeDtypeStruct((n_seq * seq_len, D_MODEL), F32),
        scratch_shapes=[pltpu.VMEM((seq_len, tn), F32)],
        compiler_params=_cparams("parallel", "arbitrary"),
        name="moe_combine",
    )(slot_t, y, x1, gt_t)


CONV_TN = 256
CONV_HALO = 8
DT_LANE = ROPE_DIM


def _conv_body(x_ref, w_ref, b_ref, o_ref):
    seq_len = x_ref.shape[0]
    halo = jnp.zeros((CONV_HALO, x_ref.shape[1]), F32)
    ext = jnp.concatenate([halo, x_ref[...].astype(F32), halo], axis=0)
    w = w_ref[...]
    y = b_ref[...]
    for k in range(SSM_CONV):
        lo = CONV_HALO - SSM_CONV // 2 + k
        y = y + w[k:k + 1, :] * ext[lo:lo + seq_len, :]
    o_ref[...] = (y * jax.nn.sigmoid(y)).astype(BF16)


def _conv_silu(zb, conv_w8, conv_b, n_seq, seq_len, row0, tn):
    s0 = row0 // seq_len
    c0 = ZB_XBC // tn
    return pl.pallas_call(
        _conv_body,
        grid=(n_seq, SSM_XBC // tn),
        in_specs=[pl.BlockSpec((seq_len, tn), lambda b, j: (s0 + b, c0 + j)),
                  pl.BlockSpec((8, tn), lambda b, j: (0, j)),
                  pl.BlockSpec((1, tn), lambda b, j: (0, j))],
        out_specs=pl.BlockSpec((seq_len, tn), lambda b, j: (b, j)),
        out_shape=jax.ShapeDtypeStruct((n_seq * seq_len, SSM_XBC), BF16),
        compiler_params=_cparams("parallel", "arbitrary"),
        name="ssm_conv",
    )(zb, conv_w8, conv_b)


def _split3(x):
    hi = x.astype(BF16)
    r = x - hi.astype(F32)
    mid = r.astype(BF16)
    return hi, mid, (r - mid.astype(F32)).astype(BF16)


def _ssd_body(nc, xa_ref, zk_ref, z_ref, h0_ref, bias_ref, a_ref, d_ref, g_ref, ef_ref, eb_ref,
              o_ref, st_ref, hb_in, hf_cur, hb_cur):
    q = SSM_CHUNK
    half = SSM_INNER // SSM_GROUPS
    ii = lax.broadcasted_iota(jnp.int32, (q, q), 0)
    jj = lax.broadcasted_iota(jnp.int32, (q, q), 1)
    lower = ii >= jj
    upper = ii <= jj
    lower_b = lower.astype(BF16)
    upper_b = upper.astype(BF16)
    lane = lax.broadcasted_iota(jnp.int32, (q, LANES), 1)
    is_dt = (lane >= DT_LANE) & (lane < DT_LANE + 2 * SSM_HEADS)
    is_fwd = lane < DT_LANE + SSM_HEADS

    def tri_cumsum(tri, v):
        hi, mid, lo = _split3(v)
        return (jnp.dot(tri, hi, preferred_element_type=F32) + jnp.dot(tri, mid, preferred_element_type=F32)
                + jnp.dot(tri, lo, preferred_element_type=F32))

    def expand(v, e_ref):
        hi, lo = _split_bf16(v)
        return (jnp.dot(hi, e_ref[...], preferred_element_type=F32)
                + jnp.dot(lo, e_ref[...], preferred_element_type=F32))

    def chunk_factors(r0):
        dt = jnp.where(is_dt, jax.nn.softplus(zk_ref[pl.ds(r0, q), :] + bias_ref[...]), 0.0)
        dta = dt * a_ref[...]
        cum = jnp.where(is_fwd, tri_cumsum(lower_b, dta), tri_cumsum(upper_b, dta))
        tot = jnp.sum(dta, axis=0, keepdims=True)
        return dt, cum, tot

    def state_update(r0, dend_x, cdec_x, h_prev):
        x = xa_ref[pl.ds(r0, q), 0:SSM_INNER].astype(F32)
        xs = (x * dend_x).astype(BF16)
        parts = []
        for g in range(SSM_GROUPS):
            lo = SSM_INNER + g * SSM_STATE
            b_t = xa_ref[pl.ds(r0, q), lo:lo + SSM_STATE].astype(F32).T.astype(BF16)
            parts.append(jnp.dot(b_t, xs[:, g * half:(g + 1) * half], preferred_element_type=F32))
        return cdec_x * h_prev + jnp.concatenate(parts, axis=1)

    hf_cur[...] = h0_ref[0, 0]
    hb_cur[...] = h0_ref[0, 1]

    def bwd_step(t, carry):
        c = nc - 1 - t
        r0 = pl.multiple_of(c * q, q)
        dt, cum, tot = chunk_factors(r0)
        hb_in[c] = hb_cur[...]
        dend_x = expand(jnp.exp(tot - cum) * dt, eb_ref)
        cdec_x = expand(jnp.broadcast_to(jnp.exp(tot), (8, LANES)), eb_ref)[0:1]
        hb_cur[...] = state_update(r0, dend_x, cdec_x, hb_cur[...])
        return carry

    lax.fori_loop(0, nc, bwd_step, 0)

    def fwd_step(c, carry):
        r0 = pl.multiple_of(c * q, q)
        dt, cum, tot = chunk_factors(r0)
        cum_t = cum.T
        dt_t = dt.T
        eoff = jnp.exp(cum)
        x_bf = xa_ref[pl.ds(r0, q), 0:SSM_INNER]
        cb = []
        c_bf = []
        for g in range(SSM_GROUPS):
            lo_b = SSM_INNER + g * SSM_STATE
            lo_c = SSM_INNER + (SSM_GROUPS + g) * SSM_STATE
            c_g = xa_ref[pl.ds(r0, q), lo_c:lo_c + SSM_STATE]
            b_g = xa_ref[pl.ds(r0, q), lo_b:lo_b + SSM_STATE]
            c_bf.append(c_g)
            cb.append(lax.dot_general(c_g, b_g, (((1,), (1,)), ((), ())), preferred_element_type=F32))

        def head_matrix(h):
            f = DT_LANE + h
            b = DT_LANE + SSM_HEADS + h
            lf = jnp.where(lower, jnp.exp(cum[:, f:f + 1] - cum_t[f:f + 1, :]), 0.0) * dt_t[f:f + 1, :]
            lb = jnp.where(upper, jnp.exp(cum[:, b:b + 1] - cum_t[b:b + 1, :]), 0.0) * dt_t[b:b + 1, :]
            return (cb[h // (SSM_HEADS // SSM_GROUPS)] * (lf + lb)).astype(BF16)

        lane_lo = lane < SSM_HEADDIM
        pairs = []
        for hp in range(SSM_HEADS // 2):
            x_pair = x_bf[:, hp * LANES:(hp + 1) * LANES]
            y0 = jnp.dot(head_matrix(2 * hp), x_pair, preferred_element_type=F32)
            y1 = jnp.dot(head_matrix(2 * hp + 1), x_pair, preferred_element_type=F32)
            pairs.append(jnp.where(lane_lo, y0, y1))
        y = jnp.concatenate(pairs, axis=1)

        def off_diag(h_t, factor_x):
            h_bf = h_t.astype(BF16)
            parts = [jnp.dot(c_bf[g], h_bf[:, g * half:(g + 1) * half], preferred_element_type=F32)
                     for g in range(SSM_GROUPS)]
            return jnp.concatenate(parts, axis=1) * factor_x

        y = y + off_diag(hf_cur[...], expand(eoff, ef_ref)) + off_diag(hb_in[c], expand(eoff, eb_ref))
        y = y + d_ref[...] * x_bf.astype(F32)
        zg = z_ref[pl.ds(r0, q), :].astype(F32)
        o_ref[pl.ds(r0, q), :] = _rms(y * (zg * jax.nn.sigmoid(zg)), g_ref[...]).astype(BF16)

        dend_x = expand(jnp.exp(tot - cum) * dt, ef_ref)
        cdec_x = expand(jnp.broadcast_to(jnp.exp(tot), (8, LANES)), ef_ref)[0:1]
        hf_cur[...] = state_update(r0, dend_x, cdec_x, hf_cur[...])
        return carry

    lax.fori_loop(0, nc, fwd_step, 0)
    st_ref[0, 0] = hf_cur[...]
    st_ref[0, 1] = hb_cur[...]


def _ssd(xa, zk, zb, h0_t, dt_bias_p, a_p, d_x, g_ssm, e_f, e_b, n_seq, seq_len, row0):
    s0 = row0 // seq_len
    nc = seq_len // SSM_CHUNK
    vec = lambda w: pl.BlockSpec((1, w), lambda b: (0, 0))
    return pl.pallas_call(
        functools.partial(_ssd_body, nc),
        grid=(n_seq,),
        in_specs=[pl.BlockSpec((seq_len, SSM_XBC), lambda b: (b, 0)),
                  pl.BlockSpec((seq_len, ZK_WIDTH), lambda b: (s0 + b, 0)),
                  pl.BlockSpec((seq_len, SSM_INNER), lambda b: (s0 + b, ZB_Z // SSM_INNER)),
                  pl.BlockSpec((1, 2, SSM_STATE, SSM_INNER), lambda b: (b, 0, 0, 0)),
                  vec(LANES), vec(LANES), vec(SSM_INNER), vec(SSM_INNER),
                  pl.BlockSpec((LANES, SSM_INNER), lambda b: (0, 0)),
                  pl.BlockSpec((LANES, SSM_INNER), lambda b: (0, 0))],
        out_specs=[pl.BlockSpec((seq_len, SSM_INNER), lambda b: (b, 0)),
                   pl.BlockSpec((1, 2, SSM_STATE, SSM_INNER), lambda b: (b, 0, 0, 0))],
        out_shape=[jax.ShapeDtypeStruct((n_seq * seq_len, SSM_INNER), BF16),
                   jax.ShapeDtypeStruct((n_seq, 2, SSM_STATE, SSM_INNER), F32)],
        scratch_shapes=[pltpu.VMEM((nc, SSM_STATE, SSM_INNER), F32),
                        pltpu.VMEM((SSM_STATE, SSM_INNER), F32),
                        pltpu.VMEM((SSM_STATE, SSM_INNER), F32)],
        compiler_params=_cparams("parallel"),
        name="ssd",
    )(xa, zk, zb, h0_t, dt_bias_p, a_p, d_x, g_ssm, e_f, e_b)


GM_TM = 512
GM_GROUP_W = GM_WIDTH // GM_GROUPS


def _gmlp_body(u_ref, v_ref, g_ref, w_ref, b_ref, o_ref):
    for r in range(0, GM_TM, GM_CHUNK):
        u = jax.nn.gelu(u_ref[r:r + GM_CHUNK, :].astype(F32))
        vg = _rms(jax.nn.gelu(v_ref[r:r + GM_CHUNK, :].astype(F32)), g_ref[...]).astype(BF16)
        sv = jnp.concatenate(
            [jnp.dot(w_ref[k], vg[:, k * GM_GROUP_W:(k + 1) * GM_GROUP_W], preferred_element_type=F32)
             for k in range(GM_GROUPS)], axis=1)
        o_ref[r:r + GM_CHUNK, :] = (u * (sv + b_ref[...])).astype(BF16)


def _gmlp(zb, g_gv, w_sp, b_x):
    return pl.pallas_call(
        _gmlp_body,
        grid=(N_TOK // GM_TM,),
        in_specs=[pl.BlockSpec((GM_TM, GM_WIDTH), lambda i: (i, ZB_GM // GM_WIDTH)),
                  pl.BlockSpec((GM_TM, GM_WIDTH), lambda i: (i, ZB_GM // GM_WIDTH + 1)),
                  pl.BlockSpec((1, GM_WIDTH), lambda i: (0, 0)),
                  pl.BlockSpec((GM_GROUPS, GM_CHUNK, GM_CHUNK), lambda i: (0, 0, 0)),
                  pl.BlockSpec((GM_CHUNK, GM_WIDTH), lambda i: (0, 0))],
        out_specs=pl.BlockSpec((GM_TM, GM_WIDTH), lambda i: (i, 0)),
        out_shape=jax.ShapeDtypeStruct((N_TOK, GM_WIDTH), BF16),
        compiler_params=_cparams("parallel"),
        name="gmlp",
    )(zb, zb, g_gv, w_sp, b_x)


FN_TM = 512


def _final_norm_body(x_ref, g_ref, o_ref):
    o_ref[...] = _rms(x_ref[...], g_ref[...])


def _final_norm(x, g):
    rows = x.shape[0]
    return pl.pallas_call(
        _final_norm_body,
        grid=(rows // FN_TM,),
        in_specs=[pl.BlockSpec((FN_TM, D_MODEL), lambda i: (i, 0)),
                  pl.BlockSpec((1, D_MODEL), lambda i: (0, 0))],
        out_specs=pl.BlockSpec((FN_TM, D_MODEL), lambda i: (i, 0)),
        out_shape=jax.ShapeDtypeStruct((rows, D_MODEL), F32),
        compiler_params=_cparams("parallel"),
        name="final_norm",
    )(x, g)


def _dt_lanes(v):
    return jnp.pad(v.reshape(1, 2 * SSM_HEADS).astype(F32),
                   ((0, 0), (DT_LANE, LANES - DT_LANE - 2 * SSM_HEADS)))


def _head_expanders():
    lane = lax.broadcasted_iota(jnp.int32, (LANES, SSM_INNER), 0)
    head = lax.broadcasted_iota(jnp.int32, (LANES, SSM_INNER), 1) // SSM_HEADDIM
    e_f = (lane == head + DT_LANE).astype(BF16)
    e_b = (lane == head + DT_LANE + SSM_HEADS).astype(BF16)
    return e_f, e_b


MOD_TN = 1024
MOD_ROWS = 8
N_COND = 1 + DEC_BATCH


def _mod_body(ct_ref, w_ref, b_ref, o_ref):
    c = ct_ref[...]
    act = c * jax.nn.sigmoid(c)
    sub = lax.broadcasted_iota(jnp.int32, (MOD_ROWS, MOD_TN), 0)
    out = jnp.zeros((MOD_ROWS, MOD_TN), F32)
    for r in range(N_COND):
        y = jnp.sum(act[:, r:r + 1] * w_ref[0], axis=0, keepdims=True)
        out = jnp.where(sub == r, y, out)
    o_ref[0] = out + b_ref[0]


def _modulation(cond_t, w_mod, b_mod):
    n = 6 * D_MODEL
    return pl.pallas_call(
        _mod_body,
        grid=(DEPTH, n // MOD_TN),
        in_specs=[pl.BlockSpec((D_MODEL, MOD_ROWS), lambda l, j: (0, 0)),
                  pl.BlockSpec((1, D_MODEL, MOD_TN), lambda l, j: (l, 0, j)),
                  pl.BlockSpec((1, 1, MOD_TN), lambda l, j: (l, 0, j))],
        out_specs=pl.BlockSpec((1, MOD_ROWS, MOD_TN), lambda l, j: (l, 0, j)),
        out_shape=jax.ShapeDtypeStruct((DEPTH, MOD_ROWS, n), F32),
        compiler_params=_cparams("parallel", "arbitrary"),
        name="modulation",
    )(cond_t, w_mod, b_mod[:, None, :])


def _prep_w_uq(w):
    w = w.reshape(Q_RANK, MLA_HEADS, NOPE_DIM + ROPE_DIM)
    pad = jnp.zeros((Q_RANK, MLA_HEADS, QK_HEAD - NOPE_DIM - ROPE_DIM), w.dtype)
    return jnp.concatenate([w, pad], axis=2).reshape(Q_RANK, MLA_HEADS * QK_HEAD).astype(BF16)


def _prep_w_ukv(w):
    w = w.reshape(KV_RANK, MLA_HEADS, NOPE_DIM + V_DIM)
    return jnp.concatenate([w[:, :, :NOPE_DIM].reshape(KV_RANK, MLA_WIDTH),
                            w[:, :, NOPE_DIM:].reshape(KV_RANK, MLA_WIDTH)], axis=1).astype(BF16)


def _rope_tables(n_lat):
    rows = n_lat // GRID_W
    row = jnp.repeat(jnp.arange(rows), GRID_W).astype(F32)
    col = jnp.tile(jnp.arange(GRID_W), rows).astype(F32)
    nf = ROPE_DIM // 4
    freqs = jnp.power(ROPE_THETA, -jnp.arange(nf, dtype=F32) / nf)
    ang = jnp.stack([row[:, None] * freqs, col[:, None] * freqs], axis=1)
    cos, sin = jnp.cos(ang), jnp.sin(ang)
    c64 = jnp.concatenate([cos[:, 0], cos[:, 0], cos[:, 1], cos[:, 1]], axis=1)
    s64 = jnp.concatenate([-sin[:, 0], sin[:, 0], -sin[:, 1], sin[:, 1]], axis=1)
    c_lat = jnp.concatenate([c64, jnp.ones((n_lat, LANES - ROPE_DIM), F32)], axis=1)
    s_lat = jnp.concatenate([s64, jnp.zeros((n_lat, LANES - ROPE_DIM), F32)], axis=1)
    rope_c = jnp.concatenate([jnp.ones((N_CTX, LANES), F32), jnp.tile(c_lat, (DEC_BATCH, 1))], axis=0)
    rope_s = jnp.concatenate([jnp.zeros((N_CTX, LANES), F32), jnp.tile(s_lat, (DEC_BATCH, 1))], axis=0)
    return rope_c, rope_s


def _rows_mod(vec3, tm):
    idx = [0] * (N_CTX // tm) + [1] * (DEC_SEQ // tm) + [2] * (DEC_SEQ // tm)
    return vec3[jnp.array(idx)][:, None, :]


def kernel(x_prompt, x_sample, c, cache_ckv, cache_krope, state_ssm, c_ctx, w_mod, b_mod,
           g_norm1, g_norm2, w_in, g_qn, w_uq, g_kvn, w_ukv, conv_w, conv_b, dt_bias, a_log,
           d_skip, g_ssm, g_gv, w_sp, b_sp, w_br_attn, w_br_ssm, w_br_gmlp, w_out, w_router,
           w_gate, w_up, w_down, g_final):
    rope_c, rope_s = _rope_tables(DEC_SEQ)
    x_c = x_prompt.reshape(N_CTX, D_MODEL)
    x_l = x_sample.reshape(N_LAT, D_MODEL)
    cond_t = jnp.pad(jnp.concatenate([c_ctx[None, :], c], axis=0).T, ((0, 0), (0, MOD_ROWS - N_COND)))
    mod_all = _modulation(cond_t, w_mod, b_mod)
    e_f, e_b = _head_expanders()
    w_in_t = jnp.swapaxes(w_in, 1, 2)
    h0_c = jnp.zeros((BATCH, 2, SSM_STATE, SSM_INNER), F32)

    ckvs, krs, sts = [], [], []
    for l in range(DEPTH):
        mod = mod_all[l, :1 + DEC_BATCH].reshape(1 + DEC_BATCH, 6, D_MODEL)
        sh1, sc1, gt1, sh2, sc2, gt2 = [mod[:, k] for k in range(6)]

        zb, ckv_raw, zk = _inproj(x_c, x_l, g_norm1[l][None, :], _rows_mod(sc1, IN_TM), _rows_mod(sh1, IN_TM),
                                  w_in_t, l)

        q = _qproj(zb, g_qn[l][None, :], _prep_w_uq(w_uq[l]), rope_c, rope_s)
        w_ukv_p = _prep_w_ukv(w_ukv[l])
        ckv, kv, krot = _kvproj(ckv_raw, zk, g_kvn[l][None, :], w_ukv_p, rope_c, rope_s)
        kv_cache = _matmul(cache_ckv[:, l].reshape(DEC_BATCH * PAST_LEN, KV_RANK), w_ukv_p,
                           out_dtype=BF16, tn=2 * MLA_WIDTH, name="mla_kv_cache")
        kr_cache = jnp.pad(cache_krope[:, l].reshape(DEC_BATCH * PAST_LEN, ROPE_DIM),
                           ((0, 0), (0, LANES - ROPE_DIM))).astype(BF16)
        attn_c = _attention(q, kv, krot, BATCH, SEQ, SEQ, 0)
        attn_l = _attention(q, kv, krot, DEC_BATCH, DEC_SEQ, 256, N_CTX, cache=(kv_cache, kr_cache, PAST_LEN))

        conv_w8 = jnp.pad(conv_w[l], ((0, 8 - SSM_CONV), (0, 0)))
        ssd_par = (_dt_lanes(dt_bias[l]), _dt_lanes(-jnp.exp(a_log[l])),
                   jnp.repeat(d_skip[l], SSM_HEADDIM)[None, :], g_ssm[l][None, :], e_f, e_b)
        xa_c = _conv_silu(zb, conv_w8, conv_b[l][None, :], BATCH, SEQ, 0, SSM_XBC)
        xa_l = _conv_silu(zb, conv_w8, conv_b[l][None, :], DEC_BATCH, DEC_SEQ, N_CTX, CONV_TN)
        h0_l = jnp.transpose(state_ssm[:, l], (0, 1, 4, 2, 3)).reshape(DEC_BATCH, 2, SSM_STATE, SSM_INNER)
        ssm_c, st_c = _ssd(xa_c, zk, zb, h0_c, *ssd_par, BATCH, SEQ, 0)
        ssm_l, _ = _ssd(xa_l, zk, zb, h0_l, *ssd_par, DEC_BATCH, DEC_SEQ, N_CTX)
        gmo = _gmlp(zb, g_gv[l][None, :], w_sp[l].astype(BF16), jnp.repeat(b_sp[l].T, GM_GROUP_W, axis=1))
        st_c = jnp.transpose(st_c.reshape(BATCH, 2, SSM_STATE, SSM_HEADS, SSM_HEADDIM), (0, 1, 3, 4, 2))

        merged = _merge(attn_c, attn_l, ssm_c, ssm_l, gmo, w_br_attn, w_br_ssm, w_br_gmlp, zb, l)
        w_router_p = jnp.pad(w_router[l], ((0, 0), (0, LANES - N_EXPERTS)))
        x1, h2, logits = _outproj(merged, w_out[l].astype(BF16), x_c, x_l, _rows_mod(gt1, OUT_TM),
                                  g_norm2[l][None, :], _rows_mod(sc2, OUT_TM), _rows_mod(sh2, OUT_TM),
                                  w_router_p)

        slot_c, srow_c, arow_c = _select(logits, 0, BATCH, SEQ, CAP_CTX)
        slot_l, srow_l, arow_l = _select(logits, 1, DEC_BATCH, DEC_SEQ, CAP_LAT)
        xe_c, ge_c = _gather(srow_c, arow_c, h2, BATCH, SEQ, CAP_CTX, 0, N_EXPERTS)
        xe_l, ge_l = _gather(srow_l, arow_l, h2, DEC_BATCH, DEC_SEQ, CAP_LAT, N_CTX, 2)
        y_c, y_l = _expert_ffn(xe_c, xe_l, ge_c, ge_l, w_gate, w_up, w_down, l)
        gt2_c = jnp.broadcast_to(gt2[0][None, None, :], (BATCH, 1, D_MODEL))
        gt2_l = gt2[1:][:, None, :]
        x_c = _combine(slot_c, y_c, x1, gt2_c, BATCH, SEQ, CAP_CTX, 0, D_MODEL)
        x_l = _combine(slot_l, y_l, x1, gt2_l, DEC_BATCH, DEC_SEQ, CAP_LAT, N_CTX, CMB_TN)

        ckvs.append(ckv[:N_CTX].reshape(BATCH, SEQ, KV_RANK))
        krs.append(zk[:N_CTX, :ROPE_DIM].reshape(BATCH, SEQ, ROPE_DIM))
        sts.append(st_c)

    y_prompt = _final_norm(x_c, g_final[None, :]).reshape(BATCH, SEQ, D_MODEL)
    y_sample = _final_norm(x_l, g_final[None, :]).reshape(DEC_BATCH, DEC_SEQ, D_MODEL)
    return (y_prompt, y_sample, jnp.stack(ckvs, axis=1), jnp.stack(krs, axis=1), jnp.stack(sts, axis=1))
```

```python
import functools
import math

import jax
import jax.numpy as jnp
from jax import lax
from jax.experimental import pallas as pl
from jax.experimental.pallas import tpu as pltpu

D_MODEL = 2048
BATCH = 16
SEQ = 256
DEPTH = 2
DEC_BATCH = 2
DEC_SEQ = 2048
PAST_LEN = 512
GRID_W = 64
ROPE_THETA = 10000.0
EPS = 1e-6
MLA_HEADS = 8
Q_RANK = 512
KV_RANK = 256
NOPE_DIM = 128
ROPE_DIM = 64
V_DIM = 128
MLA_WIDTH = MLA_HEADS * V_DIM
SSM_HEADS = 16
SSM_HEADDIM = 64
SSM_INNER = SSM_HEADS * SSM_HEADDIM
SSM_GROUPS = 2
SSM_STATE = 128
SSM_CONV = 5
SSM_CHUNK = 128
SSM_XBC = SSM_INNER + 2 * SSM_GROUPS * SSM_STATE
GM_WIDTH = 1024
GM_GROUPS = 4
GM_CHUNK = 128
N_BRANCH = 3
N_EXPERTS = 16
EXPERT_FF = 1024
EC_CAPACITY = 2

N_CTX = BATCH * SEQ
N_LAT = DEC_BATCH * DEC_SEQ
N_TOK = N_CTX + N_LAT
CAP_CTX = EC_CAPACITY * SEQ // N_EXPERTS
CAP_LAT = EC_CAPACITY * DEC_SEQ // N_EXPERTS
ROWS_CTX = BATCH * CAP_CTX
ROWS_LAT = DEC_BATCH * CAP_LAT

LANES = 128
QK_HEAD = 2 * LANES

ZB_Z = 0
ZB_CQ = SSM_INNER
ZB_XBC = ZB_CQ + Q_RANK
ZB_GM = ZB_XBC + SSM_XBC
ZB_GATE = ZB_GM + 2 * GM_WIDTH
ZB_WIDTH = ZB_GATE + N_BRANCH * D_MODEL
ZK_WIDTH = LANES

V7X_VMEM_LIMIT_BYTES = 56 * 1024 * 1024

BF16 = jnp.bfloat16
F32 = jnp.float32


V7X_VMEM_LIMIT_LARGE_BYTES = 60 * 1024 * 1024


def _cparams(*sem, large=False):
    limit = V7X_VMEM_LIMIT_LARGE_BYTES if large else V7X_VMEM_LIMIT_BYTES
    return pltpu.CompilerParams(dimension_semantics=sem, vmem_limit_bytes=limit)


def _rms(x, g):
    return x * lax.rsqrt(jnp.mean(x * x, axis=-1, keepdims=True) + EPS) * g


def _swap16(x):
    lane = lax.broadcasted_iota(jnp.int32, x.shape, 1)
    return jnp.where((lane % 32) < 16, pltpu.roll(x, LANES - 16, 1), pltpu.roll(x, 16, 1))


def _mm_body(x_ref, w_ref, o_ref):
    o_ref[...] = jnp.dot(x_ref[...].astype(BF16), w_ref[...].astype(BF16),
                         preferred_element_type=F32).astype(o_ref.dtype)


def _matmul(x, w, out_dtype=F32, tm=512, tn=512, name="matmul"):
    m, k = x.shape
    _, n = w.shape
    tn = min(tn, n)
    return pl.pallas_call(
        _mm_body,
        grid=(m // tm, n // tn),
        in_specs=[pl.BlockSpec((tm, k), lambda i, j: (i, 0)),
                  pl.BlockSpec((k, tn), lambda i, j: (0, j))],
        out_specs=pl.BlockSpec((tm, tn), lambda i, j: (i, j)),
        out_shape=jax.ShapeDtypeStruct((m, n), out_dtype),
        compiler_params=_cparams("parallel", "arbitrary"),
        name=name,
    )(x, w)


IN_TM = 1024
IN_TN = 512
NORM_ROWS = 256


def _ctx_lat_specs(tm, width, n_grid_axes):
    del n_grid_axes
    n_ctx = N_CTX // tm
    return [pl.BlockSpec((tm, width), lambda i, *_: (jnp.minimum(i, n_ctx - 1), 0)),
            pl.BlockSpec((tm, width), lambda i, *_: (jnp.maximum(i - n_ctx, 0), 0))]


def _ctx_or_lat(tm, c_ref, l_ref, rows=slice(None)):
    return jnp.where(pl.program_id(0) < N_CTX // tm, c_ref[rows, :], l_ref[rows, :])


def _norm_mod_to(h_scr, xc_ref, xl_ref, g_ref, sc_ref, sh_ref):
    g = g_ref[...]
    mul = 1.0 + sc_ref[0]
    add = sh_ref[0]
    for r in range(0, IN_TM, NORM_ROWS):
        x = _ctx_or_lat(IN_TM, xc_ref, xl_ref, slice(r, r + NORM_ROWS))
        h_scr[r:r + NORM_ROWS, :] = (_rms(x, g) * mul + add).astype(BF16)


W_IN_SEGS = {}
_off = 0
for _name, _w in (("cq", Q_RANK), ("ckv", KV_RANK), ("kr", ROPE_DIM), ("z", SSM_INNER), ("xbc", SSM_XBC),
                  ("dt", 2 * SSM_HEADS), ("gm", 2 * GM_WIDTH), ("gates", N_BRANCH * D_MODEL)):
    W_IN_SEGS[_name] = (_off, _w)
    _off += _w
ROW_UNIT = 32
ZB_TILE_STARTS = [(W_IN_SEGS[_name][0] + _k) // ROW_UNIT
                  for _name in ("z", "cq", "xbc", "gm", "gates")
                  for _k in range(0, W_IN_SEGS[_name][1], IN_TN)]
NT_DIMS = (((1,), (1,)), ((), ()))


def _inproj_body(tile_ref, xc_ref, xl_ref, g_ref, sc_ref, sh_ref, w_ref, wckv_ref, wkr_ref, wdt_ref,
                 o_ref, ckv_ref, zk_ref, h_scr, wzk_scr):
    del tile_ref

    @pl.when(pl.program_id(1) == 0)
    def _():
        _norm_mod_to(h_scr, xc_ref, xl_ref, g_ref, sc_ref, sh_ref)
        ckv_ref[...] = lax.dot_general(h_scr[...], wckv_ref[0].astype(BF16), NT_DIMS,
                                       preferred_element_type=F32)
        n_dt = 2 * SSM_HEADS
        wzk_scr[0:ROPE_DIM, :] = wkr_ref[0].astype(BF16)
        wzk_scr[ROPE_DIM:ROPE_DIM + n_dt, :] = wdt_ref[0].astype(BF16)
        wzk_scr[ROPE_DIM + n_dt:, :] = jnp.zeros((ZK_WIDTH - ROPE_DIM - n_dt, D_MODEL), BF16)
        zk_ref[...] = lax.dot_general(h_scr[...], wzk_scr[...], NT_DIMS, preferred_element_type=F32)

    o_ref[...] = lax.dot_general(h_scr[...], w_ref[0].astype(BF16), NT_DIMS,
                                 preferred_element_type=F32).astype(o_ref.dtype)


def _w_rows(layer, name):
    start, width = W_IN_SEGS[name]
    return pl.BlockSpec((pl.Element(1), pl.Element(width), pl.Element(D_MODEL)),
                        lambda i, j, tile: (layer, start, 0))


def _inproj(x_c, x_l, g, sc_t, sh_t, w_in_t, layer):
    nt = N_TOK // IN_TM
    common = _ctx_lat_specs(IN_TM, D_MODEL, 2) + [
              pl.BlockSpec((1, D_MODEL), lambda i, j, *_: (0, 0)),
              pl.BlockSpec((1, 1, D_MODEL), lambda i, j, *_: (i, 0, 0)),
              pl.BlockSpec((1, 1, D_MODEL), lambda i, j, *_: (i, 0, 0))]
    tile_rows = pl.BlockSpec((pl.Element(1), pl.Element(IN_TN), pl.Element(D_MODEL)),
                             lambda i, j, tile: (layer, tile[j] * ROW_UNIT, 0))
    return pl.pallas_call(
        _inproj_body,
        grid_spec=pltpu.PrefetchScalarGridSpec(
            num_scalar_prefetch=1,
            grid=(nt, len(ZB_TILE_STARTS)),
            in_specs=common + [tile_rows, _w_rows(layer, "ckv"), _w_rows(layer, "kr"), _w_rows(layer, "dt")],
            out_specs=[pl.BlockSpec((IN_TM, IN_TN), lambda i, j, tile: (i, j)),
                       pl.BlockSpec((IN_TM, KV_RANK), lambda i, j, tile: (i, 0)),
                       pl.BlockSpec((IN_TM, ZK_WIDTH), lambda i, j, tile: (i, 0))],
            scratch_shapes=[pltpu.VMEM((IN_TM, D_MODEL), BF16), pltpu.VMEM((ZK_WIDTH, D_MODEL), BF16)]),
        out_shape=[jax.ShapeDtypeStruct((N_TOK, ZB_WIDTH), BF16),
                   jax.ShapeDtypeStruct((N_TOK, KV_RANK), F32),
                   jax.ShapeDtypeStruct((N_TOK, ZK_WIDTH), F32)],
        compiler_params=_cparams("parallel", "arbitrary", large=True),
        name="in_proj",
    )(jnp.asarray(ZB_TILE_STARTS, jnp.int32), x_c, x_l, g, sc_t, sh_t, w_in_t, w_in_t, w_in_t, w_in_t)


QKV_TM = 512
ATTN_SCALE = 1.0 / math.sqrt(NOPE_DIM + ROPE_DIM)


def _qproj_body(cq_ref, g_ref, w_ref, c_ref, s_ref, o_ref):
    qn = _rms(cq_ref[...].astype(F32), g_ref[...]).astype(BF16)
    q = jnp.dot(qn, w_ref[...], preferred_element_type=F32)
    c = c_ref[...]
    s = s_ref[...]
    for h in range(MLA_HEADS):
        lo = h * QK_HEAD
        r = q[:, lo + LANES:lo + QK_HEAD]
        o_ref[:, lo:lo + LANES] = (q[:, lo:lo + LANES] * ATTN_SCALE).astype(BF16)
        o_ref[:, lo + LANES:lo + QK_HEAD] = ((r * c + _swap16(r) * s) * ATTN_SCALE).astype(BF16)


def _qproj(zb, g_qn, w_uq_p, rope_c, rope_s):
    return pl.pallas_call(
        _qproj_body,
        grid=(N_TOK // QKV_TM,),
        in_specs=[pl.BlockSpec((QKV_TM, Q_RANK), lambda i: (i, ZB_CQ // Q_RANK)),
                  pl.BlockSpec((1, Q_RANK), lambda i: (0, 0)),
                  pl.BlockSpec((Q_RANK, MLA_HEADS * QK_HEAD), lambda i: (0, 0)),
                  pl.BlockSpec((QKV_TM, LANES), lambda i: (i, 0)),
                  pl.BlockSpec((QKV_TM, LANES), lambda i: (i, 0))],
        out_specs=pl.BlockSpec((QKV_TM, MLA_HEADS * QK_HEAD), lambda i: (i, 0)),
        out_shape=jax.ShapeDtypeStruct((N_TOK, MLA_HEADS * QK_HEAD), BF16),
        compiler_params=_cparams("parallel"),
        name="mla_q",
    )(zb, g_qn, w_uq_p, rope_c, rope_s)


def _kvproj_body(ckv_ref, zk_ref, g_ref, w_ref, c_ref, s_ref, ckv_o, kv_o, kr_o):
    ckv = _rms(ckv_ref[...], g_ref[...])
    ckv_o[...] = ckv
    kv_o[...] = jnp.dot(ckv.astype(BF16), w_ref[...], preferred_element_type=F32).astype(BF16)
    zk = zk_ref[...]
    is_rope = lax.broadcasted_iota(jnp.int32, zk.shape, 1) < ROPE_DIM
    kr = jnp.where(is_rope, zk, 0.0)
    rot = kr * c_ref[...] + _swap16(kr) * s_ref[...]
    kr_o[...] = jnp.where(is_rope, rot, 0.0).astype(BF16)


def _kvproj(ckv_raw, zk, g_kvn, w_ukv_p, rope_c, rope_s):
    return pl.pallas_call(
        _kvproj_body,
        grid=(N_TOK // QKV_TM,),
        in_specs=[pl.BlockSpec((QKV_TM, KV_RANK), lambda i: (i, 0)),
                  pl.BlockSpec((QKV_TM, ZK_WIDTH), lambda i: (i, 0)),
                  pl.BlockSpec((1, KV_RANK), lambda i: (0, 0)),
                  pl.BlockSpec((KV_RANK, 2 * MLA_WIDTH), lambda i: (0, 0)),
                  pl.BlockSpec((QKV_TM, LANES), lambda i: (i, 0)),
                  pl.BlockSpec((QKV_TM, LANES), lambda i: (i, 0))],
        out_specs=[pl.BlockSpec((QKV_TM, KV_RANK), lambda i: (i, 0)),
                   pl.BlockSpec((QKV_TM, 2 * MLA_WIDTH), lambda i: (i, 0)),
                   pl.BlockSpec((QKV_TM, LANES), lambda i: (i, 0))],
        out_shape=[jax.ShapeDtypeStruct((N_TOK, KV_RANK), F32),
                   jax.ShapeDtypeStruct((N_TOK, 2 * MLA_WIDTH), BF16),
                   jax.ShapeDtypeStruct((N_TOK, LANES), BF16)],
        compiler_params=_cparams("parallel"),
        name="mla_kv",
    )(ckv_raw, zk, g_kvn, w_ukv_p, rope_c, rope_s)


def _attn_body(n_parts, q_ref, *refs):
    o_ref = refs[3 * n_parts]
    for h in range(MLA_HEADS):
        qh = q_ref[:, h * QK_HEAD:(h + 1) * QK_HEAD]
        scores = []
        for p in range(n_parts):
            kn_ref, kr_ref = refs[3 * p], refs[3 * p + 1]
            kh = jnp.concatenate([kn_ref[:, h * LANES:(h + 1) * LANES], kr_ref[...]], axis=1)
            scores.append(lax.dot_general(qh, kh, (((1,), (1,)), ((), ())),
                                          preferred_element_type=F32))
        m = scores[0].max(axis=1, keepdims=True)
        for s in scores[1:]:
            m = jnp.maximum(m, s.max(axis=1, keepdims=True))
        den = 0.0
        acc = 0.0
        for p in range(n_parts):
            e = jnp.exp(scores[p] - m)
            den = den + e.sum(axis=1, keepdims=True)
            v_ref = refs[3 * p + 2]
            acc = acc + jnp.dot(e.astype(BF16), v_ref[:, h * LANES:(h + 1) * LANES],
                                preferred_element_type=F32)
        o_ref[:, h * LANES:(h + 1) * LANES] = (acc / den).astype(BF16)


def _attention(q, kv, kr, n_seq, seq_len, tq, row0, cache=None):
    nq = seq_len // tq
    q0 = row0 // tq
    s0 = row0 // seq_len
    in_specs = [pl.BlockSpec((tq, MLA_HEADS * QK_HEAD), lambda b, i: (q0 + b * nq + i, 0))]
    args = [q]
    if cache is not None:
        kv_c, kr_c, len_c = cache
        in_specs += [pl.BlockSpec((len_c, MLA_WIDTH), lambda b, i: (b, 0)),
                     pl.BlockSpec((len_c, LANES), lambda b, i: (b, 0)),
                     pl.BlockSpec((len_c, MLA_WIDTH), lambda b, i: (b, 1))]
        args += [kv_c, kr_c, kv_c]
    in_specs += [pl.BlockSpec((seq_len, MLA_WIDTH), lambda b, i: (s0 + b, 0)),
                 pl.BlockSpec((seq_len, LANES), lambda b, i: (s0 + b, 0)),
                 pl.BlockSpec((seq_len, MLA_WIDTH), lambda b, i: (s0 + b, 1))]
    args += [kv, kr, kv]
    n_parts = 1 if cache is None else 2
    return pl.pallas_call(
        functools.partial(_attn_body, n_parts),
        grid=(n_seq, nq),
        in_specs=in_specs,
        out_specs=pl.BlockSpec((tq, MLA_WIDTH), lambda b, i: (b * nq + i, 0)),
        out_shape=jax.ShapeDtypeStruct((n_seq * seq_len, MLA_WIDTH), BF16),
        compiler_params=_cparams("parallel", "arbitrary", large=cache is not None),
        name="mla_attn_cache" if cache is not None else "mla_attn",
    )(*args)


MG_TM = 1024
MG_TN = 512
OUT_TM = 256


def _merge_body(ac_ref, al_ref, sc_ref, sl_ref, c_ref, wa_ref, ws_ref, wc_ref, ga_ref, gs_ref, gc_ref,
                o_ref):
    def branch(x, w_ref, gate_ref):
        y = jnp.dot(x, w_ref[0].astype(BF16), preferred_element_type=F32)
        return jax.nn.sigmoid(gate_ref[...].astype(F32)) * y

    o_ref[...] = (branch(_ctx_or_lat(MG_TM, ac_ref, al_ref), wa_ref, ga_ref)
                  + branch(_ctx_or_lat(MG_TM, sc_ref, sl_ref), ws_ref, gs_ref)
                  + branch(c_ref[...], wc_ref, gc_ref)).astype(BF16)


def _merge(attn_c, attn_l, ssm_c, ssm_l, gmo, wa, ws, wc, zb, layer):
    g0 = ZB_GATE // MG_TN
    gstep = D_MODEL // MG_TN
    pair = _ctx_lat_specs(MG_TM, MLA_WIDTH, 2)
    wspec = pl.BlockSpec((1, MLA_WIDTH, MG_TN), lambda i, j: (layer, 0, j))
    return pl.pallas_call(
        _merge_body,
        grid=(N_TOK // MG_TM, gstep),
        in_specs=pair + pair + [pl.BlockSpec((MG_TM, GM_WIDTH), lambda i, j: (i, 0)),
                                wspec, wspec, wspec,
                                pl.BlockSpec((MG_TM, MG_TN), lambda i, j: (i, g0 + j)),
                                pl.BlockSpec((MG_TM, MG_TN), lambda i, j: (i, g0 + gstep + j)),
                                pl.BlockSpec((MG_TM, MG_TN), lambda i, j: (i, g0 + 2 * gstep + j))],
        out_specs=pl.BlockSpec((MG_TM, MG_TN), lambda i, j: (i, j)),
        out_shape=jax.ShapeDtypeStruct((N_TOK, D_MODEL), BF16),
        compiler_params=_cparams("parallel", "arbitrary"),
        name="branch_merge",
    )(attn_c, attn_l, ssm_c, ssm_l, gmo, wa, ws, wc, zb, zb, zb)


def _split_bf16(x):
    hi = x.astype(BF16)
    return hi, (x - hi.astype(F32)).astype(BF16)


def _outproj_body(m_ref, w_ref, xc_ref, xl_ref, gt_ref, g_ref, sc_ref, sh_ref, wr_ref,
                  x1_ref, h2_ref, lg_ref):
    mix = jnp.dot(m_ref[...], w_ref[...], preferred_element_type=F32)
    x1 = _ctx_or_lat(OUT_TM, xc_ref, xl_ref) + gt_ref[0] * mix
    x1_ref[...] = x1
    h2 = _rms(x1, g_ref[...]) * (1.0 + sc_ref[0]) + sh_ref[0]
    h2_ref[...] = h2.astype(BF16)
    h_hi, h_lo = _split_bf16(h2)
    w_hi, w_lo = _split_bf16(wr_ref[...])
    lg_ref[...] = (jnp.dot(h_hi, w_hi, preferred_element_type=F32)
                   + jnp.dot(h_lo, w_hi, preferred_element_type=F32)
                   + jnp.dot(h_hi, w_lo, preferred_element_type=F32))


def _outproj(merged, w_out, x_c, x_l, gt_t, g2, sc_t, sh_t, w_router_p):
    mspec = pl.BlockSpec((1, 1, D_MODEL), lambda i: (i, 0, 0))
    return pl.pallas_call(
        _outproj_body,
        grid=(N_TOK // OUT_TM,),
        in_specs=[pl.BlockSpec((OUT_TM, D_MODEL), lambda i: (i, 0)),
                  pl.BlockSpec((D_MODEL, D_MODEL), lambda i: (0, 0))]
                 + _ctx_lat_specs(OUT_TM, D_MODEL, 1) + [
                  mspec,
                  pl.BlockSpec((1, D_MODEL), lambda i: (0, 0)),
                  mspec, mspec,
                  pl.BlockSpec((D_MODEL, LANES), lambda i: (0, 0))],
        out_specs=[pl.BlockSpec((OUT_TM, D_MODEL), lambda i: (i, 0)),
                   pl.BlockSpec((OUT_TM, D_MODEL), lambda i: (i, 0)),
                   pl.BlockSpec((OUT_TM, LANES), lambda i: (i, 0))],
        out_shape=[jax.ShapeDtypeStruct((N_TOK, D_MODEL), F32),
                   jax.ShapeDtypeStruct((N_TOK, D_MODEL), BF16),
                   jax.ShapeDtypeStruct((N_TOK, LANES), F32)],
        compiler_params=_cparams("parallel"),
        name="out_proj",
    )(merged, w_out, x_c, x_l, gt_t, g2, sc_t, sh_t, w_router_p)


PREFIX_BLK = 256
EXP_SEARCH_STEPS = 7
BISECT_STEPS = 40


def _prefix_count(mask):
    n = mask.shape[1]
    upper = (lax.broadcasted_iota(jnp.int32, (PREFIX_BLK, PREFIX_BLK), 0)
             < lax.broadcasted_iota(jnp.int32, (PREFIX_BLK, PREFIX_BLK), 1)).astype(BF16)
    run = jnp.zeros((mask.shape[0], 1), F32)
    outs = []
    for k in range(0, n, PREFIX_BLK):
        blk = mask[:, k:k + PREFIX_BLK]
        outs.append(jnp.dot(blk.astype(BF16), upper, preferred_element_type=F32) + run)
        run = run + jnp.sum(blk, axis=1, keepdims=True)
    return outs[0] if len(outs) == 1 else jnp.concatenate(outs, axis=1)


def _select_body(n_seq, seq_len, cap, lg_ref, slot_t_ref, slot_ref, aff_ref):
    lane = lax.broadcasted_iota(jnp.int32, (seq_len, LANES), 1)
    rows = []
    for s in range(n_seq):
        lg = jnp.where(lane < N_EXPERTS, lg_ref[s * seq_len:(s + 1) * seq_len, :], -jnp.inf)
        e = jnp.exp(lg - lg.max(axis=1, keepdims=True))
        aff = e / e.sum(axis=1, keepdims=True)
        rows.append(aff.T[:N_EXPERTS, :])
    a = rows[0] if n_seq == 1 else jnp.concatenate(rows, axis=0)

    def count_ge(t):
        return jnp.sum(jnp.where(a >= t, 1.0, 0.0), axis=1, keepdims=True)

    hi = jnp.full((a.shape[0], 1), 2.0, F32)
    for i in reversed(range(EXP_SEARCH_STEPS)):
        cand = hi * (2.0 ** -(2 ** i))
        hi = jnp.where(count_ge(cand) < cap, cand, hi)
    lo = jnp.where(hi <= 2.0 ** -126, 0.0, 0.5 * hi)
    for _ in range(BISECT_STEPS):
        mid = 0.5 * (lo + hi)
        ge = count_ge(mid) >= cap
        lo = jnp.where(ge, mid, lo)
        hi = jnp.where(ge, hi, mid)
    above = jnp.where(a >= hi, 1.0, 0.0)
    tied = jnp.where((a >= lo) & (a < hi), 1.0, 0.0)
    need = cap - jnp.sum(above, axis=1, keepdims=True)
    sel = above + tied * jnp.where(_prefix_count(tied) < need, 1.0, 0.0)
    slot = jnp.where(sel > 0.0, _prefix_count(sel), -1.0)
    slot_ref[...] = slot
    aff_ref[...] = a
    pad = jnp.full((LANES - N_EXPERTS, seq_len), -1.0, F32)
    for s in range(n_seq):
        blk = jnp.concatenate([slot[s * N_EXPERTS:(s + 1) * N_EXPERTS, :], pad], axis=0)
        slot_t_ref[s * seq_len:(s + 1) * seq_len, :] = blk.T.astype(jnp.int32)


def _select(logits, group, n_seq, seq_len, cap):
    rows = n_seq * N_EXPERTS
    n_rows = n_seq * seq_len
    return pl.pallas_call(
        functools.partial(_select_body, n_seq, seq_len, cap),
        grid=(1,),
        in_specs=[pl.BlockSpec((n_rows, LANES), lambda i: (group, 0))],
        out_specs=[pl.BlockSpec((n_rows, LANES), lambda i: (0, 0)),
                   pl.BlockSpec((rows, seq_len), lambda i: (0, 0)),
                   pl.BlockSpec((rows, seq_len), lambda i: (0, 0))],
        out_shape=[jax.ShapeDtypeStruct((n_rows, LANES), jnp.int32),
                   jax.ShapeDtypeStruct((rows, seq_len), F32),
                   jax.ShapeDtypeStruct((rows, seq_len), F32)],
        compiler_params=_cparams("arbitrary"),
        name="moe_select",
    )(logits)


def _gather_body(cap, epb, slot_ref, aff_ref, h_ref, xe_ref, ge_ref):
    n = h_ref.shape[0]
    if cap < LANES and epb == N_EXPERTS:
        assert cap & (cap - 1) == 0
        width = N_EXPERTS * cap
        row_expert = lax.shift_right_logical(lax.broadcasted_iota(jnp.int32, (width, N_EXPERTS), 0),
                                             cap.bit_length() - 1)
        spread = (lax.broadcasted_iota(jnp.int32, (width, N_EXPERTS), 1) == row_expert).astype(BF16)
        slot_x = jnp.dot(spread, slot_ref[...].astype(BF16), preferred_element_type=F32)
        target = (lax.broadcasted_iota(jnp.int32, (width, n), 0) & (cap - 1)).astype(F32)
        onehot = slot_x == target
        xe = jnp.dot(onehot.astype(BF16), h_ref[...], preferred_element_type=F32).astype(BF16)
        xe_ref[...] = xe.reshape(N_EXPERTS, cap, xe.shape[1])
        aff_x = sum(jnp.dot(spread, part, preferred_element_type=F32) for part in _split3(aff_ref[...]))
        gate = jnp.sum(jnp.where(onehot, aff_x, 0.0), axis=1, keepdims=True)
        ge_ref[...] = jnp.broadcast_to(gate, (width, LANES)).reshape(N_EXPERTS, cap, LANES)
        return
    step = pl.program_id(1)
    row = lax.broadcasted_iota(jnp.int32, (cap, n), 0).astype(F32)
    for k in range(epb):
        x = step * epb + k
        onehot = slot_ref[pl.ds(x, 1), :] == row
        xe_ref[k] = jnp.dot(onehot.astype(BF16), h_ref[...], preferred_element_type=F32).astype(BF16)
        gate = jnp.sum(jnp.where(onehot, aff_ref[pl.ds(x, 1), :], 0.0), axis=1, keepdims=True)
        ge_ref[k] = jnp.broadcast_to(gate, (cap, LANES))


def _gather(slot, aff, h2, n_seq, seq_len, cap, row0, epb):
    s0 = row0 // seq_len
    return pl.pallas_call(
        functools.partial(_gather_body, cap, epb),
        grid=(n_seq, N_EXPERTS // epb),
        in_specs=[pl.BlockSpec((N_EXPERTS, seq_len), lambda b, x: (b, 0)),
                  pl.BlockSpec((N_EXPERTS, seq_len), lambda b, x: (b, 0)),
                  pl.BlockSpec((seq_len, D_MODEL), lambda b, x: (s0 + b, 0))],
        out_specs=[pl.BlockSpec((epb, cap, D_MODEL), lambda b, x: (x, b, 0)),
                   pl.BlockSpec((epb, cap, LANES), lambda b, x: (x, b, 0))],
        out_shape=[jax.ShapeDtypeStruct((N_EXPERTS, n_seq * cap, D_MODEL), BF16),
                   jax.ShapeDtypeStruct((N_EXPERTS, n_seq * cap, LANES), F32)],
        compiler_params=_cparams("parallel", "arbitrary"),
        name="moe_gather",
    )(slot, aff, h2)


FFN_TF = 256


FFN_DOWN_TN = 512


def _ffn_body(xc_ref, xl_ref, gc_ref, gl_ref, wg_ref, wu_ref, wd_ref, yc_ref, yl_ref, hid_scr):
    f = pl.program_id(1)
    nf = EXPERT_FF // FFN_TF
    x = jnp.concatenate([xc_ref[0], xl_ref[0]], axis=0)
    gate = jnp.dot(x, wg_ref[0, 0].astype(BF16), preferred_element_type=F32)
    up = jnp.dot(x, wu_ref[0, 0].astype(BF16), preferred_element_type=F32)
    hid_scr[f] = (gate * jax.nn.sigmoid(gate) * up).astype(BF16)

    @pl.when(f == nf - 1)
    def _():
        hid = jnp.concatenate([hid_scr[k] for k in range(nf)], axis=1)
        g_c = gc_ref[0][:, :1]
        g_l = gl_ref[0][:, :1]
        for c in range(0, D_MODEL, FFN_DOWN_TN):
            out = jnp.dot(hid, wd_ref[0, 0, :, c:c + FFN_DOWN_TN].astype(BF16), preferred_element_type=F32)
            yc_ref[0, :, c:c + FFN_DOWN_TN] = (out[:ROWS_CTX, :] * g_c).astype(BF16)
            yl_ref[0, :, c:c + FFN_DOWN_TN] = (out[ROWS_CTX:, :] * g_l).astype(BF16)


def _expert_ffn(xe_c, xe_l, ge_c, ge_l, w_gate, w_up, w_down, layer):
    def xspec(rows, width):
        return pl.BlockSpec((1, rows, width), lambda i, f: (i, 0, 0))

    return pl.pallas_call(
        _ffn_body,
        grid=(N_EXPERTS, EXPERT_FF // FFN_TF),
        in_specs=[xspec(ROWS_CTX, D_MODEL), xspec(ROWS_LAT, D_MODEL),
                  xspec(ROWS_CTX, LANES), xspec(ROWS_LAT, LANES),
                  pl.BlockSpec((1, 1, D_MODEL, FFN_TF), lambda i, f: (layer, i, 0, f)),
                  pl.BlockSpec((1, 1, D_MODEL, FFN_TF), lambda i, f: (layer, i, 0, f)),
                  pl.BlockSpec((1, 1, EXPERT_FF, D_MODEL), lambda i, f: (layer, i, 0, 0))],
        out_specs=[xspec(ROWS_CTX, D_MODEL), xspec(ROWS_LAT, D_MODEL)],
        out_shape=[jax.ShapeDtypeStruct((N_EXPERTS, ROWS_CTX, D_MODEL), BF16),
                   jax.ShapeDtypeStruct((N_EXPERTS, ROWS_LAT, D_MODEL), BF16)],
        scratch_shapes=[pltpu.VMEM((EXPERT_FF // FFN_TF, ROWS_CTX + ROWS_LAT, FFN_TF), BF16)],
        compiler_params=_cparams("parallel", "arbitrary"),
        name="expert_ffn",
    )(xe_c, xe_l, ge_c, ge_l, w_gate, w_up, w_down)


CMB_TN = 512


def _combine_body(cap, slot_ref, y_ref, x_ref, gt_ref, o_ref, acc):
    n = slot_ref.shape[0]
    slot = slot_ref[...]
    if cap < LANES:
        assert cap & (cap - 1) == 0
        width = N_EXPERTS * cap
        lane_expert = lax.shift_right_logical(lax.broadcasted_iota(jnp.int32, (LANES, width), 1),
                                              cap.bit_length() - 1)
        spread = (lax.broadcasted_iota(jnp.int32, (LANES, width), 0) == lane_expert).astype(BF16)
        slot_x = jnp.dot(slot.astype(F32).astype(BF16), spread, preferred_element_type=F32)
        target = (lax.broadcasted_iota(jnp.int32, (n, width), 1) & (cap - 1)).astype(F32)
        onehot = (slot_x == target).astype(BF16)
        acc[...] = jnp.dot(onehot, y_ref[...].reshape(width, y_ref.shape[2]), preferred_element_type=F32)
    else:
        col = lax.broadcasted_iota(jnp.int32, (n, cap), 1)
        for x in range(N_EXPERTS):
            onehot = (slot[:, x:x + 1] == col).astype(BF16)
            part = jnp.dot(onehot, y_ref[x], preferred_element_type=F32)
            if x == 0:
                acc[...] = part
            else:
                acc[...] += part
    o_ref[...] = x_ref[...] + gt_ref[0] * acc[...]


def _combine(slot_t, y, x1, gt_t, n_seq, seq_len, cap, row0, tn):
    s0 = row0 // seq_len
    return pl.pallas_call(
        functools.partial(_combine_body, cap),
        grid=(n_seq, D_MODEL // tn),
        in_specs=[pl.BlockSpec((seq_len, LANES), lambda b, j: (b, 0)),
                  pl.BlockSpec((N_EXPERTS, cap, tn), lambda b, j: (0, b, j)),
                  pl.BlockSpec((seq_len, tn), lambda b, j: (s0 + b, j)),
                  pl.BlockSpec((1, 1, tn), lambda b, j: (b, 0, j))],
        out_specs=pl.BlockSpec((seq_len, tn), lambda b, j: (b, j)),
        out_shape=jax.ShapeDtypeStruct((n_seq * seq_len, D_MODEL), F32),
        scratch_shapes=[pltpu.VMEM((seq_len, tn), F32)],
        compiler_params=_cparams("parallel", "arbitrary"),
        name="moe_combine",
    )(slot_t, y, x1, gt_t)


CONV_TN = 256
CONV_HALO = 8
DT_LANE = ROPE_DIM


def _conv_body(x_ref, w_ref, b_ref, o_ref):
    seq_len = x_ref.shape[0]
    halo = jnp.zeros((CONV_HALO, x_ref.shape[1]), F32)
    ext = jnp.concatenate([halo, x_ref[...].astype(F32), halo], axis=0)
    w = w_ref[...]
    y = b_ref[...]
    for k in range(SSM_CONV):
        lo = CONV_HALO - SSM_CONV // 2 + k
        y = y + w[k:k + 1, :] * ext[lo:lo + seq_len, :]
    o_ref[...] = (y * jax.nn.sigmoid(y)).astype(BF16)


def _conv_silu(zb, conv_w8, conv_b, n_seq, seq_len, row0, tn):
    s0 = row0 // seq_len
    c0 = ZB_XBC // tn
    return pl.pallas_call(
        _conv_body,
        grid=(n_seq, SSM_XBC // tn),
        in_specs=[pl.BlockSpec((seq_len, tn), lambda b, j: (s0 + b, c0 + j)),
                  pl.BlockSpec((8, tn), lambda b, j: (0, j)),
                  pl.BlockSpec((1, tn), lambda b, j: (0, j))],
        out_specs=pl.BlockSpec((seq_len, tn), lambda b, j: (b, j)),
        out_shape=jax.ShapeDtypeStruct((n_seq * seq_len, SSM_XBC), BF16),
        compiler_params=_cparams("parallel", "arbitrary"),
        name="ssm_conv",
    )(zb, conv_w8, conv_b)


def _split3(x):
    hi = x.astype(BF16)
    r = x - hi.astype(F32)
    mid = r.astype(BF16)
    return hi, mid, (r - mid.astype(F32)).astype(BF16)


def _ssd_body(nc, xa_ref, zk_ref, z_ref, h0_ref, bias_ref, a_ref, d_ref, g_ref, ef_ref, eb_ref,
              o_ref, st_ref, hb_in, hf_cur, hb_cur):
    q = SSM_CHUNK
    half = SSM_INNER // SSM_GROUPS
    ii = lax.broadcasted_iota(jnp.int32, (q, q), 0)
    jj = lax.broadcasted_iota(jnp.int32, (q, q), 1)
    lower = ii >= jj
    upper = ii <= jj
    lower_b = lower.astype(BF16)
    upper_b = upper.astype(BF16)
    lane = lax.broadcasted_iota(jnp.int32, (q, LANES), 1)
    is_dt = (lane >= DT_LANE) & (lane < DT_LANE + 2 * SSM_HEADS)
    is_fwd = lane < DT_LANE + SSM_HEADS

    def tri_cumsum(tri, v):
        hi, mid, lo = _split3(v)
        return (jnp.dot(tri, hi, preferred_element_type=F32) + jnp.dot(tri, mid, preferred_element_type=F32)
                + jnp.dot(tri, lo, preferred_element_type=F32))

    def expand(v, e_ref):
        hi, lo = _split_bf16(v)
        return (jnp.dot(hi, e_ref[...], preferred_element_type=F32)
                + jnp.dot(lo, e_ref[...], preferred_element_type=F32))

    def chunk_factors(r0):
        dt = jnp.where(is_dt, jax.nn.softplus(zk_ref[pl.ds(r0, q), :] + bias_ref[...]), 0.0)
        dta = dt * a_ref[...]
        cum = jnp.where(is_fwd, tri_cumsum(lower_b, dta), tri_cumsum(upper_b, dta))
        tot = jnp.sum(dta, axis=0, keepdims=True)
        return dt, cum, tot

    def state_update(r0, dend_x, cdec_x, h_prev):
        x = xa_ref[pl.ds(r0, q), 0:SSM_INNER].astype(F32)
        xs = (x * dend_x).astype(BF16)
        parts = []
        for g in range(SSM_GROUPS):
            lo = SSM_INNER + g * SSM_STATE
            b_t = xa_ref[pl.ds(r0, q), lo:lo + SSM_STATE].astype(F32).T.astype(BF16)
            parts.append(jnp.dot(b_t, xs[:, g * half:(g + 1) * half], preferred_element_type=F32))
        return cdec_x * h_prev + jnp.concatenate(parts, axis=1)

    hf_cur[...] = h0_ref[0, 0]
    hb_cur[...] = h0_ref[0, 1]

    def bwd_step(t, carry):
        c = nc - 1 - t
        r0 = pl.multiple_of(c * q, q)
        dt, cum, tot = chunk_factors(r0)
        hb_in[c] = hb_cur[...]
        dend_x = expand(jnp.exp(tot - cum) * dt, eb_ref)
        cdec_x = expand(jnp.broadcast_to(jnp.exp(tot), (8, LANES)), eb_ref)[0:1]
        hb_cur[...] = state_update(r0, dend_x, cdec_x, hb_cur[...])
        return carry

    lax.fori_loop(0, nc, bwd_step, 0)

    def fwd_step(c, carry):
        r0 = pl.multiple_of(c * q, q)
        dt, cum, tot = chunk_factors(r0)
        cum_t = cum.T
        dt_t = dt.T
        eoff = jnp.exp(cum)
        x_bf = xa_ref[pl.ds(r0, q), 0:SSM_INNER]
        cb = []
        c_bf = []
        for g in range(SSM_GROUPS):
            lo_b = SSM_INNER + g * SSM_STATE
            lo_c = SSM_INNER + (SSM_GROUPS + g) * SSM_STATE
            c_g = xa_ref[pl.ds(r0, q), lo_c:lo_c + SSM_STATE]
            b_g = xa_ref[pl.ds(r0, q), lo_b:lo_b + SSM_STATE]
            c_bf.append(c_g)
            cb.append(lax.dot_general(c_g, b_g, (((1,), (1,)), ((), ())), preferred_element_type=F32))

        def head_matrix(h):
            f = DT_LANE + h
            b = DT_LANE + SSM_HEADS + h
            lf = jnp.where(lower, jnp.exp(cum[:, f:f + 1] - cum_t[f:f + 1, :]), 0.0) * dt_t[f:f + 1, :]
            lb = jnp.where(upper, jnp.exp(cum[:, b:b + 1] - cum_t[b:b + 1, :]), 0.0) * dt_t[b:b + 1, :]
            return (cb[h // (SSM_HEADS // SSM_GROUPS)] * (lf + lb)).astype(BF16)

        lane_lo = lane < SSM_HEADDIM
        pairs = []
        for hp in range(SSM_HEADS // 2):
            x_pair = x_bf[:, hp * LANES:(hp + 1) * LANES]
            y0 = jnp.dot(head_matrix(2 * hp), x_pair, preferred_element_type=F32)
            y1 = jnp.dot(head_matrix(2 * hp + 1), x_pair, preferred_element_type=F32)
            pairs.append(jnp.where(lane_lo, y0, y1))
        y = jnp.concatenate(pairs, axis=1)

        def off_diag(h_t, factor_x):
            h_bf = h_t.astype(BF16)
            parts = [jnp.dot(c_bf[g], h_bf[:, g * half:(g + 1) * half], preferred_element_type=F32)
                     for g in range(SSM_GROUPS)]
            return jnp.concatenate(parts, axis=1) * factor_x

        y = y + off_diag(hf_cur[...], expand(eoff, ef_ref)) + off_diag(hb_in[c], expand(eoff, eb_ref))
        y = y + d_ref[...] * x_bf.astype(F32)
        zg = z_ref[pl.ds(r0, q), :].astype(F32)
        o_ref[pl.ds(r0, q), :] = _rms(y * (zg * jax.nn.sigmoid(zg)), g_ref[...]).astype(BF16)

        dend_x = expand(jnp.exp(tot - cum) * dt, ef_ref)
        cdec_x = expand(jnp.broadcast_to(jnp.exp(tot), (8, LANES)), ef_ref)[0:1]
        hf_cur[...] = state_update(r0, dend_x, cdec_x, hf_cur[...])
        return carry

    lax.fori_loop(0, nc, fwd_step, 0)
    st_ref[0, 0] = hf_cur[...]
    st_ref[0, 1] = hb_cur[...]


def _ssd(xa, zk, zb, h0_t, dt_bias_p, a_p, d_x, g_ssm, e_f, e_b, n_seq, seq_len, row0):
    s0 = row0 // seq_len
    nc = seq_len // SSM_CHUNK
    vec = lambda w: pl.BlockSpec((1, w), lambda b: (0, 0))
    return pl.pallas_call(
        functools.partial(_ssd_body, nc),
        grid=(n_seq,),
        in_specs=[pl.BlockSpec((seq_len, SSM_XBC), lambda b: (b, 0)),
                  pl.BlockSpec((seq_len, ZK_WIDTH), lambda b: (s0 + b, 0)),
                  pl.BlockSpec((seq_len, SSM_INNER), lambda b: (s0 + b, ZB_Z // SSM_INNER)),
                  pl.BlockSpec((1, 2, SSM_STATE, SSM_INNER), lambda b: (b, 0, 0, 0)),
                  vec(LANES), vec(LANES), vec(SSM_INNER), vec(SSM_INNER),
                  pl.BlockSpec((LANES, SSM_INNER), lambda b: (0, 0)),
                  pl.BlockSpec((LANES, SSM_INNER), lambda b: (0, 0))],
        out_specs=[pl.BlockSpec((seq_len, SSM_INNER), lambda b: (b, 0)),
                   pl.BlockSpec((1, 2, SSM_STATE, SSM_INNER), lambda b: (b, 0, 0, 0))],
        out_shape=[jax.ShapeDtypeStruct((n_seq * seq_len, SSM_INNER), BF16),
                   jax.ShapeDtypeStruct((n_seq, 2, SSM_STATE, SSM_INNER), F32)],
        scratch_shapes=[pltpu.VMEM((nc, SSM_STATE, SSM_INNER), F32),
                        pltpu.VMEM((SSM_STATE, SSM_INNER), F32),
                        pltpu.VMEM((SSM_STATE, SSM_INNER), F32)],
        compiler_params=_cparams("parallel"),
        name="ssd",
    )(xa, zk, zb, h0_t, dt_bias_p, a_p, d_x, g_ssm, e_f, e_b)


GM_TM = 512
GM_GROUP_W = GM_WIDTH // GM_GROUPS


def _gmlp_body(u_ref, v_ref, g_ref, w_ref, b_ref, o_ref):
    for r in range(0, GM_TM, GM_CHUNK):
        u = jax.nn.gelu(u_ref[r:r + GM_CHUNK, :].astype(F32))
        vg = _rms(jax.nn.gelu(v_ref[r:r + GM_CHUNK, :].astype(F32)), g_ref[...]).astype(BF16)
        sv = jnp.concatenate(
            [jnp.dot(w_ref[k], vg[:, k * GM_GROUP_W:(k + 1) * GM_GROUP_W], preferred_element_type=F32)
             for k in range(GM_GROUPS)], axis=1)
        o_ref[r:r + GM_CHUNK, :] = (u * (sv + b_ref[...])).astype(BF16)


def _gmlp(zb, g_gv, w_sp, b_x):
    return pl.pallas_call(
        _gmlp_body,
        grid=(N_TOK // GM_TM,),
        in_specs=[pl.BlockSpec((GM_TM, GM_WIDTH), lambda i: (i, ZB_GM // GM_WIDTH)),
                  pl.BlockSpec((GM_TM, GM_WIDTH), lambda i: (i, ZB_GM // GM_WIDTH + 1)),
                  pl.BlockSpec((1, GM_WIDTH), lambda i: (0, 0)),
                  pl.BlockSpec((GM_GROUPS, GM_CHUNK, GM_CHUNK), lambda i: (0, 0, 0)),
                  pl.BlockSpec((GM_CHUNK, GM_WIDTH), lambda i: (0, 0))],
        out_specs=pl.BlockSpec((GM_TM, GM_WIDTH), lambda i: (i, 0)),
        out_shape=jax.ShapeDtypeStruct((N_TOK, GM_WIDTH), BF16),
        compiler_params=_cparams("parallel"),
        name="gmlp",
    )(zb, zb, g_gv, w_sp, b_x)


FN_TM = 512


def _final_norm_body(x_ref, g_ref, o_ref):
    o_ref[...] = _rms(x_ref[...], g_ref[...])


def _final_norm(x, g):
    rows = x.shape[0]
    return pl.pallas_call(
        _final_norm_body,
        grid=(rows // FN_TM,),
        in_specs=[pl.BlockSpec((FN_TM, D_MODEL), lambda i: (i, 0)),
                  pl.BlockSpec((1, D_MODEL), lambda i: (0, 0))],
        out_specs=pl.BlockSpec((FN_TM, D_MODEL), lambda i: (i, 0)),
        out_shape=jax.ShapeDtypeStruct((rows, D_MODEL), F32),
        compiler_params=_cparams("parallel"),
        name="final_norm",
    )(x, g)


def _dt_lanes(v):
    return jnp.pad(v.reshape(1, 2 * SSM_HEADS).astype(F32),
                   ((0, 0), (DT_LANE, LANES - DT_LANE - 2 * SSM_HEADS)))


def _head_expanders():
    lane = lax.broadcasted_iota(jnp.int32, (LANES, SSM_INNER), 0)
    head = lax.broadcasted_iota(jnp.int32, (LANES, SSM_INNER), 1) // SSM_HEADDIM
    e_f = (lane == head + DT_LANE).astype(BF16)
    e_b = (lane == head + DT_LANE + SSM_HEADS).astype(BF16)
    return e_f, e_b


MOD_TN = 1024
MOD_ROWS = 8
N_COND = 1 + DEC_BATCH


def _mod_body(ct_ref, w_ref, b_ref, o_ref):
    c = ct_ref[...]
    act = c * jax.nn.sigmoid(c)
    sub = lax.broadcasted_iota(jnp.int32, (MOD_ROWS, MOD_TN), 0)
    out = jnp.zeros((MOD_ROWS, MOD_TN), F32)
    for r in range(N_COND):
        y = jnp.sum(act[:, r:r + 1] * w_ref[0], axis=0, keepdims=True)
        out = jnp.where(sub == r, y, out)
    o_ref[0] = out + b_ref[0]


def _modulation(cond_t, w_mod, b_mod):
    n = 6 * D_MODEL
    return pl.pallas_call(
        _mod_body,
        grid=(DEPTH, n // MOD_TN),
        in_specs=[pl.BlockSpec((D_MODEL, MOD_ROWS), lambda l, j: (0, 0)),
                  pl.BlockSpec((1, D_MODEL, MOD_TN), lambda l, j: (l, 0, j)),
                  pl.BlockSpec((1, 1, MOD_TN), lambda l, j: (l, 0, j))],
        out_specs=pl.BlockSpec((1, MOD_ROWS, MOD_TN), lambda l, j: (l, 0, j)),
        out_shape=jax.ShapeDtypeStruct((DEPTH, MOD_ROWS, n), F32),
        compiler_params=_cparams("parallel", "arbitrary"),
        name="modulation",
    )(cond_t, w_mod, b_mod[:, None, :])


def _prep_w_uq(w):
    w = w.reshape(Q_RANK, MLA_HEADS, NOPE_DIM + ROPE_DIM)
    pad = jnp.zeros((Q_RANK, MLA_HEADS, QK_HEAD - NOPE_DIM - ROPE_DIM), w.dtype)
    return jnp.concatenate([w, pad], axis=2).reshape(Q_RANK, MLA_HEADS * QK_HEAD).astype(BF16)


def _prep_w_ukv(w):
    w = w.reshape(KV_RANK, MLA_HEADS, NOPE_DIM + V_DIM)
    return jnp.concatenate([w[:, :, :NOPE_DIM].reshape(KV_RANK, MLA_WIDTH),
                            w[:, :, NOPE_DIM:].reshape(KV_RANK, MLA_WIDTH)], axis=1).astype(BF16)


def _rope_tables(n_lat):
    rows = n_lat // GRID_W
    row = jnp.repeat(jnp.arange(rows), GRID_W).astype(F32)
    col = jnp.tile(jnp.arange(GRID_W), rows).astype(F32)
    nf = ROPE_DIM // 4
    freqs = jnp.power(ROPE_THETA, -jnp.arange(nf, dtype=F32) / nf)
    ang = jnp.stack([row[:, None] * freqs, col[:, None] * freqs], axis=1)
    cos, sin = jnp.cos(ang), jnp.sin(ang)
    c64 = jnp.concatenate([cos[:, 0], cos[:, 0], cos[:, 1], cos[:, 1]], axis=1)
    s64 = jnp.concatenate([-sin[:, 0], sin[:, 0], -sin[:, 1], sin[:, 1]], axis=1)
    c_lat = jnp.concatenate([c64, jnp.ones((n_lat, LANES - ROPE_DIM), F32)], axis=1)
    s_lat = jnp.concatenate([s64, jnp.zeros((n_lat, LANES - ROPE_DIM), F32)], axis=1)
    rope_c = jnp.concatenate([jnp.ones((N_CTX, LANES), F32), jnp.tile(c_lat, (DEC_BATCH, 1))], axis=0)
    rope_s = jnp.concatenate([jnp.zeros((N_CTX, LANES), F32), jnp.tile(s_lat, (DEC_BATCH, 1))], axis=0)
    return rope_c, rope_s


def _rows_mod(vec3, tm):
    idx = [0] * (N_CTX // tm) + [1] * (DEC_SEQ // tm) + [2] * (DEC_SEQ // tm)
    return vec3[jnp.array(idx)][:, None, :]


def kernel(x_prompt, x_sample, c, cache_ckv, cache_krope, state_ssm, c_ctx, w_mod, b_mod,
           g_norm1, g_norm2, w_in, g_qn, w_uq, g_kvn, w_ukv, conv_w, conv_b, dt_bias, a_log,
           d_skip, g_ssm, g_gv, w_sp, b_sp, w_br_attn, w_br_ssm, w_br_gmlp, w_out, w_router,
           w_gate, w_up, w_down, g_final):
    rope_c, rope_s = _rope_tables(DEC_SEQ)
    x_c = x_prompt.reshape(N_CTX, D_MODEL)
    x_l = x_sample.reshape(N_LAT, D_MODEL)
    cond_t = jnp.pad(jnp.concatenate([c_ctx[None, :], c], axis=0).T, ((0, 0), (0, MOD_ROWS - N_COND)))
    mod_all = _modulation(cond_t, w_mod, b_mod)
    e_f, e_b = _head_expanders()
    w_in_t = jnp.swapaxes(w_in, 1, 2)
    h0_c = jnp.zeros((BATCH, 2, SSM_STATE, SSM_INNER), F32)

    ckvs, krs, sts = [], [], []
    for l in range(DEPTH):
        mod = mod_all[l, :1 + DEC_BATCH].reshape(1 + DEC_BATCH, 6, D_MODEL)
        sh1, sc1, gt1, sh2, sc2, gt2 = [mod[:, k] for k in range(6)]

        zb, ckv_raw, zk = _inproj(x_c, x_l, g_norm1[l][None, :], _rows_mod(sc1, IN_TM), _rows_mod(sh1, IN_TM),
                                  w_in_t, l)

        q = _qproj(zb, g_qn[l][None, :], _prep_w_uq(w_uq[l]), rope_c, rope_s)
        w_ukv_p = _prep_w_ukv(w_ukv[l])
        ckv, kv, krot = _kvproj(ckv_raw, zk, g_kvn[l][None, :], w_ukv_p, rope_c, rope_s)
        kv_cache = _matmul(cache_ckv[:, l].reshape(DEC_BATCH * PAST_LEN, KV_RANK), w_ukv_p,
                           out_dtype=BF16, tn=2 * MLA_WIDTH, name="mla_kv_cache")
        kr_cache = jnp.pad(cache_krope[:, l].reshape(DEC_BATCH * PAST_LEN, ROPE_DIM),
                           ((0, 0), (0, LANES - ROPE_DIM))).astype(BF16)
        attn_c = _attention(q, kv, krot, BATCH, SEQ, SEQ, 0)
        attn_l = _attention(q, kv, krot, DEC_BATCH, DEC_SEQ, 512, N_CTX, cache=(kv_cache, kr_cache, PAST_LEN))

        conv_w8 = jnp.pad(conv_w[l], ((0, 8 - SSM_CONV), (0, 0)))
        ssd_par = (_dt_lanes(dt_bias[l]), _dt_lanes(-jnp.exp(a_log[l])),
                   jnp.repeat(d_skip[l], SSM_HEADDIM)[None, :], g_ssm[l][None, :], e_f, e_b)
        xa_c = _conv_silu(zb, conv_w8, conv_b[l][None, :], BATCH, SEQ, 0, SSM_XBC)
        xa_l = _conv_silu(zb, conv_w8, conv_b[l][None, :], DEC_BATCH, DEC_SEQ, N_CTX, CONV_TN)
        h0_l = jnp.transpose(state_ssm[:, l], (0, 1, 4, 2, 3)).reshape(DEC_BATCH, 2, SSM_STATE, SSM_INNER)
        ssm_c, st_c = _ssd(xa_c, zk, zb, h0_c, *ssd_par, BATCH, SEQ, 0)
        ssm_l, _ = _ssd(xa_l, zk, zb, h0_l, *ssd_par, DEC_BATCH, DEC_SEQ, N_CTX)
        gmo = _gmlp(zb, g_gv[l][None, :], w_sp[l].astype(BF16), jnp.repeat(b_sp[l].T, GM_GROUP_W, axis=1))
        st_c = jnp.transpose(st_c.reshape(BATCH, 2, SSM_STATE, SSM_HEADS, SSM_HEADDIM), (0, 1, 3, 4, 2))

        merged = _merge(attn_c, attn_l, ssm_c, ssm_l, gmo, w_br_attn, w_br_ssm, w_br_gmlp, zb, l)
        w_router_p = jnp.pad(w_router[l], ((0, 0), (0, LANES - N_EXPERTS)))
        x1, h2, logits = _outproj(merged, w_out[l].astype(BF16), x_c, x_l, _rows_mod(gt1, OUT_TM),
                                  g_norm2[l][None, :], _rows_mod(sc2, OUT_TM), _rows_mod(sh2, OUT_TM),
                                  w_router_p)

        slot_c, srow_c, arow_c = _select(logits, 0, BATCH, SEQ, CAP_CTX)
        slot_l, srow_l, arow_l = _select(logits, 1, DEC_BATCH, DEC_SEQ, CAP_LAT)
        xe_c, ge_c = _gather(srow_c, arow_c, h2, BATCH, SEQ, CAP_CTX, 0, N_EXPERTS)
        xe_l, ge_l = _gather(srow_l, arow_l, h2, DEC_BATCH, DEC_SEQ, CAP_LAT, N_CTX, 2)
        y_c, y_l = _expert_ffn(xe_c, xe_l, ge_c, ge_l, w_gate, w_up, w_down, l)
        gt2_c = jnp.broadcast_to(gt2[0][None, None, :], (BATCH, 1, D_MODEL))
        gt2_l = gt2[1:][:, None, :]
        x_c = _combine(slot_c, y_c, x1, gt2_c, BATCH, SEQ, CAP_CTX, 0, D_MODEL)
        x_l = _combine(slot_l, y_l, x1, gt2_l, DEC_BATCH, DEC_SEQ, CAP_LAT, N_CTX, CMB_TN)

        ckvs.append(ckv[:N_CTX].reshape(BATCH, SEQ, KV_RANK))
        krs.append(zk[:N_CTX, :ROPE_DIM].reshape(BATCH, SEQ, ROPE_DIM))
        sts.append(st_c)

    y_prompt = _final_norm(x_c, g_final[None, :]).reshape(BATCH, SEQ, D_MODEL)
    y_sample = _final_norm(x_l, g_final[None, :]).reshape(DEC_BATCH, DEC_SEQ, D_MODEL)
    return (y_prompt, y_sample, jnp.stack(ckvs, axis=1), jnp.stack(krs, axis=1), jnp.stack(sts, axis=1))
```

```python
import functools
import math

import jax
import jax.numpy as jnp
from jax import lax
from jax.experimental import pallas as pl
from jax.experimental.pallas import tpu as pltpu

D_MODEL = 2048
BATCH = 16
SEQ = 256
DEPTH = 2
DEC_BATCH = 2
DEC_SEQ = 2048
PAST_LEN = 512
GRID_W = 64
ROPE_THETA = 10000.0
EPS = 1e-6
MLA_HEADS = 8
Q_RANK = 512
KV_RANK = 256
NOPE_DIM = 128
ROPE_DIM = 64
V_DIM = 128
MLA_WIDTH = MLA_HEADS * V_DIM
SSM_HEADS = 16
SSM_HEADDIM = 64
SSM_INNER = SSM_HEADS * SSM_HEADDIM
SSM_GROUPS = 2
SSM_STATE = 128
SSM_CONV = 5
SSM_CHUNK = 128
SSM_XBC = SSM_INNER + 2 * SSM_GROUPS * SSM_STATE
GM_WIDTH = 1024
GM_GROUPS = 4
GM_CHUNK = 128
N_BRANCH = 3
N_EXPERTS = 16
EXPERT_FF = 1024
EC_CAPACITY = 2

N_CTX = BATCH * SEQ
N_LAT = DEC_BATCH * DEC_SEQ
N_TOK = N_CTX + N_LAT
CAP_CTX = EC_CAPACITY * SEQ // N_EXPERTS
CAP_LAT = EC_CAPACITY * DEC_SEQ // N_EXPERTS
ROWS_CTX = BATCH * CAP_CTX
ROWS_LAT = DEC_BATCH * CAP_LAT

LANES = 128
QK_HEAD = 2 * LANES

ZB_Z = 0
ZB_CQ = SSM_INNER
ZB_XBC = ZB_CQ + Q_RANK
ZB_GM = ZB_XBC + SSM_XBC
ZB_GATE = ZB_GM + 2 * GM_WIDTH
ZB_WIDTH = ZB_GATE + N_BRANCH * D_MODEL
ZK_WIDTH = LANES

V7X_VMEM_LIMIT_BYTES = 56 * 1024 * 1024

BF16 = jnp.bfloat16
F32 = jnp.float32


V7X_VMEM_LIMIT_LARGE_BYTES = 60 * 1024 * 1024


def _cparams(*sem, large=False):
    limit = V7X_VMEM_LIMIT_LARGE_BYTES if large else V7X_VMEM_LIMIT_BYTES
    return pltpu.CompilerParams(dimension_semantics=sem, vmem_limit_bytes=limit)


def _rms(x, g):
    return x * lax.rsqrt(jnp.mean(x * x, axis=-1, keepdims=True) + EPS) * g


def _swap16(x):
    lane = lax.broadcasted_iota(jnp.int32, x.shape, 1)
    return jnp.where((lane % 32) < 16, pltpu.roll(x, LANES - 16, 1), pltpu.roll(x, 16, 1))


def _mm_body(x_ref, w_ref, o_ref):
    o_ref[...] = jnp.dot(x_ref[...].astype(BF16), w_ref[...].astype(BF16),
                         preferred_element_type=F32).astype(o_ref.dtype)


def _matmul(x, w, out_dtype=F32, tm=512, tn=512, name="matmul"):
    m, k = x.shape
    _, n = w.shape
    tn = min(tn, n)
    return pl.pallas_call(
        _mm_body,
        grid=(m // tm, n // tn),
        in_specs=[pl.BlockSpec((tm, k), lambda i, j: (i, 0)),
                  pl.BlockSpec((k, tn), lambda i, j: (0, j))],
        out_specs=pl.BlockSpec((tm, tn), lambda i, j: (i, j)),
        out_shape=jax.ShapeDtypeStruct((m, n), out_dtype),
        compiler_params=_cparams("parallel", "arbitrary"),
        name=name,
    )(x, w)


IN_TM = 1024
IN_TN = 512
NORM_ROWS = 256


def _ctx_lat_specs(tm, width, n_grid_axes):
    del n_grid_axes
    n_ctx = N_CTX // tm
    return [pl.BlockSpec((tm, width), lambda i, *_: (jnp.minimum(i, n_ctx - 1), 0)),
            pl.BlockSpec((tm, width), lambda i, *_: (jnp.maximum(i - n_ctx, 0), 0))]


def _ctx_or_lat(tm, c_ref, l_ref, rows=slice(None)):
    return jnp.where(pl.program_id(0) < N_CTX // tm, c_ref[rows, :], l_ref[rows, :])


def _norm_mod_to(h_scr, xc_ref, xl_ref, g_ref, sc_ref, sh_ref):
    g = g_ref[...]
    mul = 1.0 + sc_ref[0]
    add = sh_ref[0]
    for r in range(0, IN_TM, NORM_ROWS):
        x = _ctx_or_lat(IN_TM, xc_ref, xl_ref, slice(r, r + NORM_ROWS))
        h_scr[r:r + NORM_ROWS, :] = (_rms(x, g) * mul + add).astype(BF16)


W_IN_SEGS = {}
_off = 0
for _name, _w in (("cq", Q_RANK), ("ckv", KV_RANK), ("kr", ROPE_DIM), ("z", SSM_INNER), ("xbc", SSM_XBC),
                  ("dt", 2 * SSM_HEADS), ("gm", 2 * GM_WIDTH), ("gates", N_BRANCH * D_MODEL)):
    W_IN_SEGS[_name] = (_off, _w)
    _off += _w
ROW_UNIT = 32
ZB_TILE_STARTS = [(W_IN_SEGS[_name][0] + _k) // ROW_UNIT
                  for _name in ("z", "cq", "xbc", "gm", "gates")
                  for _k in range(0, W_IN_SEGS[_name][1], IN_TN)]
NT_DIMS = (((1,), (1,)), ((), ()))


def _inproj_body(tile_ref, xc_ref, xl_ref, g_ref, sc_ref, sh_ref, w_ref, wckv_ref, wkr_ref, wdt_ref,
                 o_ref, ckv_ref, zk_ref, h_scr, wzk_scr):
    del tile_ref

    @pl.when(pl.program_id(1) == 0)
    def _():
        _norm_mod_to(h_scr, xc_ref, xl_ref, g_ref, sc_ref, sh_ref)
        ckv_ref[...] = lax.dot_general(h_scr[...], wckv_ref[0].astype(BF16), NT_DIMS,
                                       preferred_element_type=F32)
        n_dt = 2 * SSM_HEADS
        wzk_scr[0:ROPE_DIM, :] = wkr_ref[0].astype(BF16)
        wzk_scr[ROPE_DIM:ROPE_DIM + n_dt, :] = wdt_ref[0].astype(BF16)
        wzk_scr[ROPE_DIM + n_dt:, :] = jnp.zeros((ZK_WIDTH - ROPE_DIM - n_dt, D_MODEL), BF16)
        zk_ref[...] = lax.dot_general(h_scr[...], wzk_scr[...], NT_DIMS, preferred_element_type=F32)

    o_ref[...] = lax.dot_general(h_scr[...], w_ref[0].astype(BF16), NT_DIMS,
                                 preferred_element_type=F32).astype(o_ref.dtype)


def _w_rows(layer, name):
    start, width = W_IN_SEGS[name]
    return pl.BlockSpec((pl.Element(1), pl.Element(width), pl.Element(D_MODEL)),
                        lambda i, j, tile: (layer, start, 0))


def _inproj(x_c, x_l, g, sc_t, sh_t, w_in_t, layer):
    nt = N_TOK // IN_TM
    common = _ctx_lat_specs(IN_TM, D_MODEL, 2) + [
              pl.BlockSpec((1, D_MODEL), lambda i, j, *_: (0, 0)),
              pl.BlockSpec((1, 1, D_MODEL), lambda i, j, *_: (i, 0, 0)),
              pl.BlockSpec((1, 1, D_MODEL), lambda i, j, *_: (i, 0, 0))]
    tile_rows = pl.BlockSpec((pl.Element(1), pl.Element(IN_TN), pl.Element(D_MODEL)),
                             lambda i, j, tile: (layer, tile[j] * ROW_UNIT, 0))
    return pl.pallas_call(
        _inproj_body,
        grid_spec=pltpu.PrefetchScalarGridSpec(
            num_scalar_prefetch=1,
            grid=(nt, len(ZB_TILE_STARTS)),
            in_specs=common + [tile_rows, _w_rows(layer, "ckv"), _w_rows(layer, "kr"), _w_rows(layer, "dt")],
            out_specs=[pl.BlockSpec((IN_TM, IN_TN), lambda i, j, tile: (i, j)),
                       pl.BlockSpec((IN_TM, KV_RANK), lambda i, j, tile: (i, 0)),
                       pl.BlockSpec((IN_TM, ZK_WIDTH), lambda i, j, tile: (i, 0))],
            scratch_shapes=[pltpu.VMEM((IN_TM, D_MODEL), BF16), pltpu.VMEM((ZK_WIDTH, D_MODEL), BF16)]),
        out_shape=[jax.ShapeDtypeStruct((N_TOK, ZB_WIDTH), BF16),
                   jax.ShapeDtypeStruct((N_TOK, KV_RANK), F32),
                   jax.ShapeDtypeStruct((N_TOK, ZK_WIDTH), F32)],
        compiler_params=_cparams("parallel", "arbitrary", large=True),
        name="in_proj",
    )(jnp.asarray(ZB_TILE_STARTS, jnp.int32), x_c, x_l, g, sc_t, sh_t, w_in_t, w_in_t, w_in_t, w_in_t)


QKV_TM = 512
ATTN_SCALE = 1.0 / math.sqrt(NOPE_DIM + ROPE_DIM)


def _qproj_body(cq_ref, g_ref, w_ref, c_ref, s_ref, o_ref):
    qn = _rms(cq_ref[...].astype(F32), g_ref[...]).astype(BF16)
    q = jnp.dot(qn, w_ref[...], preferred_element_type=F32)
    c = c_ref[...]
    s = s_ref[...]
    for h in range(MLA_HEADS):
        lo = h * QK_HEAD
        r = q[:, lo + LANES:lo + QK_HEAD]
        o_ref[:, lo:lo + LANES] = (q[:, lo:lo + LANES] * ATTN_SCALE).astype(BF16)
        o_ref[:, lo + LANES:lo + QK_HEAD] = ((r * c + _swap16(r) * s) * ATTN_SCALE).astype(BF16)


def _qproj(zb, g_qn, w_uq_p, rope_c, rope_s):
    return pl.pallas_call(
        _qproj_body,
        grid=(N_TOK // QKV_TM,),
        in_specs=[pl.BlockSpec((QKV_TM, Q_RANK), lambda i: (i, ZB_CQ // Q_RANK)),
                  pl.BlockSpec((1, Q_RANK), lambda i: (0, 0)),
                  pl.BlockSpec((Q_RANK, MLA_HEADS * QK_HEAD), lambda i: (0, 0)),
                  pl.BlockSpec((QKV_TM, LANES), lambda i: (i, 0)),
                  pl.BlockSpec((QKV_TM, LANES), lambda i: (i, 0))],
        out_specs=pl.BlockSpec((QKV_TM, MLA_HEADS * QK_HEAD), lambda i: (i, 0)),
        out_shape=jax.ShapeDtypeStruct((N_TOK, MLA_HEADS * QK_HEAD), BF16),
        compiler_params=_cparams("parallel"),
        name="mla_q",
    )(zb, g_qn, w_uq_p, rope_c, rope_s)


def _kvproj_body(ckv_ref, zk_ref, g_ref, w_ref, c_ref, s_ref, ckv_o, kv_o, kr_o):
    ckv = _rms(ckv_ref[...], g_ref[...])
    ckv_o[...] = ckv
    kv_o[...] = jnp.dot(ckv.astype(BF16), w_ref[...], preferred_element_type=F32).astype(BF16)
    zk = zk_ref[...]
    is_rope = lax.broadcasted_iota(jnp.int32, zk.shape, 1) < ROPE_DIM
    kr = jnp.where(is_rope, zk, 0.0)
    rot = kr * c_ref[...] + _swap16(kr) * s_ref[...]
    kr_o[...] = jnp.where(is_rope, rot, 0.0).astype(BF16)


def _kvproj(ckv_raw, zk, g_kvn, w_ukv_p, rope_c, rope_s):
    return pl.pallas_call(
        _kvproj_body,
        grid=(N_TOK // QKV_TM,),
        in_specs=[pl.BlockSpec((QKV_TM, KV_RANK), lambda i: (i, 0)),
                  pl.BlockSpec((QKV_TM, ZK_WIDTH), lambda i: (i, 0)),
                  pl.BlockSpec((1, KV_RANK), lambda i: (0, 0)),
                  pl.BlockSpec((KV_RANK, 2 * MLA_WIDTH), lambda i: (0, 0)),
                  pl.BlockSpec((QKV_TM, LANES), lambda i: (i, 0)),
                  pl.BlockSpec((QKV_TM, LANES), lambda i: (i, 0))],
        out_specs=[pl.BlockSpec((QKV_TM, KV_RANK), lambda i: (i, 0)),
                   pl.BlockSpec((QKV_TM, 2 * MLA_WIDTH), lambda i: (i, 0)),
                   pl.BlockSpec((QKV_TM, LANES), lambda i: (i, 0))],
        out_shape=[jax.ShapeDtypeStruct((N_TOK, KV_RANK), F32),
                   jax.ShapeDtypeStruct((N_TOK, 2 * MLA_WIDTH), BF16),
                   jax.ShapeDtypeStruct((N_TOK, LANES), BF16)],
        compiler_params=_cparams("parallel"),
        name="mla_kv",
    )(ckv_raw, zk, g_kvn, w_ukv_p, rope_c, rope_s)


def _attn_body(n_parts, q_ref, *refs):
    o_ref = refs[3 * n_parts]
    for h in range(MLA_HEADS):
        qh = q_ref[:, h * QK_HEAD:(h + 1) * QK_HEAD]
        scores = []
        for p in range(n_parts):
            kn_ref, kr_ref = refs[3 * p], refs[3 * p + 1]
            kh = jnp.concatenate([kn_ref[:, h * LANES:(h + 1) * LANES], kr_ref[...]], axis=1)
            scores.append(lax.dot_general(qh, kh, (((1,), (1,)), ((), ())),
                                          preferred_element_type=F32))
        m = scores[0].max(axis=1, keepdims=True)
        for s in scores[1:]:
            m = jnp.maximum(m, s.max(axis=1, keepdims=True))
        den = 0.0
        acc = 0.0
        for p in range(n_parts):
            e = jnp.exp(scores[p] - m)
            den = den + e.sum(axis=1, keepdims=True)
            v_ref = refs[3 * p + 2]
            acc = acc + jnp.dot(e.astype(BF16), v_ref[:, h * LANES:(h + 1) * LANES],
                                preferred_element_type=F32)
        o_ref[:, h * LANES:(h + 1) * LANES] = (acc / den).astype(BF16)


def _attention(q, kv, kr, n_seq, seq_len, tq, row0, cache=None):
    nq = seq_len // tq
    q0 = row0 // tq
    s0 = row0 // seq_len
    in_specs = [pl.BlockSpec((tq, MLA_HEADS * QK_HEAD), lambda b, i: (q0 + b * nq + i, 0))]
    args = [q]
    if cache is not None:
        kv_c, kr_c, len_c = cache
        in_specs += [pl.BlockSpec((len_c, MLA_WIDTH), lambda b, i: (b, 0)),
                     pl.BlockSpec((len_c, LANES), lambda b, i: (b, 0)),
                     pl.BlockSpec((len_c, MLA_WIDTH), lambda b, i: (b, 1))]
        args += [kv_c, kr_c, kv_c]
    in_specs += [pl.BlockSpec((seq_len, MLA_WIDTH), lambda b, i: (s0 + b, 0)),
                 pl.BlockSpec((seq_len, LANES), lambda b, i: (s0 + b, 0)),
                 pl.BlockSpec((seq_len, MLA_WIDTH), lambda b, i: (s0 + b, 1))]
    args += [kv, kr, kv]
    n_parts = 1 if cache is None else 2
    return pl.pallas_call(
        functools.partial(_attn_body, n_parts),
        grid=(n_seq, nq),
        in_specs=in_specs,
        out_specs=pl.BlockSpec((tq, MLA_WIDTH), lambda b, i: (b * nq + i, 0)),
        out_shape=jax.ShapeDtypeStruct((n_seq * seq_len, MLA_WIDTH), BF16),
        compiler_params=_cparams("parallel", "arbitrary", large=cache is not None),
        name="mla_attn_cache" if cache is not None else "mla_attn",
    )(*args)


MG_TM = 1024
MG_TN = 512
OUT_TM = 512


def _merge_body(ac_ref, al_ref, sc_ref, sl_ref, c_ref, wa_ref, ws_ref, wc_ref, ga_ref, gs_ref, gc_ref,
                o_ref):
    def branch(x, w_ref, gate_ref):
        y = jnp.dot(x, w_ref[0].astype(BF16), preferred_element_type=F32)
        return jax.nn.sigmoid(gate_ref[...].astype(F32)) * y

    o_ref[...] = (branch(_ctx_or_lat(MG_TM, ac_ref, al_ref), wa_ref, ga_ref)
                  + branch(_ctx_or_lat(MG_TM, sc_ref, sl_ref), ws_ref, gs_ref)
                  + branch(c_ref[...], wc_ref, gc_ref)).astype(BF16)


def _merge(attn_c, attn_l, ssm_c, ssm_l, gmo, wa, ws, wc, zb, layer):
    g0 = ZB_GATE // MG_TN
    gstep = D_MODEL // MG_TN
    pair = _ctx_lat_specs(MG_TM, MLA_WIDTH, 2)
    wspec = pl.BlockSpec((1, MLA_WIDTH, MG_TN), lambda i, j: (layer, 0, j))
    return pl.pallas_call(
        _merge_body,
        grid=(N_TOK // MG_TM, gstep),
        in_specs=pair + pair + [pl.BlockSpec((MG_TM, GM_WIDTH), lambda i, j: (i, 0)),
                                wspec, wspec, wspec,
                                pl.BlockSpec((MG_TM, MG_TN), lambda i, j: (i, g0 + j)),
                                pl.BlockSpec((MG_TM, MG_TN), lambda i, j: (i, g0 + gstep + j)),
                                pl.BlockSpec((MG_TM, MG_TN), lambda i, j: (i, g0 + 2 * gstep + j))],
        out_specs=pl.BlockSpec((MG_TM, MG_TN), lambda i, j: (i, j)),
        out_shape=jax.ShapeDtypeStruct((N_TOK, D_MODEL), BF16),
        compiler_params=_cparams("parallel", "arbitrary"),
        name="branch_merge",
    )(attn_c, attn_l, ssm_c, ssm_l, gmo, wa, ws, wc, zb, zb, zb)


def _split_bf16(x):
    hi = x.astype(BF16)
    return hi, (x - hi.astype(F32)).astype(BF16)


def _outproj_body(m_ref, w_ref, xc_ref, xl_ref, gt_ref, g_ref, sc_ref, sh_ref, wr_ref,
                  x1_ref, h2_ref, lg_ref):
    mix = jnp.dot(m_ref[...], w_ref[...], preferred_element_type=F32)
    x1 = _ctx_or_lat(OUT_TM, xc_ref, xl_ref) + gt_ref[0] * mix
    x1_ref[...] = x1
    h2 = _rms(x1, g_ref[...]) * (1.0 + sc_ref[0]) + sh_ref[0]
    h2_ref[...] = h2.astype(BF16)
    h_hi, h_lo = _split_bf16(h2)
    w_hi, w_lo = _split_bf16(wr_ref[...])
    lg_ref[...] = (jnp.dot(h_hi, w_hi, preferred_element_type=F32)
                   + jnp.dot(h_lo, w_hi, preferred_element_type=F32)
                   + jnp.dot(h_hi, w_lo, preferred_element_type=F32))


def _outproj(merged, w_out, x_c, x_l, gt_t, g2, sc_t, sh_t, w_router_p):
    mspec = pl.BlockSpec((1, 1, D_MODEL), lambda i: (i, 0, 0))
    return pl.pallas_call(
        _outproj_body,
        grid=(N_TOK // OUT_TM,),
        in_specs=[pl.BlockSpec((OUT_TM, D_MODEL), lambda i: (i, 0)),
                  pl.BlockSpec((D_MODEL, D_MODEL), lambda i: (0, 0), pipeline_mode=pl.Buffered(1))]
                 + _ctx_lat_specs(OUT_TM, D_MODEL, 1) + [
                  mspec,
                  pl.BlockSpec((1, D_MODEL), lambda i: (0, 0)),
                  mspec, mspec,
                  pl.BlockSpec((D_MODEL, LANES), lambda i: (0, 0))],
        out_specs=[pl.BlockSpec((OUT_TM, D_MODEL), lambda i: (i, 0)),
                   pl.BlockSpec((OUT_TM, D_MODEL), lambda i: (i, 0)),
                   pl.BlockSpec((OUT_TM, LANES), lambda i: (i, 0))],
        out_shape=[jax.ShapeDtypeStruct((N_TOK, D_MODEL), F32),
                   jax.ShapeDtypeStruct((N_TOK, D_MODEL), BF16),
                   jax.ShapeDtypeStruct((N_TOK, LANES), F32)],
        compiler_params=_cparams("parallel", large=True),
        name="out_proj",
    )(merged, w_out, x_c, x_l, gt_t, g2, sc_t, sh_t, w_router_p)


PREFIX_BLK = 256
EXP_SEARCH_STEPS = 7
BISECT_STEPS = 40


def _prefix_count(mask):
    n = mask.shape[1]
    upper = (lax.broadcasted_iota(jnp.int32, (PREFIX_BLK, PREFIX_BLK), 0)
             < lax.broadcasted_iota(jnp.int32, (PREFIX_BLK, PREFIX_BLK), 1)).astype(BF16)
    run = jnp.zeros((mask.shape[0], 1), F32)
    outs = []
    for k in range(0, n, PREFIX_BLK):
        blk = mask[:, k:k + PREFIX_BLK]
        outs.append(jnp.dot(blk.astype(BF16), upper, preferred_element_type=F32) + run)
        run = run + jnp.sum(blk, axis=1, keepdims=True)
    return outs[0] if len(outs) == 1 else jnp.concatenate(outs, axis=1)


def _select_body(n_seq, seq_len, cap, lg_ref, slot_t_ref, slot_ref, aff_ref):
    lane = lax.broadcasted_iota(jnp.int32, (seq_len, LANES), 1)
    rows = []
    for s in range(n_seq):
        lg = jnp.where(lane < N_EXPERTS, lg_ref[s * seq_len:(s + 1) * seq_len, :], -jnp.inf)
        e = jnp.exp(lg - lg.max(axis=1, keepdims=True))
        aff = e / e.sum(axis=1, keepdims=True)
        rows.append(aff.T[:N_EXPERTS, :])
    a = rows[0] if n_seq == 1 else jnp.concatenate(rows, axis=0)

    def count_ge(t):
        return jnp.sum(jnp.where(a >= t, 1.0, 0.0), axis=1, keepdims=True)

    hi = jnp.full((a.shape[0], 1), 2.0, F32)
    for i in reversed(range(EXP_SEARCH_STEPS)):
        cand = hi * (2.0 ** -(2 ** i))
        hi = jnp.where(count_ge(cand) < cap, cand, hi)
    lo = jnp.where(hi <= 2.0 ** -126, 0.0, 0.5 * hi)
    for _ in range(BISECT_STEPS):
        mid = 0.5 * (lo + hi)
        ge = count_ge(mid) >= cap
        lo = jnp.where(ge, mid, lo)
        hi = jnp.where(ge, hi, mid)
    above = jnp.where(a >= hi, 1.0, 0.0)
    tied = jnp.where((a >= lo) & (a < hi), 1.0, 0.0)
    need = cap - jnp.sum(above, axis=1, keepdims=True)
    sel = above + tied * jnp.where(_prefix_count(tied) < need, 1.0, 0.0)
    slot = jnp.where(sel > 0.0, _prefix_count(sel), -1.0)
    slot_ref[...] = slot
    aff_ref[...] = a
    pad = jnp.full((LANES - N_EXPERTS, seq_len), -1.0, F32)
    for s in range(n_seq):
        blk = jnp.concatenate([slot[s * N_EXPERTS:(s + 1) * N_EXPERTS, :], pad], axis=0)
        slot_t_ref[s * seq_len:(s + 1) * seq_len, :] = blk.T.astype(jnp.int32)


def _select(logits, group, n_seq, seq_len, cap):
    rows = n_seq * N_EXPERTS
    n_rows = n_seq * seq_len
    return pl.pallas_call(
        functools.partial(_select_body, n_seq, seq_len, cap),
        grid=(1,),
        in_specs=[pl.BlockSpec((n_rows, LANES), lambda i: (group, 0))],
        out_specs=[pl.BlockSpec((n_rows, LANES), lambda i: (0, 0)),
                   pl.BlockSpec((rows, seq_len), lambda i: (0, 0)),
                   pl.BlockSpec((rows, seq_len), lambda i: (0, 0))],
        out_shape=[jax.ShapeDtypeStruct((n_rows, LANES), jnp.int32),
                   jax.ShapeDtypeStruct((rows, seq_len), F32),
                   jax.ShapeDtypeStruct((rows, seq_len), F32)],
        compiler_params=_cparams("arbitrary"),
        name="moe_select",
    )(logits)


def _gather_body(cap, epb, slot_ref, aff_ref, h_ref, xe_ref, ge_ref):
    n = h_ref.shape[0]
    if cap < LANES and epb == N_EXPERTS:
        assert cap & (cap - 1) == 0
        width = N_EXPERTS * cap
        row_expert = lax.shift_right_logical(lax.broadcasted_iota(jnp.int32, (width, N_EXPERTS), 0),
                                             cap.bit_length() - 1)
        spread = (lax.broadcasted_iota(jnp.int32, (width, N_EXPERTS), 1) == row_expert).astype(BF16)
        slot_x = jnp.dot(spread, slot_ref[...].astype(BF16), preferred_element_type=F32)
        target = (lax.broadcasted_iota(jnp.int32, (width, n), 0) & (cap - 1)).astype(F32)
        onehot = slot_x == target
        xe = jnp.dot(onehot.astype(BF16), h_ref[...], preferred_element_type=F32).astype(BF16)
        xe_ref[...] = xe.reshape(N_EXPERTS, cap, xe.shape[1])
        aff_x = sum(jnp.dot(spread, part, preferred_element_type=F32) for part in _split3(aff_ref[...]))
        gate = jnp.sum(jnp.where(onehot, aff_x, 0.0), axis=1, keepdims=True)
        ge_ref[...] = jnp.broadcast_to(gate, (width, LANES)).reshape(N_EXPERTS, cap, LANES)
        return
    step = pl.program_id(1)
    row = lax.broadcasted_iota(jnp.int32, (cap, n), 0).astype(F32)
    for k in range(epb):
        x = step * epb + k
        onehot = slot_ref[pl.ds(x, 1), :] == row
        xe_ref[k] = jnp.dot(onehot.astype(BF16), h_ref[...], preferred_element_type=F32).astype(BF16)
        gate = jnp.sum(jnp.where(onehot, aff_ref[pl.ds(x, 1), :], 0.0), axis=1, keepdims=True)
        ge_ref[k] = jnp.broadcast_to(gate, (cap, LANES))


def _gather(slot, aff, h2, n_seq, seq_len, cap, row0, epb):
    s0 = row0 // seq_len
    return pl.pallas_call(
        functools.partial(_gather_body, cap, epb),
        grid=(n_seq, N_EXPERTS // epb),
        in_specs=[pl.BlockSpec((N_EXPERTS, seq_len), lambda b, x: (b, 0)),
                  pl.BlockSpec((N_EXPERTS, seq_len), lambda b, x: (b, 0)),
                  pl.BlockSpec((seq_len, D_MODEL), lambda b, x: (s0 + b, 0))],
        out_specs=[pl.BlockSpec((epb, cap, D_MODEL), lambda b, x: (x, b, 0)),
                   pl.BlockSpec((epb, cap, LANES), lambda b, x: (x, b, 0))],
        out_shape=[jax.ShapeDtypeStruct((N_EXPERTS, n_seq * cap, D_MODEL), BF16),
                   jax.ShapeDtypeStruct((N_EXPERTS, n_seq * cap, LANES), F32)],
        compiler_params=_cparams("parallel", "arbitrary"),
        name="moe_gather",
    )(slot, aff, h2)


FFN_TF = 256


FFN_DOWN_TN = 512


def _ffn_body(xc_ref, xl_ref, gc_ref, gl_ref, wg_ref, wu_ref, wd_ref, yc_ref, yl_ref, hid_scr):
    f = pl.program_id(1)
    nf = EXPERT_FF // FFN_TF
    x = jnp.concatenate([xc_ref[0], xl_ref[0]], axis=0)
    gate = jnp.dot(x, wg_ref[0, 0].astype(BF16), preferred_element_type=F32)
    up = jnp.dot(x, wu_ref[0, 0].astype(BF16), preferred_element_type=F32)
    hid_scr[f] = (gate * jax.nn.sigmoid(gate) * up).astype(BF16)

    @pl.when(f == nf - 1)
    def _():
        hid = jnp.concatenate([hid_scr[k] for k in range(nf)], axis=1)
        g_c = gc_ref[0][:, :1]
        g_l = gl_ref[0][:, :1]
        for c in range(0, D_MODEL, FFN_DOWN_TN):
            out = jnp.dot(hid, wd_ref[0, 0, :, c:c + FFN_DOWN_TN].astype(BF16), preferred_element_type=F32)
            yc_ref[0, :, c:c + FFN_DOWN_TN] = (out[:ROWS_CTX, :] * g_c).astype(BF16)
            yl_ref[0, :, c:c + FFN_DOWN_TN] = (out[ROWS_CTX:, :] * g_l).astype(BF16)


def _expert_ffn(xe_c, xe_l, ge_c, ge_l, w_gate, w_up, w_down, layer):
    def xspec(rows, width):
        return pl.BlockSpec((1, rows, width), lambda i, f: (i, 0, 0))

    return pl.pallas_call(
        _ffn_body,
        grid=(N_EXPERTS, EXPERT_FF // FFN_TF),
        in_specs=[xspec(ROWS_CTX, D_MODEL), xspec(ROWS_LAT, D_MODEL),
                  xspec(ROWS_CTX, LANES), xspec(ROWS_LAT, LANES),
                  pl.BlockSpec((1, 1, D_MODEL, FFN_TF), lambda i, f: (layer, i, 0, f)),
                  pl.BlockSpec((1, 1, D_MODEL, FFN_TF), lambda i, f: (layer, i, 0, f)),
                  pl.BlockSpec((1, 1, EXPERT_FF, D_MODEL), lambda i, f: (layer, i, 0, 0))],
        out_specs=[xspec(ROWS_CTX, D_MODEL), xspec(ROWS_LAT, D_MODEL)],
        out_shape=[jax.ShapeDtypeStruct((N_EXPERTS, ROWS_CTX, D_MODEL), BF16),
                   jax.ShapeDtypeStruct((N_EXPERTS, ROWS_LAT, D_MODEL), BF16)],
        scratch_shapes=[pltpu.VMEM((EXPERT_FF // FFN_TF, ROWS_CTX + ROWS_LAT, FFN_TF), BF16)],
        compiler_params=_cparams("parallel", "arbitrary"),
        name="expert_ffn",
    )(xe_c, xe_l, ge_c, ge_l, w_gate, w_up, w_down)


CMB_TN = 512


def _combine_body(cap, slot_ref, y_ref, x_ref, gt_ref, o_ref, acc):
    n = slot_ref.shape[0]
    slot = slot_ref[...]
    if cap < LANES:
        assert cap & (cap - 1) == 0
        width = N_EXPERTS * cap
        lane_expert = lax.shift_right_logical(lax.broadcasted_iota(jnp.int32, (LANES, width), 1),
                                              cap.bit_length() - 1)
        spread = (lax.broadcasted_iota(jnp.int32, (LANES, width), 0) == lane_expert).astype(BF16)
        slot_x = jnp.dot(slot.astype(F32).astype(BF16), spread, preferred_element_type=F32)
        target = (lax.broadcasted_iota(jnp.int32, (n, width), 1) & (cap - 1)).astype(F32)
        onehot = (slot_x == target).astype(BF16)
        acc[...] = jnp.dot(onehot, y_ref[...].reshape(width, y_ref.shape[2]), preferred_element_type=F32)
    else:
        col = lax.broadcasted_iota(jnp.int32, (n, cap), 1)
        for x in range(N_EXPERTS):
            onehot = (slot[:, x:x + 1] == col).astype(BF16)
            part = jnp.dot(onehot, y_ref[x], preferred_element_type=F32)
            if x == 0:
                acc[...] = part
            else:
                acc[...] += part
    o_ref[...] = x_ref[...] + gt_ref[0] * acc[...]


def _combine(slot_t, y, x1, gt_t, n_seq, seq_len, cap, row0, tn):
    s0 = row0 // seq_len
    return pl.pallas_call(
        functools.partial(_combine_body, cap),
        grid=(n_seq, D_MODEL // tn),
        in_specs=[pl.BlockSpec((seq_len, LANES), lambda b, j: (b, 0)),
                  pl.BlockSpec((N_EXPERTS, cap, tn), lambda b, j: (0, b, j)),
                  pl.BlockSpec((seq_len, tn), lambda b, j: (s0 + b, j)),
                  pl.BlockSpec((1, 1, tn), lambda b, j: (b, 0, j))],
        out_specs=pl.BlockSpec((seq_len, tn), lambda b, j: (b, j)),
        out_shape=jax.ShapeDtypeStruct((n_seq * seq_len, D_MODEL), F32),
        scratch_shapes=[pltpu.VMEM((seq_len, tn), F32)],
        compiler_params=_cparams("parallel", "arbitrary"),
        name="moe_combine",
    )(slot_t, y, x1, gt_t)


CONV_TN = 256
CONV_HALO = 8
DT_LANE = ROPE_DIM


def _conv_body(x_ref, w_ref, b_ref, o_ref):
    seq_len = x_ref.shape[0]
    halo = jnp.zeros((CONV_HALO, x_ref.shape[1]), F32)
    ext = jnp.concatenate([halo, x_ref[...].astype(F32), halo], axis=0)
    w = w_ref[...]
    y = b_ref[...]
    for k in range(SSM_CONV):
        lo = CONV_HALO - SSM_CONV // 2 + k
        y = y + w[k:k + 1, :] * ext[lo:lo + seq_len, :]
    o_ref[...] = (y * jax.nn.sigmoid(y)).astype(BF16)


def _conv_silu(zb, conv_w8, conv_b, n_seq, seq_len, row0, tn):
    s0 = row0 // seq_len
    c0 = ZB_XBC // tn
    return pl.pallas_call(
        _conv_body,
        grid=(n_seq, SSM_XBC // tn),
        in_specs=[pl.BlockSpec((seq_len, tn), lambda b, j: (s0 + b, c0 + j)),
                  pl.BlockSpec((8, tn), lambda b, j: (0, j)),
                  pl.BlockSpec((1, tn), lambda b, j: (0, j))],
        out_specs=pl.BlockSpec((seq_len, tn), lambda b, j: (b, j)),
        out_shape=jax.ShapeDtypeStruct((n_seq * seq_len, SSM_XBC), BF16),
        compiler_params=_cparams("parallel", "arbitrary"),
        name="ssm_conv",
    )(zb, conv_w8, conv_b)


def _split3(x):
    hi = x.astype(BF16)
    r = x - hi.astype(F32)
    mid = r.astype(BF16)
    return hi, mid, (r - mid.astype(F32)).astype(BF16)


def _ssd_body(nc, xa_ref, zk_ref, z_ref, h0_ref, bias_ref, a_ref, d_ref, g_ref, ef_ref, eb_ref,
              o_ref, st_ref, hb_in, hf_cur, hb_cur):
    q = SSM_CHUNK
    half = SSM_INNER // SSM_GROUPS
    ii = lax.broadcasted_iota(jnp.int32, (q, q), 0)
    jj = lax.broadcasted_iota(jnp.int32, (q, q), 1)
    lower = ii >= jj
    upper = ii <= jj
    lower_b = lower.astype(BF16)
    upper_b = upper.astype(BF16)
    lane = lax.broadcasted_iota(jnp.int32, (q, LANES), 1)
    is_dt = (lane >= DT_LANE) & (lane < DT_LANE + 2 * SSM_HEADS)
    is_fwd = lane < DT_LANE + SSM_HEADS

    def tri_cumsum(tri, v):
        hi, mid, lo = _split3(v)
        return (jnp.dot(tri, hi, preferred_element_type=F32) + jnp.dot(tri, mid, preferred_element_type=F32)
                + jnp.dot(tri, lo, preferred_element_type=F32))

    def expand(v, e_ref):
        hi, lo = _split_bf16(v)
        return (jnp.dot(hi, e_ref[...], preferred_element_type=F32)
                + jnp.dot(lo, e_ref[...], preferred_element_type=F32))

    def chunk_factors(r0):
        dt = jnp.where(is_dt, jax.nn.softplus(zk_ref[pl.ds(r0, q), :] + bias_ref[...]), 0.0)
        dta = dt * a_ref[...]
        cum = jnp.where(is_fwd, tri_cumsum(lower_b, dta), tri_cumsum(upper_b, dta))
        tot = jnp.sum(dta, axis=0, keepdims=True)
        return dt, cum, tot

    def state_update(r0, dend_x, cdec_x, h_prev):
        x = xa_ref[pl.ds(r0, q), 0:SSM_INNER].astype(F32)
        xs = (x * dend_x).astype(BF16)
        parts = []
        for g in range(SSM_GROUPS):
            lo = SSM_INNER + g * SSM_STATE
            b_t = xa_ref[pl.ds(r0, q), lo:lo + SSM_STATE].astype(F32).T.astype(BF16)
            parts.append(jnp.dot(b_t, xs[:, g * half:(g + 1) * half], preferred_element_type=F32))
        return cdec_x * h_prev + jnp.concatenate(parts, axis=1)

    hf_cur[...] = h0_ref[0, 0]
    hb_cur[...] = h0_ref[0, 1]

    def bwd_step(t, carry):
        c = nc - 1 - t
        r0 = pl.multiple_of(c * q, q)
        dt, cum, tot = chunk_factors(r0)
        hb_in[c] = hb_cur[...]
        dend_x = expand(jnp.exp(tot - cum) * dt, eb_ref)
        cdec_x = expand(jnp.broadcast_to(jnp.exp(tot), (8, LANES)), eb_ref)[0:1]
        hb_cur[...] = state_update(r0, dend_x, cdec_x, hb_cur[...])
        return carry

    lax.fori_loop(0, nc, bwd_step, 0)

    def fwd_step(c, carry):
        r0 = pl.multiple_of(c * q, q)
        dt, cum, tot = chunk_factors(r0)
        cum_t = cum.T
        dt_t = dt.T
        eoff = jnp.exp(cum)
        x_bf = xa_ref[pl.ds(r0, q), 0:SSM_INNER]
        cb = []
        c_bf = []
        for g in range(SSM_GROUPS):
            lo_b = SSM_INNER + g * SSM_STATE
            lo_c = SSM_INNER + (SSM_GROUPS + g) * SSM_STATE
            c_g = xa_ref[pl.ds(r0, q), lo_c:lo_c + SSM_STATE]
            b_g = xa_ref[pl.ds(r0, q), lo_b:lo_b + SSM_STATE]
            c_bf.append(c_g)
            cb.append(lax.dot_general(c_g, b_g, (((1,), (1,)), ((), ())), preferred_element_type=F32))

        def head_matrix(h):
            f = DT_LANE + h
            b = DT_LANE + SSM_HEADS + h
            lf = jnp.where(lower, jnp.exp(cum[:, f:f + 1] - cum_t[f:f + 1, :]), 0.0) * dt_t[f:f + 1, :]
            lb = jnp.where(upper, jnp.exp(cum[:, b:b + 1] - cum_t[b:b + 1, :]), 0.0) * dt_t[b:b + 1, :]
            return (cb[h // (SSM_HEADS // SSM_GROUPS)] * (lf + lb)).astype(BF16)

        lane_lo = lane < SSM_HEADDIM
        pairs = []
        for hp in range(SSM_HEADS // 2):
            x_pair = x_bf[:, hp * LANES:(hp + 1) * LANES]
            y0 = jnp.dot(head_matrix(2 * hp), x_pair, preferred_element_type=F32)
            y1 = jnp.dot(head_matrix(2 * hp + 1), x_pair, preferred_element_type=F32)
            pairs.append(jnp.where(lane_lo, y0, y1))
        y = jnp.concatenate(pairs, axis=1)

        def off_diag(h_t, factor_x):
            h_bf = h_t.astype(BF16)
            parts = [jnp.dot(c_bf[g], h_bf[:, g * half:(g + 1) * half], preferred_element_type=F32)
                     for g in range(SSM_GROUPS)]
            return jnp.concatenate(parts, axis=1) * factor_x

        y = y + off_diag(hf_cur[...], expand(eoff, ef_ref)) + off_diag(hb_in[c], expand(eoff, eb_ref))
        y = y + d_ref[...] * x_bf.astype(F32)
        zg = z_ref[pl.ds(r0, q), :].astype(F32)
        o_ref[pl.ds(r0, q), :] = _rms(y * (zg * jax.nn.sigmoid(zg)), g_ref[...]).astype(BF16)

        dend_x = expand(jnp.exp(tot - cum) * dt, ef_ref)
        cdec_x = expand(jnp.broadcast_to(jnp.exp(tot), (8, LANES)), ef_ref)[0:1]
        hf_cur[...] = state_update(r0, dend_x, cdec_x, hf_cur[...])
        return carry

    lax.fori_loop(0, nc, fwd_step, 0)
    st_ref[0, 0] = hf_cur[...]
    st_ref[0, 1] = hb_cur[...]


def _ssd(xa, zk, zb, h0_t, dt_bias_p, a_p, d_x, g_ssm, e_f, e_b, n_seq, seq_len, row0):
    s0 = row0 // seq_len
    nc = seq_len // SSM_CHUNK
    vec = lambda w: pl.BlockSpec((1, w), lambda b: (0, 0))
    return pl.pallas_call(
        functools.partial(_ssd_body, nc),
        grid=(n_seq,),
        in_specs=[pl.BlockSpec((seq_len, SSM_XBC), lambda b: (b, 0)),
                  pl.BlockSpec((seq_len, ZK_WIDTH), lambda b: (s0 + b, 0)),
                  pl.BlockSpec((seq_len, SSM_INNER), lambda b: (s0 + b, ZB_Z // SSM_INNER)),
                  pl.BlockSpec((1, 2, SSM_STATE, SSM_INNER), lambda b: (b, 0, 0, 0)),
                  vec(LANES), vec(LANES), vec(SSM_INNER), vec(SSM_INNER),
                  pl.BlockSpec((LANES, SSM_INNER), lambda b: (0, 0)),
                  pl.BlockSpec((LANES, SSM_INNER), lambda b: (0, 0))],
        out_specs=[pl.BlockSpec((seq_len, SSM_INNER), lambda b: (b, 0)),
                   pl.BlockSpec((1, 2, SSM_STATE, SSM_INNER), lambda b: (b, 0, 0, 0))],
        out_shape=[jax.ShapeDtypeStruct((n_seq * seq_len, SSM_INNER), BF16),
                   jax.ShapeDtypeStruct((n_seq, 2, SSM_STATE, SSM_INNER), F32)],
        scratch_shapes=[pltpu.VMEM((nc, SSM_STATE, SSM_INNER), F32),
                        pltpu.VMEM((SSM_STATE, SSM_INNER), F32),
                        pltpu.VMEM((SSM_STATE, SSM_INNER), F32)],
        compiler_params=_cparams("parallel"),
        name="ssd",
    )(xa, zk, zb, h0_t, dt_bias_p, a_p, d_x, g_ssm, e_f, e_b)


GM_TM = 512
GM_GROUP_W = GM_WIDTH // GM_GROUPS


def _gmlp_body(u_ref, v_ref, g_ref, w_ref, b_ref, o_ref):
    for r in range(0, GM_TM, GM_CHUNK):
        u = jax.nn.gelu(u_ref[r:r + GM_CHUNK, :].astype(F32))
        vg = _rms(jax.nn.gelu(v_ref[r:r + GM_CHUNK, :].astype(F32)), g_ref[...]).astype(BF16)
        sv = jnp.concatenate(
            [jnp.dot(w_ref[k], vg[:, k * GM_GROUP_W:(k + 1) * GM_GROUP_W], preferred_element_type=F32)
             for k in range(GM_GROUPS)], axis=1)
        o_ref[r:r + GM_CHUNK, :] = (u * (sv + b_ref[...])).astype(BF16)


def _gmlp(zb, g_gv, w_sp, b_x):
    return pl.pallas_call(
        _gmlp_body,
        grid=(N_TOK // GM_TM,),
        in_specs=[pl.BlockSpec((GM_TM, GM_WIDTH), lambda i: (i, ZB_GM // GM_WIDTH)),
                  pl.BlockSpec((GM_TM, GM_WIDTH), lambda i: (i, ZB_GM // GM_WIDTH + 1)),
                  pl.BlockSpec((1, GM_WIDTH), lambda i: (0, 0)),
                  pl.BlockSpec((GM_GROUPS, GM_CHUNK, GM_CHUNK), lambda i: (0, 0, 0)),
                  pl.BlockSpec((GM_CHUNK, GM_WIDTH), lambda i: (0, 0))],
        out_specs=pl.BlockSpec((GM_TM, GM_WIDTH), lambda i: (i, 0)),
        out_shape=jax.ShapeDtypeStruct((N_TOK, GM_WIDTH), BF16),
        compiler_params=_cparams("parallel"),
        name="gmlp",
    )(zb, zb, g_gv, w_sp, b_x)


FN_TM = 512


def _final_norm_body(x_ref, g_ref, o_ref):
    o_ref[...] = _rms(x_ref[...], g_ref[...])


def _final_norm(x, g):
    rows = x.shape[0]
    return pl.pallas_call(
        _final_norm_body,
        grid=(rows // FN_TM,),
        in_specs=[pl.BlockSpec((FN_TM, D_MODEL), lambda i: (i, 0)),
                  pl.BlockSpec((1, D_MODEL), lambda i: (0, 0))],
        out_specs=pl.BlockSpec((FN_TM, D_MODEL), lambda i: (i, 0)),
        out_shape=jax.ShapeDtypeStruct((rows, D_MODEL), F32),
        compiler_params=_cparams("parallel"),
        name="final_norm",
    )(x, g)


def _dt_lanes(v):
    return jnp.pad(v.reshape(1, 2 * SSM_HEADS).astype(F32),
                   ((0, 0), (DT_LANE, LANES - DT_LANE - 2 * SSM_HEADS)))


def _head_expanders():
    lane = lax.broadcasted_iota(jnp.int32, (LANES, SSM_INNER), 0)
    head = lax.broadcasted_iota(jnp.int32, (LANES, SSM_INNER), 1) // SSM_HEADDIM
    e_f = (lane == head + DT_LANE).astype(BF16)
    e_b = (lane == head + DT_LANE + SSM_HEADS).astype(BF16)
    return e_f, e_b


MOD_TN = 1024
MOD_ROWS = 8
N_COND = 1 + DEC_BATCH


def _mod_body(ct_ref, w_ref, b_ref, o_ref):
    c = ct_ref[...]
    act = c * jax.nn.sigmoid(c)
    sub = lax.broadcasted_iota(jnp.int32, (MOD_ROWS, MOD_TN), 0)
    out = jnp.zeros((MOD_ROWS, MOD_TN), F32)
    for r in range(N_COND):
        y = jnp.sum(act[:, r:r + 1] * w_ref[0], axis=0, keepdims=True)
        out = jnp.where(sub == r, y, out)
    o_ref[0] = out + b_ref[0]


def _modulation(cond_t, w_mod, b_mod):
    n = 6 * D_MODEL
    return pl.pallas_call(
        _mod_body,
        grid=(DEPTH, n // MOD_TN),
        in_specs=[pl.BlockSpec((D_MODEL, MOD_ROWS), lambda l, j: (0, 0)),
                  pl.BlockSpec((1, D_MODEL, MOD_TN), lambda l, j: (l, 0, j)),
                  pl.BlockSpec((1, 1, MOD_TN), lambda l, j: (l, 0, j))],
        out_specs=pl.BlockSpec((1, MOD_ROWS, MOD_TN), lambda l, j: (l, 0, j)),
        out_shape=jax.ShapeDtypeStruct((DEPTH, MOD_ROWS, n), F32),
        compiler_params=_cparams("parallel", "arbitrary"),
        name="modulation",
    )(cond_t, w_mod, b_mod[:, None, :])


def _prep_w_uq(w):
    w = w.reshape(Q_RANK, MLA_HEADS, NOPE_DIM + ROPE_DIM)
    pad = jnp.zeros((Q_RANK, MLA_HEADS, QK_HEAD - NOPE_DIM - ROPE_DIM), w.dtype)
    return jnp.concatenate([w, pad], axis=2).reshape(Q_RANK, MLA_HEADS * QK_HEAD).astype(BF16)


def _prep_w_ukv(w):
    w = w.reshape(KV_RANK, MLA_HEADS, NOPE_DIM + V_DIM)
    return jnp.concatenate([w[:, :, :NOPE_DIM].reshape(KV_RANK, MLA_WIDTH),
                            w[:, :, NOPE_DIM:].reshape(KV_RANK, MLA_WIDTH)], axis=1).astype(BF16)


def _rope_tables(n_lat):
    rows = n_lat // GRID_W
    row = jnp.repeat(jnp.arange(rows), GRID_W).astype(F32)
    col = jnp.tile(jnp.arange(GRID_W), rows).astype(F32)
    nf = ROPE_DIM // 4
    freqs = jnp.power(ROPE_THETA, -jnp.arange(nf, dtype=F32) / nf)
    ang = jnp.stack([row[:, None] * freqs, col[:, None] * freqs], axis=1)
    cos, sin = jnp.cos(ang), jnp.sin(ang)
    c64 = jnp.concatenate([cos[:, 0], cos[:, 0], cos[:, 1], cos[:, 1]], axis=1)
    s64 = jnp.concatenate([-sin[:, 0], sin[:, 0], -sin[:, 1], sin[:, 1]], axis=1)
    c_lat = jnp.concatenate([c64, jnp.ones((n_lat, LANES - ROPE_DIM), F32)], axis=1)
    s_lat = jnp.concatenate([s64, jnp.zeros((n_lat, LANES - ROPE_DIM), F32)], axis=1)
    rope_c = jnp.concatenate([jnp.ones((N_CTX, LANES), F32), jnp.tile(c_lat, (DEC_BATCH, 1))], axis=0)
    rope_s = jnp.concatenate([jnp.zeros((N_CTX, LANES), F32), jnp.tile(s_lat, (DEC_BATCH, 1))], axis=0)
    return rope_c, rope_s


def _rows_mod(vec3, tm):
    idx = [0] * (N_CTX // tm) + [1] * (DEC_SEQ // tm) + [2] * (DEC_SEQ // tm)
    return vec3[jnp.array(idx)][:, None, :]


def kernel(x_prompt, x_sample, c, cache_ckv, cache_krope, state_ssm, c_ctx, w_mod, b_mod,
           g_norm1, g_norm2, w_in, g_qn, w_uq, g_kvn, w_ukv, conv_w, conv_b, dt_bias, a_log,
           d_skip, g_ssm, g_gv, w_sp, b_sp, w_br_attn, w_br_ssm, w_br_gmlp, w_out, w_router,
           w_gate, w_up, w_down, g_final):
    rope_c, rope_s = _rope_tables(DEC_SEQ)
    x_c = x_prompt.reshape(N_CTX, D_MODEL)
    x_l = x_sample.reshape(N_LAT, D_MODEL)
    cond_t = jnp.pad(jnp.concatenate([c_ctx[None, :], c], axis=0).T, ((0, 0), (0, MOD_ROWS - N_COND)))
    mod_all = _modulation(cond_t, w_mod, b_mod)
    e_f, e_b = _head_expanders()
    w_in_t = jnp.swapaxes(w_in, 1, 2)
    h0_c = jnp.zeros((BATCH, 2, SSM_STATE, SSM_INNER), F32)

    ckvs, krs, sts = [], [], []
    for l in range(DEPTH):
        mod = mod_all[l, :1 + DEC_BATCH].reshape(1 + DEC_BATCH, 6, D_MODEL)
        sh1, sc1, gt1, sh2, sc2, gt2 = [mod[:, k] for k in range(6)]

        zb, ckv_raw, zk = _inproj(x_c, x_l, g_norm1[l][None, :], _rows_mod(sc1, IN_TM), _rows_mod(sh1, IN_TM),
                                  w_in_t, l)

        q = _qproj(zb, g_qn[l][None, :], _prep_w_uq(w_uq[l]), rope_c, rope_s)
        w_ukv_p = _prep_w_ukv(w_ukv[l])
        ckv, kv, krot = _kvproj(ckv_raw, zk, g_kvn[l][None, :], w_ukv_p, rope_c, rope_s)
        kv_cache = _matmul(cache_ckv[:, l].reshape(DEC_BATCH * PAST_LEN, KV_RANK), w_ukv_p,
                           out_dtype=BF16, tn=2 * MLA_WIDTH, name="mla_kv_cache")
        kr_cache = jnp.pad(cache_krope[:, l].reshape(DEC_BATCH * PAST_LEN, ROPE_DIM),
                           ((0, 0), (0, LANES - ROPE_DIM))).astype(BF16)
        attn_c = _attention(q, kv, krot, BATCH, SEQ, SEQ, 0)
        attn_l = _attention(q, kv, krot, DEC_BATCH, DEC_SEQ, 512, N_CTX, cache=(kv_cache, kr_cache, PAST_LEN))

        conv_w8 = jnp.pad(conv_w[l], ((0, 8 - SSM_CONV), (0, 0)))
        ssd_par = (_dt_lanes(dt_bias[l]), _dt_lanes(-jnp.exp(a_log[l])),
                   jnp.repeat(d_skip[l], SSM_HEADDIM)[None, :], g_ssm[l][None, :], e_f, e_b)
        xa_c = _conv_silu(zb, conv_w8, conv_b[l][None, :], BATCH, SEQ, 0, SSM_XBC)
        xa_l = _conv_silu(zb, conv_w8, conv_b[l][None, :], DEC_BATCH, DEC_SEQ, N_CTX, CONV_TN)
        h0_l = jnp.transpose(state_ssm[:, l], (0, 1, 4, 2, 3)).reshape(DEC_BATCH, 2, SSM_STATE, SSM_INNER)
        ssm_c, st_c = _ssd(xa_c, zk, zb, h0_c, *ssd_par, BATCH, SEQ, 0)
        ssm_l, _ = _ssd(xa_l, zk, zb, h0_l, *ssd_par, DEC_BATCH, DEC_SEQ, N_CTX)
        gmo = _gmlp(zb, g_gv[l][None, :], w_sp[l].astype(BF16), jnp.repeat(b_sp[l].T, GM_GROUP_W, axis=1))
        st_c = jnp.transpose(st_c.reshape(BATCH, 2, SSM_STATE, SSM_HEADS, SSM_HEADDIM), (0, 1, 3, 4, 2))

        merged = _merge(attn_c, attn_l, ssm_c, ssm_l, gmo, w_br_attn, w_br_ssm, w_br_gmlp, zb, l)
        w_router_p = jnp.pad(w_router[l], ((0, 0), (0, LANES - N_EXPERTS)))
        x1, h2, logits = _outproj(merged, w_out[l].astype(BF16), x_c, x_l, _rows_mod(gt1, OUT_TM),
                                  g_norm2[l][None, :], _rows_mod(sc2, OUT_TM), _rows_mod(sh2, OUT_TM),
                                  w_router_p)

        slot_c, srow_c, arow_c = _select(logits, 0, BATCH, SEQ, CAP_CTX)
        slot_l, srow_l, arow_l = _select(logits, 1, DEC_BATCH, DEC_SEQ, CAP_LAT)
        xe_c, ge_c = _gather(srow_c, arow_c, h2, BATCH, SEQ, CAP_CTX, 0, N_EXPERTS)
        xe_l, ge_l = _gather(srow_l, arow_l, h2, DEC_BATCH, DEC_SEQ, CAP_LAT, N_CTX, 2)
        y_c, y_l = _expert_ffn(xe_c, xe_l, ge_c, ge_l, w_gate, w_up, w_down, l)
        gt2_c = jnp.broadcast_to(gt2[0][None, None, :], (BATCH, 1, D_MODEL))
        gt2_l = gt2[1:][:, None, :]
        x_c = _combine(slot_c, y_c, x1, gt2_c, BATCH, SEQ, CAP_CTX, 0, D_MODEL)
        x_l = _combine(slot_l, y_l, x1, gt2_l, DEC_BATCH, DEC_SEQ, CAP_LAT, N_CTX, CMB_TN)

        ckvs.append(ckv[:N_CTX].reshape(BATCH, SEQ, KV_RANK))
        krs.append(zk[:N_CTX, :ROPE_DIM].reshape(BATCH, SEQ, ROPE_DIM))
        sts.append(st_c)

    y_prompt = _final_norm(x_c, g_final[None, :]).reshape(BATCH, SEQ, D_MODEL)
    y_sample = _final_norm(x_l, g_final[None, :]).reshape(DEC_BATCH, DEC_SEQ, D_MODEL)
    return (y_prompt, y_sample, jnp.stack(ckvs, axis=1), jnp.stack(krs, axis=1), jnp.stack(sts, axis=1))
```
